```python
import jax, jax.numpy as jnp
from jax import lax
import numpy as np

D_MODEL = 1024
BATCH = 8
SEQ = 4096
DEPTH = 2

N_EVEN = (DEPTH + 1) // 2
N_ODD = DEPTH // 2

HEAD_DIM = 64
N_HEADS = D_MODEL // HEAD_DIM
N_KV_HEADS = 4
GROUP = N_HEADS // N_KV_HEADS
Q_WIDTH = N_HEADS * HEAD_DIM
KV_WIDTH = N_KV_HEADS * HEAD_DIM
QKV_WIDTH = Q_WIDTH + 2 * KV_WIDTH
WINDOW = 128
BLOCK = 128
NEG_INF = -1e30

POOL_WINDOWS = (2, 4, 8, 16)
N_POOL_GROUPS = len(POOL_WINDOWS)
POOL_GROUP_DIM = D_MODEL // N_POOL_GROUPS

D_FF = 2816
N_EXPERTS = 8
TOP_K = 2
D_FF_EXPERT = 3584

RMS_EPS = 1e-5

kernel_name = "hybrid_swa_sink_alibi_pool_moe_trunk"


def rms_norm(x, gain):
    xf = x.astype(jnp.float32)
    y = xf * lax.rsqrt(jnp.mean(xf * xf, axis=-1, keepdims=True) + RMS_EPS)
    return (y * gain.astype(jnp.float32)).astype(x.dtype)


def alibi_slopes(n_heads):
    return jnp.exp2(-8.0 * jnp.arange(1, n_heads + 1, dtype=jnp.float32) / n_heads)


def swiglu(h, w_gate, w_up, w_down):
    return (jax.nn.silu(h @ w_gate) * (h @ w_up)) @ w_down


def sliding_window_attention(h, w_qkv, sinks, w_o):
    B, S, _ = h.shape
    nb = S // BLOCK
    qkv = h @ w_qkv
    q, k, v = jnp.split(qkv, [Q_WIDTH, Q_WIDTH + KV_WIDTH], axis=-1)
    q = q.reshape(B, nb, BLOCK, N_KV_HEADS, GROUP, HEAD_DIM)
    k = k.reshape(B, nb, BLOCK, N_KV_HEADS, HEAD_DIM)
    v = v.reshape(B, nb, BLOCK, N_KV_HEADS, HEAD_DIM)

    def with_prev(t):
        prev = jnp.pad(t[:, :-1], ((0, 0), (1, 0), (0, 0), (0, 0), (0, 0)))
        return jnp.concatenate([prev, t], axis=2)

    kb, vb = with_prev(k), with_prev(v)
    s = jnp.einsum('bnqhgd,bnkhd->bhgnqk', q, kb,
                   preferred_element_type=jnp.float32) * (HEAD_DIM ** -0.5)

    blk = jnp.arange(nb)[:, None, None] * BLOCK
    q_pos = blk + jnp.arange(BLOCK)[None, :, None]
    k_pos = blk - BLOCK + jnp.arange(2 * BLOCK)[None, None, :]
    dist = q_pos - k_pos
    valid = (dist >= 0) & (dist < WINDOW) & (k_pos >= 0)
    slopes = alibi_slopes(N_HEADS).reshape(N_KV_HEADS, GROUP)
    bias = -slopes[:, :, None, None, None] * dist.astype(jnp.float32)[None, None]
    s = jnp.where(valid[None, None, None], s + bias[None], NEG_INF)

    sink = jnp.broadcast_to(sinks.astype(jnp.float32).reshape(1, N_KV_HEADS, GROUP, 1, 1, 1),
                            s.shape[:-1] + (1,))
    p = jax.nn.softmax(jnp.concatenate([s, sink], axis=-1), axis=-1)[..., :-1]
    o = jnp.einsum('bhgnqk,bnkhd->bnqhgd', p.astype(vb.dtype), vb)
    return o.reshape(B, S, Q_WIDTH) @ w_o


def pooling_mixer(h, w_pool, scale):
    B, S, D = h.shape
    groups = jnp.split(h, N_POOL_GROUPS, axis=-1)
    count = jnp.arange(1, S + 1, dtype=jnp.float32)[None, :, None]
    outs = []
    for g, w in zip(groups, POOL_WINDOWS):
        gf = g.astype(jnp.float32)
        c = lax.cumsum(gf, axis=1)
        lower = jnp.pad(c[:, :S - w], ((0, 0), (w, 0), (0, 0)))
        mean = (c - lower) / jnp.minimum(count, float(w))
        outs.append((mean - gf).astype(h.dtype))
    p = jnp.stack(outs, axis=2)
    y = jnp.einsum('bsgc,gcd->bsgd', p, w_pool).reshape(B, S, D)
    return y * scale


def moe_swiglu(h, w_router, w_gate, w_up, w_down):
    B, S, D = h.shape
    t = h.reshape(B * S, D)
    logits = jnp.dot(t, w_router, preferred_element_type=jnp.float32)
    top_val, top_idx = lax.top_k(logits, TOP_K)
    gates = jax.nn.softmax(top_val, axis=-1)
    combine = jnp.sum(jax.nn.one_hot(top_idx, N_EXPERTS, dtype=jnp.float32)
                      * gates[..., None], axis=1).astype(t.dtype)
    out = jnp.zeros_like(t)
    for e in range(N_EXPERTS):
        out = out + combine[:, e:e + 1] * swiglu(t, w_gate[e], w_up[e], w_down[e])
    return out.reshape(B, S, D)


def setup_inputs(seed: int = 0) -> dict:
    key = jax.random.key(seed)
    ks = jax.random.split(key, 24)
    f32 = jnp.float32
    nrm = lambda k, shape, scale: jax.random.normal(k, shape, f32) * scale
    gain = lambda k, shape: 1.0 + 0.05 * jax.random.normal(k, shape, f32)
    return {
        "x": jax.random.normal(ks[0], (BATCH, SEQ, D_MODEL), f32),
        "attn_norm": gain(ks[1], (N_EVEN, D_MODEL)),
        "attn_w_qkv": nrm(ks[2], (N_EVEN, D_MODEL, QKV_WIDTH), D_MODEL ** -0.5),
        "attn_sinks": nrm(ks[3], (N_EVEN, N_HEADS), 0.5),
        "attn_w_o": nrm(ks[4], (N_EVEN, Q_WIDTH, D_MODEL), Q_WIDTH ** -0.5),
        "ffn_norm": gain(ks[5], (N_EVEN, D_MODEL)),
        "ffn_w_gate": nrm(ks[6], (N_EVEN, D_MODEL, D_FF), D_MODEL ** -0.5),
        "ffn_w_up": nrm(ks[7], (N_EVEN, D_MODEL, D_FF), D_MODEL ** -0.5),
        "ffn_w_down": nrm(ks[8], (N_EVEN, D_FF, D_MODEL), D_FF ** -0.5),
        "pool_norm": gain(ks[9], (N_ODD, D_MODEL)),
        "pool_w": nrm(ks[10], (N_ODD, N_POOL_GROUPS, POOL_GROUP_DIM, POOL_GROUP_DIM), POOL_GROUP_DIM ** -0.5),
        "pool_scale": gain(ks[11], (N_ODD, D_MODEL)),
        "moe_norm": gain(ks[12], (N_ODD, D_MODEL)),
        "moe_w_router": nrm(ks[13], (N_ODD, D_MODEL, N_EXPERTS), D_MODEL ** -0.5),
        "moe_w_gate": nrm(ks[14], (N_ODD, N_EXPERTS, D_MODEL, D_FF_EXPERT), D_MODEL ** -0.5),
        "moe_w_up": nrm(ks[15], (N_ODD, N_EXPERTS, D_MODEL, D_FF_EXPERT), D_MODEL ** -0.5),
        "moe_w_down": nrm(ks[16], (N_ODD, N_EXPERTS, D_FF_EXPERT, D_MODEL), D_FF_EXPERT ** -0.5),
        "final_norm": gain(ks[17], (D_MODEL,)),
    }


def reference(x, attn_norm, attn_w_qkv, attn_sinks, attn_w_o,
              ffn_norm, ffn_w_gate, ffn_w_up, ffn_w_down,
              pool_norm, pool_w, pool_scale,
              moe_norm, moe_w_router, moe_w_gate, moe_w_up, moe_w_down,
              final_norm):
    for i in range(DEPTH):
        j = i // 2
        if i % 2 == 0:
            x = x + sliding_window_attention(rms_norm(x, attn_norm[j]), attn_w_qkv[j],
                                             attn_sinks[j], attn_w_o[j])
            x = x + swiglu(rms_norm(x, ffn_norm[j]), ffn_w_gate[j], ffn_w_up[j], ffn_w_down[j])
        else:
            x = x + pooling_mixer(rms_norm(x, pool_norm[j]), pool_w[j], pool_scale[j])
            x = x + moe_swiglu(rms_norm(x, moe_norm[j]), moe_w_router[j],
                               moe_w_gate[j], moe_w_up[j], moe_w_down[j])
    return rms_norm(x, final_norm)
```

```python
import functools

import jax
import jax.numpy as jnp
from jax import lax
from jax.experimental import pallas as pl
from jax.experimental.pallas import tpu as pltpu

F32 = jnp.float32
BF16 = jnp.bfloat16

HEAD_DIM = 64
N_KV_HEADS = 4
WINDOW = 128
POOL_WINDOWS = (2, 4, 8, 16)
TOP_K = 2
RMS_EPS = 1e-5
NEG_INF = -1e30

V7X_LANES = 128
V7X_MXU_DIM = 256
V7X_VMEM_LIMIT_BYTES = 56 * 1024 * 1024

QKV_ROWS = 512
ATTN_ROWS = 512
FFN_ROWS = 512
POOL_ROWS = 512
DISPATCH_ROWS = 1024
COMBINE_ROWS = 512
MOE_ROWS = 512
MOE_FF_SPLIT = 2
POOL_HALO = 16


def _params(*semantics):
    return pltpu.CompilerParams(dimension_semantics=semantics,
                                vmem_limit_bytes=V7X_VMEM_LIMIT_BYTES)


def _rms_norm(x, gain):
    ms = jnp.mean(x * x, axis=-1, keepdims=True)
    return x * lax.rsqrt(ms + RMS_EPS) * gain


def _silu(x):
    return x / (1.0 + jnp.exp(-x))


def _qkv_kernel(x_ref, g_ref, w_ref, o_ref):
    h = _rms_norm(x_ref[...], g_ref[...]).astype(BF16)
    o_ref[...] = jnp.dot(h, w_ref[...], preferred_element_type=F32).astype(BF16)


def _qkv_proj(x2d, gain, w_qkv):
    t, d = x2d.shape
    n = w_qkv.shape[1]
    return pl.pallas_call(
        _qkv_kernel,
        grid=(t // QKV_ROWS,),
        in_specs=[pl.BlockSpec((QKV_ROWS, d), lambda i: (i, 0)),
                  pl.BlockSpec((1, d), lambda i: (0, 0)),
                  pl.BlockSpec((d, n), lambda i: (0, 0))],
        out_specs=pl.BlockSpec((QKV_ROWS, n), lambda i: (i, 0)),
        out_shape=jax.ShapeDtypeStruct((t, n), BF16),
        compiler_params=_params("parallel"),
        name="rms_qkv",
    )(x2d, gain, w_qkv)


def _attn_kernel(sinks_ref, q_ref, kc_ref, kp_ref, vc_ref, vp_ref, x_ref, wo_ref, o_ref, oacc_ref,
                 *, n_heads):
    group = n_heads // N_KV_HEADS
    j = pl.program_id(1)
    row = lax.broadcasted_iota(jnp.int32, (WINDOW, 2 * WINDOW), 0)
    col = lax.broadcasted_iota(jnp.int32, (WINDOW, 2 * WINDOW), 1)
    dist = row + WINDOW - col
    in_window = (dist >= 0) & (dist < WINDOW)
    in_window_first = in_window & ((col >= WINDOW) | (j > 0))
    dist_f = dist.astype(F32)

    n_blocks = q_ref.shape[1] // WINDOW
    for blk in range(n_blocks):
        rows = slice(blk * WINDOW, (blk + 1) * WINDOW)
        q_blk = q_ref[0, rows, :] * jnp.asarray(HEAD_DIM ** -0.5, BF16)
        if blk == 0:
            k_prev, v_prev, valid = kp_ref[0], vp_ref[0], in_window_first
        else:
            prev = slice((blk - 1) * WINDOW, blk * WINDOW)
            k_prev, v_prev, valid = kc_ref[0, prev, :], vc_ref[0, prev, :], in_window
        k_cat = jnp.concatenate([k_prev, kc_ref[0, rows, :]], axis=0)
        v_cat = jnp.concatenate([v_prev, vc_ref[0, rows, :]], axis=0)
        outs = []
        for h in range(n_heads):
            kvh = h // group
            slope = 2.0 ** (-8.0 * (h + 1) / n_heads)
            q_h = q_blk[:, h * HEAD_DIM:(h + 1) * HEAD_DIM]
            k_h = k_cat[:, kvh * HEAD_DIM:(kvh + 1) * HEAD_DIM]
            v_h = v_cat[:, kvh * HEAD_DIM:(kvh + 1) * HEAD_DIM]
            s = lax.dot_general(q_h, k_h, (((1,), (1,)), ((), ())), preferred_element_type=F32)
            s = jnp.where(valid, s - slope * dist_f, NEG_INF)
            sink = sinks_ref[h]
            m = jnp.maximum(jnp.max(s, axis=-1, keepdims=True), sink)
            e = jnp.exp(s - m)
            denom = jnp.sum(e, axis=-1, keepdims=True) + jnp.exp(sink - m)
            pv = jnp.dot(e.astype(BF16), v_h, preferred_element_type=F32)
            outs.append(pv / denom)
        oacc_ref[rows, :] = jnp.concatenate(outs, axis=-1).astype(BF16)
    o_ref[0] = x_ref[0] + jnp.dot(oacc_ref[...], wo_ref[...], preferred_element_type=F32)


def _attention(x, qkv, sinks, w_o):
    b, s, d = x.shape
    n_heads = w_o.shape[0] // HEAD_DIM
    q_width = n_heads * HEAD_DIM
    kv_width = N_KV_HEADS * HEAD_DIM
    k_col = q_width // kv_width
    v_col = k_col + 1
    blocks_per_step = ATTN_ROWS // WINDOW

    def prev_block(bi, j):
        return jnp.maximum(j * blocks_per_step - 1, 0)

    kernel = functools.partial(_attn_kernel, n_heads=n_heads)
    return pl.pallas_call(
        kernel,
        grid=(b, s // ATTN_ROWS),
        in_specs=[
            pl.BlockSpec(memory_space=pltpu.SMEM),
            pl.BlockSpec((1, ATTN_ROWS, q_width), lambda bi, j: (bi, j, 0)),
            pl.BlockSpec((1, ATTN_ROWS, kv_width), lambda bi, j: (bi, j, k_col)),
            pl.BlockSpec((1, WINDOW, kv_width), lambda bi, j: (bi, prev_block(bi, j), k_col)),
            pl.BlockSpec((1, ATTN_ROWS, kv_width), lambda bi, j: (bi, j, v_col)),
            pl.BlockSpec((1, WINDOW, kv_width), lambda bi, j: (bi, prev_block(bi, j), v_col)),
            pl.BlockSpec((1, ATTN_ROWS, d), lambda bi, j: (bi, j, 0)),
            pl.BlockSpec((q_width, d), lambda bi, j: (0, 0)),
        ],
        out_specs=pl.BlockSpec((1, ATTN_ROWS, d), lambda bi, j: (bi, j, 0)),
        out_shape=jax.ShapeDtypeStruct((b, s, d), F32),
        scratch_shapes=[pltpu.VMEM((ATTN_ROWS, q_width), BF16)],
        compiler_params=_params("parallel", "parallel"),
        name="swa_attention",
    )(sinks, qkv, qkv, qkv, qkv, qkv, x, w_o)


def _ffn_kernel(x_ref, g_ref, wg_ref, wu_ref, wd_ref, o_ref, act_ref):
    x = x_ref[...]
    h = _rms_norm(x, g_ref[...]).astype(BF16)
    d_ff = wg_ref.shape[1]
    for c in range(d_ff // V7X_MXU_DIM):
        cols = slice(c * V7X_MXU_DIM, (c + 1) * V7X_MXU_DIM)
        gate = jnp.dot(h, wg_ref[:, cols], preferred_element_type=F32)
        up = jnp.dot(h, wu_ref[:, cols], preferred_element_type=F32)
        act_ref[:, cols] = (_silu(gate) * up).astype(BF16)
    o_ref[...] = x + jnp.dot(act_ref[...], wd_ref[...], preferred_element_type=F32)


def _dense_ffn(x2d, gain, w_gate, w_up, w_down):
    t, d = x2d.shape
    d_ff = w_gate.shape[1]
    resident = dict(pipeline_mode=pl.Buffered(1))
    return pl.pallas_call(
        _ffn_kernel,
        grid=(t // FFN_ROWS,),
        in_specs=[pl.BlockSpec((FFN_ROWS, d), lambda i: (i, 0)),
                  pl.BlockSpec((1, d), lambda i: (0, 0)),
                  pl.BlockSpec((d, d_ff), lambda i: (0, 0), **resident),
                  pl.BlockSpec((d, d_ff), lambda i: (0, 0), **resident),
                  pl.BlockSpec((d_ff, d), lambda i: (0, 0), **resident)],
        out_specs=pl.BlockSpec((FFN_ROWS, d), lambda i: (i, 0)),
        out_shape=jax.ShapeDtypeStruct((t, d), F32),
        scratch_shapes=[pltpu.VMEM((FFN_ROWS, d_ff), BF16)],
        compiler_params=_params("parallel"),
        name="rms_swiglu",
    )(x2d, gain, w_gate, w_up, w_down)


ROUTE_E1, ROUTE_E2, ROUTE_G1, ROUTE_G2, ROUTE_R1, ROUTE_R2 = range(6)


def _pool_route_kernel(x_ref, halo_ref, pn_ref, ps_ref, mn_ref, wp_ref, wr_ref,
                       x3_ref, h3_ref, route_ref, counts_ref, running_ref, *, n_experts):
    bi, j = pl.program_id(0), pl.program_id(1)
    rows = x_ref.shape[1]

    @pl.when((bi == 0) & (j == 0))
    def _():
        running_ref[...] = jnp.zeros_like(running_ref)

    x = x_ref[0]
    h = _rms_norm(x, pn_ref[...])
    halo = jnp.where(j > 0, _rms_norm(halo_ref[0], pn_ref[...]), 0.0)
    y = jnp.concatenate([halo, h], axis=0)
    pos = j * rows + lax.broadcasted_iota(jnp.int32, (rows, 1), 0)
    gdim = wp_ref.shape[1]
    mixed = []
    for g, w in enumerate(POOL_WINDOWS):
        cols = slice(g * gdim, (g + 1) * gdim)
        acc = y[:, cols]
        span = 1
        while span < w:
            acc = acc + pltpu.roll(acc, span, axis=0)
            span *= 2
        count = jnp.minimum(pos + 1, w).astype(F32)
        p = acc[POOL_HALO:, :] / count - h[:, cols]
        mixed.append(jnp.dot(p.astype(BF16), wp_ref[g], preferred_element_type=F32))
    x3 = x + jnp.concatenate(mixed, axis=-1) * ps_ref[...]
    x3_ref[0] = x3

    h3 = _rms_norm(x3, mn_ref[...])
    h3_ref[0] = h3
    lane = lax.broadcasted_iota(jnp.int32, (rows, V7X_LANES), 1)
    logits = jnp.dot(h3, wr_ref[...], preferred_element_type=F32, precision=lax.Precision.HIGHEST)
    logits = jnp.where(lane < n_experts, logits, -jnp.inf)
    v1 = jnp.max(logits, axis=-1, keepdims=True)
    e1 = jnp.min(jnp.where(logits == v1, lane, V7X_LANES), axis=-1, keepdims=True)
    rest = jnp.where(lane == e1, -jnp.inf, logits)
    v2 = jnp.max(rest, axis=-1, keepdims=True)
    e2 = jnp.min(jnp.where(rest == v2, lane, V7X_LANES), axis=-1, keepdims=True)
    ex = jnp.exp(v2 - v1)
    g1 = 1.0 / (1.0 + ex)
    g2 = ex / (1.0 + ex)

    chosen = ((lane == e1) | (lane == e2)).astype(BF16)
    r_i = lax.broadcasted_iota(jnp.int32, (rows, rows), 0)
    c_i = lax.broadcasted_iota(jnp.int32, (rows, rows), 1)
    earlier = (c_i < r_i).astype(BF16)
    rank = running_ref[...] + jnp.dot(earlier, chosen, preferred_element_type=F32)
    r1 = jnp.sum(jnp.where(lane == e1, rank, 0.0), axis=-1, keepdims=True)
    r2 = jnp.sum(jnp.where(lane == e2, rank, 0.0), axis=-1, keepdims=True)
    total = running_ref[...] + jnp.sum(chosen.astype(F32), axis=0, keepdims=True)
    running_ref[...] = total
    counts_ref[...] = total

    rec = jnp.zeros((rows, V7X_LANES), F32)
    for idx, val in ((ROUTE_E1, e1.astype(F32)), (ROUTE_E2, e2.astype(F32)), (ROUTE_G1, g1),
                     (ROUTE_G2, g2), (ROUTE_R1, r1), (ROUTE_R2, r2)):
        rec = jnp.where(lane == idx, val, rec)
    route_ref[0] = rec


def _pool_route(x, pool_norm, pool_scale, moe_norm, w_pool, w_router_padded, n_experts):
    b, s, d = x.shape
    halo_blocks = POOL_ROWS // POOL_HALO
    vec = pl.BlockSpec((1, d), lambda bi, j: (0, 0))
    tile = pl.BlockSpec((1, POOL_ROWS, d), lambda bi, j: (bi, j, 0))
    kernel = functools.partial(_pool_route_kernel, n_experts=n_experts)
    return pl.pallas_call(
        kernel,
        grid=(b, s // POOL_ROWS),
        in_specs=[tile,
                  pl.BlockSpec((1, POOL_HALO, d),
                               lambda bi, j: (bi, jnp.maximum(j * halo_blocks - 1, 0), 0)),
                  vec, vec, vec,
                  pl.BlockSpec(w_pool.shape, lambda bi, j: (0, 0, 0)),
                  pl.BlockSpec(w_router_padded.shape, lambda bi, j: (0, 0))],
        out_specs=[tile, tile,
                   pl.BlockSpec((1, POOL_ROWS, V7X_LANES), lambda bi, j: (bi, j, 0)),
                   pl.BlockSpec((1, V7X_LANES), lambda bi, j: (0, 0))],
        out_shape=[jax.ShapeDtypeStruct((b, s, d), F32),
                   jax.ShapeDtypeStruct((b, s, d), F32),
                   jax.ShapeDtypeStruct((b, s, V7X_LANES), F32),
                   jax.ShapeDtypeStruct((1, V7X_LANES), F32)],
        scratch_shapes=[pltpu.VMEM((1, V7X_LANES), F32)],
        compiler_params=_params("arbitrary", "arbitrary"),
        name="pool_route",
    )(x, x, pool_norm, pool_scale, moe_norm, w_pool, w_router_padded)


def _dispatch_kernel(slots_ref, h_ref, xs_ref, sem):
    rows = h_ref.shape[0]

    def row_copy(r, k):
        slot = slots_ref[0, 0, TOP_K * r + k]
        return pltpu.make_async_copy(h_ref.at[pl.ds(r, 1)], xs_ref.at[pl.ds(slot, 1)], sem)

    def issue(r, carry):
        for k in range(TOP_K):
            row_copy(r, k).start()
        return carry

    lax.fori_loop(0, rows, issue, 0, unroll=8)
    for _ in range(TOP_K):
        pltpu.make_async_copy(h_ref, xs_ref.at[pl.ds(0, rows)], sem).wait()


def _dispatch(h2d, slots):
    t, d = h2d.shape
    n_tiles = t // DISPATCH_ROWS
    slots3 = slots.reshape(n_tiles, 1, DISPATCH_ROWS * TOP_K)
    return pl.pallas_call(
        _dispatch_kernel,
        grid=(n_tiles,),
        in_specs=[pl.BlockSpec((1, 1, DISPATCH_ROWS * TOP_K), lambda i: (i, 0, 0),
                               memory_space=pltpu.SMEM),
                  pl.BlockSpec((DISPATCH_ROWS, d), lambda i: (i, 0))],
        out_specs=pl.BlockSpec(memory_space=pl.ANY),
        out_shape=jax.ShapeDtypeStruct((t * TOP_K, d), F32),
        scratch_shapes=[pltpu.SemaphoreType.DMA(())],
        compiler_params=_params("arbitrary"),
        name="moe_dispatch",
    )(slots3, h2d)


def _moe_kernel(tile_ref, expert_ref, lo_ref, hi_ref, x_ref, wg_ref, wu_ref, wd_ref, o_ref,
                xb_ref, act_ref):
    v, f = pl.program_id(0), pl.program_id(1)
    rows = x_ref.shape[0]
    lo, hi = lo_ref[v], hi_ref[v]
    first_visit = (v == 0) | (tile_ref[v] != tile_ref[jnp.maximum(v - 1, 0)])

    @pl.when(first_visit & (f == 0))
    def _():
        o_ref[...] = jnp.zeros_like(o_ref)

    @pl.when(hi > lo)
    def _():
        @pl.when(f == 0)
        def _():
            xb_ref[...] = x_ref[...].astype(BF16)

        xb = xb_ref[...]
        ff = wg_ref.shape[2]
        for c in range(ff // V7X_MXU_DIM):
            cols = slice(c * V7X_MXU_DIM, (c + 1) * V7X_MXU_DIM)
            gate = jnp.dot(xb, wg_ref[0, :, cols], preferred_element_type=F32)
            up = jnp.dot(xb, wu_ref[0, :, cols], preferred_element_type=F32)
            act_ref[:, cols] = (_silu(gate) * up).astype(BF16)
        row = tile_ref[v] * rows + lax.broadcasted_iota(jnp.int32, (rows, 1), 0)
        mine = (row >= lo) & (row < hi)
        y = jnp.dot(act_ref[...], wd_ref[0], preferred_element_type=F32)
        o_ref[...] += jnp.where(mine, y, 0.0)


def _moe_experts(xs, visits, w_gate, w_up, w_down):
    n_rows, d = xs.shape
    n_experts, _, d_ff = w_gate.shape
    ff = d_ff // MOE_FF_SPLIT
    n_visits = visits[0].shape[0]
    last_f = MOE_FF_SPLIT - 1

    def f_eff(v, f, lo, hi):
        return jnp.where(hi[v] > lo[v], f, last_f)

    grid_spec = pltpu.PrefetchScalarGridSpec(
        num_scalar_prefetch=4,
        grid=(n_visits, MOE_FF_SPLIT),
        in_specs=[
            pl.BlockSpec((MOE_ROWS, d), lambda v, f, tile, ex, lo, hi: (tile[v], 0)),
            pl.BlockSpec((1, d, ff), lambda v, f, tile, ex, lo, hi: (ex[v], 0, f_eff(v, f, lo, hi))),
            pl.BlockSpec((1, d, ff), lambda v, f, tile, ex, lo, hi: (ex[v], 0, f_eff(v, f, lo, hi))),
            pl.BlockSpec((1, ff, d), lambda v, f, tile, ex, lo, hi: (ex[v], f_eff(v, f, lo, hi), 0)),
        ],
        out_specs=pl.BlockSpec((MOE_ROWS, d), lambda v, f, tile, ex, lo, hi: (tile[v], 0)),
        scratch_shapes=[pltpu.VMEM((MOE_ROWS, d), BF16), pltpu.VMEM((MOE_ROWS, ff), BF16)],
    )
    return pl.pallas_call(
        _moe_kernel,
        grid_spec=grid_spec,
        out_shape=jax.ShapeDtypeStruct((n_rows, d), F32),
        compiler_params=_params("arbitrary", "arbitrary"),
        name="moe_experts",
    )(*visits, xs, w_gate, w_up, w_down)


def _plan_visits(counts, n_rows):
    n_experts = counts.shape[0]
    n_tiles = n_rows // MOE_ROWS
    ends = jnp.cumsum(counts)
    starts = ends - counts
    tile_starts = jnp.arange(n_tiles, dtype=jnp.int32) * MOE_ROWS
    needs_cut = (counts > 0) & (starts % MOE_ROWS != 0)
    cuts = jnp.where(needs_cut, starts, n_rows)
    lo = jnp.sort(jnp.concatenate([tile_starts, cuts]))
    hi = jnp.concatenate([lo[1:], jnp.full((1,), n_rows, jnp.int32)])
    last_row = jnp.minimum(lo, n_rows - 1)
    tile = last_row // MOE_ROWS
    expert = jnp.sum(ends[None, :] <= last_row[:, None], axis=1).astype(jnp.int32)
    expert = jnp.minimum(expert, n_experts - 1)
    return tile.astype(jnp.int32), expert, lo.astype(jnp.int32), hi.astype(jnp.int32)


def _combine_kernel(slots_ref, next_slots_ref, x_ref, route_ref, g_ref, ys_ref, o_ref, buf_ref, sem):
    i, n = pl.program_id(0), pl.num_programs(0)
    rows = x_ref.shape[0]

    def issue(slots, which):
        def body(r, carry):
            for k in range(TOP_K):
                slot = slots[0, 0, TOP_K * r + k]
                pltpu.make_async_copy(ys_ref.at[pl.ds(slot, 1)],
                                      buf_ref.at[which, k, pl.ds(r, 1)], sem.at[which]).start()
            return carry
        lax.fori_loop(0, rows, body, 0, unroll=8)

    cur = i % 2

    @pl.when(i == 0)
    def _():
        issue(slots_ref, 0)

    @pl.when(i + 1 < n)
    def _():
        issue(next_slots_ref, 1 - cur)

    for k in range(TOP_K):
        pltpu.make_async_copy(ys_ref.at[pl.ds(0, rows)], buf_ref.at[cur, k], sem.at[cur]).wait()

    route = route_ref[...]
    g1 = route[:, ROUTE_G1:ROUTE_G1 + 1]
    g2 = route[:, ROUTE_G2:ROUTE_G2 + 1]
    y = x_ref[...] + (g1 * buf_ref[cur, 0] + g2 * buf_ref[cur, 1])
    o_ref[...] = _rms_norm(y, g_ref[...])


def _combine(x2d, route2d, slots, ys, final_norm):
    t, d = x2d.shape
    n_tiles = t // COMBINE_ROWS
    slots3 = slots.reshape(n_tiles, 1, COMBINE_ROWS * TOP_K)
    slot_block = (1, 1, COMBINE_ROWS * TOP_K)
    return pl.pallas_call(
        _combine_kernel,
        grid=(n_tiles,),
        in_specs=[pl.BlockSpec(slot_block, lambda i: (i, 0, 0), memory_space=pltpu.SMEM),
                  pl.BlockSpec(slot_block, lambda i: (jnp.minimum(i + 1, n_tiles - 1), 0, 0),
                               memory_space=pltpu.SMEM),
                  pl.BlockSpec((COMBINE_ROWS, d), lambda i: (i, 0)),
                  pl.BlockSpec((COMBINE_ROWS, V7X_LANES), lambda i: (i, 0)),
                  pl.BlockSpec((1, d), lambda i: (0, 0)),
                  pl.BlockSpec(memory_space=pl.ANY)],
        out_specs=pl.BlockSpec((COMBINE_ROWS, d), lambda i: (i, 0)),
        out_shape=jax.ShapeDtypeStruct((t, d), F32),
        scratch_shapes=[pltpu.VMEM((2, TOP_K, COMBINE_ROWS, d), F32),
                        pltpu.SemaphoreType.DMA((2,))],
        compiler_params=_params("arbitrary"),
        name="moe_combine",
    )(slots3, slots3, x2d, route2d, final_norm, ys)


def kernel(x, attn_norm, attn_w_qkv, attn_sinks, attn_w_o, ffn_norm, ffn_w_gate, ffn_w_up,
           ffn_w_down, pool_norm, pool_w, pool_scale, moe_norm, moe_w_router, moe_w_gate, moe_w_up,
           moe_w_down, final_norm):
    b, s, d = x.shape
    t = b * s
    n_experts = moe_w_router.shape[-1]
    assert s % ATTN_ROWS == 0 and s % POOL_ROWS == 0 and t % DISPATCH_ROWS == 0
    assert attn_norm.shape[0] == 1 and pool_norm.shape[0] == 1

    qkv = _qkv_proj(x.reshape(t, d), attn_norm[0][None], attn_w_qkv[0].astype(BF16))
    x1 = _attention(x, qkv.reshape(b, s, -1), attn_sinks[0].astype(F32), attn_w_o[0].astype(BF16))
    x2 = _dense_ffn(x1.reshape(t, d), ffn_norm[0][None], ffn_w_gate[0].astype(BF16),
                    ffn_w_up[0].astype(BF16), ffn_w_down[0].astype(BF16))

    w_router = jnp.pad(moe_w_router[0], ((0, 0), (0, V7X_LANES - n_experts)))
    x3, h3, route, counts = _pool_route(x2.reshape(b, s, d), pool_norm[0][None], pool_scale[0][None],
                                        moe_norm[0][None], pool_w[0].astype(BF16), w_router, n_experts)
    route2d = route.reshape(t, V7X_LANES)
    counts = counts[0, :n_experts].astype(jnp.int32)
    starts = jnp.cumsum(counts) - counts
    experts = route2d[:, ROUTE_E1:ROUTE_E2 + 1].astype(jnp.int32)
    ranks = route2d[:, ROUTE_R1:ROUTE_R2 + 1].astype(jnp.int32)
    slots = (starts[experts] + ranks).reshape(t * TOP_K)

    xs = _dispatch(h3.reshape(t, d), slots)
    visits = _plan_visits(counts, t * TOP_K)
    ys = _moe_experts(xs, visits, moe_w_gate[0].astype(BF16), moe_w_up[0].astype(BF16),
                      moe_w_down[0].astype(BF16))
    out = _combine(x3.reshape(t, d), route2d, slots, ys, final_norm[None])
    return out.reshape(b, s, d)
```

```python
import functools

import jax
import jax.numpy as jnp
from jax import lax
from jax.experimental import pallas as pl
from jax.experimental.pallas import tpu as pltpu

F32 = jnp.float32
BF16 = jnp.bfloat16

HEAD_DIM = 64
N_KV_HEADS = 4
WINDOW = 128
POOL_WINDOWS = (2, 4, 8, 16)
TOP_K = 2
RMS_EPS = 1e-5
NEG_INF = -1e30

V7X_LANES = 128
V7X_MXU_DIM = 256
V7X_VMEM_LIMIT_BYTES = 56 * 1024 * 1024

QKV_ROWS = 512
ATTN_ROWS = 512
FFN_ROWS = 512
POOL_ROWS = 512
DISPATCH_ROWS = 1024
COMBINE_ROWS = 512
MOE_ROWS = 512
MOE_FF_SPLIT = 2
POOL_HALO = 16


def _params(*semantics):
    return pltpu.CompilerParams(dimension_semantics=semantics,
                                vmem_limit_bytes=V7X_VMEM_LIMIT_BYTES)


def _rms_norm(x, gain):
    ms = jnp.mean(x * x, axis=-1, keepdims=True)
    return x * lax.rsqrt(ms + RMS_EPS) * gain


def _silu(x):
    return x / (1.0 + jnp.exp(-x))


def _qkv_kernel(x_ref, g_ref, w_ref, o_ref):
    h = _rms_norm(x_ref[...], g_ref[...]).astype(BF16)
    o_ref[...] = jnp.dot(h, w_ref[...], preferred_element_type=F32).astype(BF16)


def _qkv_proj(x2d, gain, w_qkv):
    t, d = x2d.shape
    n = w_qkv.shape[1]
    return pl.pallas_call(
        _qkv_kernel,
        grid=(t // QKV_ROWS,),
        in_specs=[pl.BlockSpec((QKV_ROWS, d), lambda i: (i, 0)),
                  pl.BlockSpec((1, d), lambda i: (0, 0)),
                  pl.BlockSpec((d, n), lambda i: (0, 0))],
        out_specs=pl.BlockSpec((QKV_ROWS, n), lambda i: (i, 0)),
        out_shape=jax.ShapeDtypeStruct((t, n), BF16),
        compiler_params=_params("parallel"),
        name="rms_qkv",
    )(x2d, gain, w_qkv)


def _attn_kernel(bias_ref, sink_ref, q_ref, kc_ref, kp_ref, vc_ref, vp_ref, x_ref, wo_ref, o_ref,
                 ot_ref, *, n_heads):
    group = n_heads // N_KV_HEADS
    gw = group * WINDOW
    j = pl.program_id(1)
    key = lax.broadcasted_iota(jnp.int32, (WINDOW, n_heads * WINDOW), 0)
    query = lax.broadcasted_iota(jnp.int32, (WINDOW, n_heads * WINDOW), 1) & (WINDOW - 1)
    from_prev = key > query
    from_prev_bf = from_prev.astype(BF16)
    no_prev = from_prev & (j == 0)
    contract_last = (((1,), (1,)), ((), ()))
    contract_first = (((0,), (0,)), ((), ()))
    sink = sink_ref[...]

    n_blocks = q_ref.shape[1] // WINDOW
    for blk in range(n_blocks):
        rows = slice(blk * WINDOW, (blk + 1) * WINDOW)
        q_blk = q_ref[0, rows, :] * jnp.asarray(HEAD_DIM ** -0.5, BF16)
        if blk == 0:
            k_prev, v_prev = kp_ref[0], vp_ref[0]
        else:
            prev = slice((blk - 1) * WINDOW, blk * WINDOW)
            k_prev, v_prev = kc_ref[0, prev, :], vc_ref[0, prev, :]
        k_cat = jnp.concatenate([k_prev, kc_ref[0, rows, :]], axis=0)
        v_cat = jnp.concatenate([v_prev, vc_ref[0, rows, :]], axis=0)
        scores = []
        for kvh in range(N_KV_HEADS):
            q_g = jnp.concatenate([q_blk[:, h * HEAD_DIM:(h + 1) * HEAD_DIM]
                                   for h in range(kvh * group, (kvh + 1) * group)], axis=0)
            k_h = k_cat[:, kvh * HEAD_DIM:(kvh + 1) * HEAD_DIM]
            scores.append(lax.dot_general(k_h, q_g, contract_last, preferred_element_type=F32))
        s_all = jnp.concatenate(scores, axis=1)
        s = jnp.where(from_prev, s_all[:WINDOW], s_all[WINDOW:]) + bias_ref[...]
        if blk == 0:
            s = jnp.where(no_prev, NEG_INF, s)
        m = jnp.maximum(jnp.max(s, axis=0, keepdims=True), sink)
        e = jnp.exp(s - m)
        inv_denom = 1.0 / (jnp.sum(e, axis=0, keepdims=True) + jnp.exp(sink - m))
        e_bf = e.astype(BF16)
        p_prev = e_bf * from_prev_bf
        p_cat = jnp.concatenate([p_prev, e_bf - p_prev], axis=0)
        for kvh in range(N_KV_HEADS):
            lanes = slice(kvh * gw, (kvh + 1) * gw)
            v_h = v_cat[:, kvh * HEAD_DIM:(kvh + 1) * HEAD_DIM]
            o_t = lax.dot_general(v_h, p_cat[:, lanes], contract_first,
                                  preferred_element_type=F32)
            o_t = (o_t * inv_denom[:, lanes]).astype(BF16)
            for g in range(group):
                h = kvh * group + g
                ot_ref[h * HEAD_DIM:(h + 1) * HEAD_DIM, rows] = o_t[:, g * WINDOW:(g + 1) * WINDOW]
    o_ref[0] = x_ref[0] + lax.dot_general(ot_ref[...], wo_ref[...], contract_first,
                                          preferred_element_type=F32)


def _alibi_band_bias(n_heads):
    slopes = jnp.exp2(-8.0 * jnp.arange(1, n_heads + 1, dtype=F32) / n_heads)
    c = jnp.arange(WINDOW)[:, None, None]
    r = jnp.arange(WINDOW)[None, None, :]
    dist = ((r - c) % WINDOW).astype(F32)
    return (-slopes[None, :, None] * dist).reshape(WINDOW, n_heads * WINDOW)


def _attention(x, qkv, sinks, w_o):
    b, s, d = x.shape
    n_heads = w_o.shape[0] // HEAD_DIM
    q_width = n_heads * HEAD_DIM
    kv_width = N_KV_HEADS * HEAD_DIM
    k_col = q_width // kv_width
    v_col = k_col + 1
    blocks_per_step = ATTN_ROWS // WINDOW
    sink_row = jnp.repeat(sinks, WINDOW)[None, :]

    def prev_block(bi, j):
        return jnp.maximum(j * blocks_per_step - 1, 0)

    kernel = functools.partial(_attn_kernel, n_heads=n_heads)
    return pl.pallas_call(
        kernel,
        grid=(b, s // ATTN_ROWS),
        in_specs=[
            pl.BlockSpec((WINDOW, n_heads * WINDOW), lambda bi, j: (0, 0)),
            pl.BlockSpec((1, n_heads * WINDOW), lambda bi, j: (0, 0)),
            pl.BlockSpec((1, ATTN_ROWS, q_width), lambda bi, j: (bi, j, 0)),
            pl.BlockSpec((1, ATTN_ROWS, kv_width), lambda bi, j: (bi, j, k_col)),
            pl.BlockSpec((1, WINDOW, kv_width), lambda bi, j: (bi, prev_block(bi, j), k_col)),
            pl.BlockSpec((1, ATTN_ROWS, kv_width), lambda bi, j: (bi, j, v_col)),
            pl.BlockSpec((1, WINDOW, kv_width), lambda bi, j: (bi, prev_block(bi, j), v_col)),
            pl.BlockSpec((1, ATTN_ROWS, d), lambda bi, j: (bi, j, 0)),
            pl.BlockSpec((q_width, d), lambda bi, j: (0, 0)),
        ],
        out_specs=pl.BlockSpec((1, ATTN_ROWS, d), lambda bi, j: (bi, j, 0)),
        out_shape=jax.ShapeDtypeStruct((b, s, d), F32),
        scratch_shapes=[pltpu.VMEM((q_width, ATTN_ROWS), BF16)],
        compiler_params=_params("parallel", "parallel"),
        name="swa_attention",
    )(_alibi_band_bias(n_heads), sink_row, qkv, qkv, qkv, qkv, qkv, x, w_o)


def _ffn_kernel(x_ref, g_ref, wg_ref, wu_ref, wd_ref, o_ref, act_ref):
    x = x_ref[...]
    h = _rms_norm(x, g_ref[...]).astype(BF16)
    d_ff = wg_ref.shape[1]
    for c in range(d_ff // V7X_MXU_DIM):
        cols = slice(c * V7X_MXU_DIM, (c + 1) * V7X_MXU_DIM)
        gate = jnp.dot(h, wg_ref[:, cols], preferred_element_type=F32)
        up = jnp.dot(h, wu_ref[:, cols], preferred_element_type=F32)
        act_ref[:, cols] = (_silu(gate) * up).astype(BF16)
    o_ref[...] = x + jnp.dot(act_ref[...], wd_ref[...], preferred_element_type=F32)


def _dense_ffn(x2d, gain, w_gate, w_up, w_down):
    t, d = x2d.shape
    d_ff = w_gate.shape[1]
    resident = dict(pipeline_mode=pl.Buffered(1))
    return pl.pallas_call(
        _ffn_kernel,
        grid=(t // FFN_ROWS,),
        in_specs=[pl.BlockSpec((FFN_ROWS, d), lambda i: (i, 0)),
                  pl.BlockSpec((1, d), lambda i: (0, 0)),
                  pl.BlockSpec((d, d_ff), lambda i: (0, 0), **resident),
                  pl.BlockSpec((d, d_ff), lambda i: (0, 0), **resident),
                  pl.BlockSpec((d_ff, d), lambda i: (0, 0), **resident)],
        out_specs=pl.BlockSpec((FFN_ROWS, d), lambda i: (i, 0)),
        out_shape=jax.ShapeDtypeStruct((t, d), F32),
        scratch_shapes=[pltpu.VMEM((FFN_ROWS, d_ff), BF16)],
        compiler_params=_params("parallel"),
        name="rms_swiglu",
    )(x2d, gain, w_gate, w_up, w_down)


ROUTE_E1, ROUTE_E2, ROUTE_G1, ROUTE_G2, ROUTE_R1, ROUTE_R2 = range(6)


def _pool_route_kernel(x_ref, halo_ref, pn_ref, ps_ref, mn_ref, wp_ref, wr_ref,
                       x3_ref, h3_ref, route_ref, counts_ref, running_ref, *, n_experts):
    bi, j = pl.program_id(0), pl.program_id(1)
    rows = x_ref.shape[1]

    @pl.when((bi == 0) & (j == 0))
    def _():
        running_ref[...] = jnp.zeros_like(running_ref)

    x = x_ref[0]
    h = _rms_norm(x, pn_ref[...])
    halo = jnp.where(j > 0, _rms_norm(halo_ref[0], pn_ref[...]), 0.0)
    y = jnp.concatenate([halo, h], axis=0)
    pos = j * rows + lax.broadcasted_iota(jnp.int32, (rows, 1), 0)
    gdim = wp_ref.shape[1]
    mixed = []
    for g, w in enumerate(POOL_WINDOWS):
        cols = slice(g * gdim, (g + 1) * gdim)
        acc = y[:, cols]
        span = 1
        while span < w:
            acc = acc + pltpu.roll(acc, span, axis=0)
            span *= 2
        count = jnp.minimum(pos + 1, w).astype(F32)
        p = acc[POOL_HALO:, :] / count - h[:, cols]
        mixed.append(jnp.dot(p.astype(BF16), wp_ref[g], preferred_element_type=F32))
    x3 = x + jnp.concatenate(mixed, axis=-1) * ps_ref[...]
    x3_ref[0] = x3

    h3 = _rms_norm(x3, mn_ref[...])
    h3_ref[0] = h3
    lane = lax.broadcasted_iota(jnp.int32, (rows, V7X_LANES), 1)
    logits = jnp.dot(h3, wr_ref[...], preferred_element_type=F32, precision=lax.Precision.HIGHEST)
    logits = jnp.where(lane < n_experts, logits, -jnp.inf)
    v1 = jnp.max(logits, axis=-1, keepdims=True)
    e1 = jnp.min(jnp.where(logits == v1, lane, V7X_LANES), axis=-1, keepdims=True)
    rest = jnp.where(lane == e1, -jnp.inf, logits)
    v2 = jnp.max(rest, axis=-1, keepdims=True)
    e2 = jnp.min(jnp.where(rest == v2, lane, V7X_LANES), axis=-1, keepdims=True)
    ex = jnp.exp(v2 - v1)
    g1 = 1.0 / (1.0 + ex)
    g2 = ex / (1.0 + ex)

    chosen = ((lane == e1) | (lane == e2)).astype(BF16)
    r_i = lax.broadcasted_iota(jnp.int32, (rows, rows), 0)
    c_i = lax.broadcasted_iota(jnp.int32, (rows, rows), 1)
    earlier = (c_i < r_i).astype(BF16)
    rank = running_ref[...] + jnp.dot(earlier, chosen, preferred_element_type=F32)
    r1 = jnp.sum(jnp.where(lane == e1, rank, 0.0), axis=-1, keepdims=True)
    r2 = jnp.sum(jnp.where(lane == e2, rank, 0.0), axis=-1, keepdims=True)
    total = running_ref[...] + jnp.sum(chosen.astype(F32), axis=0, keepdims=True)
    running_ref[...] = total
    counts_ref[...] = total

    rec = jnp.zeros((rows, V7X_LANES), F32)
    for idx, val in ((ROUTE_E1, e1.astype(F32)), (ROUTE_E2, e2.astype(F32)), (ROUTE_G1, g1),
                     (ROUTE_G2, g2), (ROUTE_R1, r1), (ROUTE_R2, r2)):
        rec = jnp.where(lane == idx, val, rec)
    route_ref[0] = rec


def _pool_route(x, pool_norm, pool_scale, moe_norm, w_pool, w_router_padded, n_experts):
    b, s, d = x.shape
    halo_blocks = POOL_ROWS // POOL_HALO
    vec = pl.BlockSpec((1, d), lambda bi, j: (0, 0))
    tile = pl.BlockSpec((1, POOL_ROWS, d), lambda bi, j: (bi, j, 0))
    kernel = functools.partial(_pool_route_kernel, n_experts=n_experts)
    return pl.pallas_call(
        kernel,
        grid=(b, s // POOL_ROWS),
        in_specs=[tile,
                  pl.BlockSpec((1, POOL_HALO, d),
                               lambda bi, j: (bi, jnp.maximum(j * halo_blocks - 1, 0), 0)),
                  vec, vec, vec,
                  pl.BlockSpec(w_pool.shape, lambda bi, j: (0, 0, 0)),
                  pl.BlockSpec(w_router_padded.shape, lambda bi, j: (0, 0))],
        out_specs=[tile, tile,
                   pl.BlockSpec((1, POOL_ROWS, V7X_LANES), lambda bi, j: (bi, j, 0)),
                   pl.BlockSpec((1, V7X_LANES), lambda bi, j: (0, 0))],
        out_shape=[jax.ShapeDtypeStruct((b, s, d), F32),
                   jax.ShapeDtypeStruct((b, s, d), F32),
                   jax.ShapeDtypeStruct((b, s, V7X_LANES), F32),
                   jax.ShapeDtypeStruct((1, V7X_LANES), F32)],
        scratch_shapes=[pltpu.VMEM((1, V7X_LANES), F32)],
        compiler_params=_params("arbitrary", "arbitrary"),
        name="pool_route",
    )(x, x, pool_norm, pool_scale, moe_norm, w_pool, w_router_padded)


def _dispatch_kernel(slots_ref, h_ref, xs_ref, sem):
    rows = h_ref.shape[0]

    def row_copy(r, k):
        slot = slots_ref[0, 0, TOP_K * r + k]
        return pltpu.make_async_copy(h_ref.at[pl.ds(r, 1)], xs_ref.at[pl.ds(slot, 1)], sem)

    def issue(r, carry):
        for k in range(TOP_K):
            row_copy(r, k).start()
        return carry

    lax.fori_loop(0, rows, issue, 0, unroll=8)
    for _ in range(TOP_K):
        pltpu.make_async_copy(h_ref, xs_ref.at[pl.ds(0, rows)], sem).wait()


def _dispatch(h2d, slots):
    t, d = h2d.shape
    n_tiles = t // DISPATCH_ROWS
    slots3 = slots.reshape(n_tiles, 1, DISPATCH_ROWS * TOP_K)
    return pl.pallas_call(
        _dispatch_kernel,
        grid=(n_tiles,),
        in_specs=[pl.BlockSpec((1, 1, DISPATCH_ROWS * TOP_K), lambda i: (i, 0, 0),
                               memory_space=pltpu.SMEM),
                  pl.BlockSpec((DISPATCH_ROWS, d), lambda i: (i, 0))],
        out_specs=pl.BlockSpec(memory_space=pl.ANY),
        out_shape=jax.ShapeDtypeStruct((t * TOP_K, d), F32),
        scratch_shapes=[pltpu.SemaphoreType.DMA(())],
        compiler_params=_params("arbitrary"),
        name="moe_dispatch",
    )(slots3, h2d)


def _moe_kernel(tile_ref, expert_ref, lo_ref, hi_ref, x_ref, wg_ref, wu_ref, wd_ref, o_ref,
                xb_ref, act_ref):
    v, f = pl.program_id(0), pl.program_id(1)
    rows = x_ref.shape[0]
    lo, hi = lo_ref[v], hi_ref[v]
    first_visit = (v == 0) | (tile_ref[v] != tile_ref[jnp.maximum(v - 1, 0)])

    @pl.when(first_visit & (f == 0))
    def _():
        o_ref[...] = jnp.zeros_like(o_ref)

    @pl.when(hi > lo)
    def _():
        @pl.when(f == 0)
        def _():
            xb_ref[...] = x_ref[...].astype(BF16)

        xb = xb_ref[...]
        ff = wg_ref.shape[2]
        for c in range(ff // V7X_MXU_DIM):
            cols = slice(c * V7X_MXU_DIM, (c + 1) * V7X_MXU_DIM)
            gate = jnp.dot(xb, wg_ref[0, :, cols], preferred_element_type=F32)
            up = jnp.dot(xb, wu_ref[0, :, cols], preferred_element_type=F32)
            act_ref[:, cols] = (_silu(gate) * up).astype(BF16)
        row = tile_ref[v] * rows + lax.broadcasted_iota(jnp.int32, (rows, 1), 0)
        mine = (row >= lo) & (row < hi)
        y = jnp.dot(act_ref[...], wd_ref[0], preferred_element_type=F32)
        o_ref[...] += jnp.where(mine, y, 0.0)


def _moe_experts(xs, visits, w_gate, w_up, w_down):
    n_rows, d = xs.shape
    n_experts, _, d_ff = w_gate.shape
    ff = d_ff // MOE_FF_SPLIT
    n_visits = visits[0].shape[0]
    last_f = MOE_FF_SPLIT - 1

    def f_eff(v, f, lo, hi):
        return jnp.where(hi[v] > lo[v], f, last_f)

    grid_spec = pltpu.PrefetchScalarGridSpec(
        num_scalar_prefetch=4,
        grid=(n_visits, MOE_FF_SPLIT),
        in_specs=[
            pl.BlockSpec((MOE_ROWS, d), lambda v, f, tile, ex, lo, hi: (tile[v], 0)),
            pl.BlockSpec((1, d, ff), lambda v, f, tile, ex, lo, hi: (ex[v], 0, f_eff(v, f, lo, hi))),
            pl.BlockSpec((1, d, ff), lambda v, f, tile, ex, lo, hi: (ex[v], 0, f_eff(v, f, lo, hi))),
            pl.BlockSpec((1, ff, d), lambda v, f, tile, ex, lo, hi: (ex[v], f_eff(v, f, lo, hi), 0)),
        ],
        out_specs=pl.BlockSpec((MOE_ROWS, d), lambda v, f, tile, ex, lo, hi: (tile[v], 0)),
        scratch_shapes=[pltpu.VMEM((MOE_ROWS, d), BF16), pltpu.VMEM((MOE_ROWS, ff), BF16)],
    )
    return pl.pallas_call(
        _moe_kernel,
        grid_spec=grid_spec,
        out_shape=jax.ShapeDtypeStruct((n_rows, d), F32),
        compiler_params=_params("arbitrary", "arbitrary"),
        name="moe_experts",
    )(*visits, xs, w_gate, w_up, w_down)


def _plan_visits(counts, n_rows):
    n_experts = counts.shape[0]
    n_tiles = n_rows // MOE_ROWS
    ends = jnp.cumsum(counts)
    starts = ends - counts
    tile_starts = jnp.arange(n_tiles, dtype=jnp.int32) * MOE_ROWS
    needs_cut = (counts > 0) & (starts % MOE_ROWS != 0)
    cuts = jnp.where(needs_cut, starts, n_rows)
    lo = jnp.sort(jnp.concatenate([tile_starts, cuts]))
    hi = jnp.concatenate([lo[1:], jnp.full((1,), n_rows, jnp.int32)])
    last_row = jnp.minimum(lo, n_rows - 1)
    tile = last_row // MOE_ROWS
    expert = jnp.sum(ends[None, :] <= last_row[:, None], axis=1).astype(jnp.int32)
    expert = jnp.minimum(expert, n_experts - 1)
    return tile.astype(jnp.int32), expert, lo.astype(jnp.int32), hi.astype(jnp.int32)


def _combine_kernel(slots_ref, next_slots_ref, x_ref, route_ref, g_ref, ys_ref, o_ref, buf_ref, sem):
    i, n = pl.program_id(0), pl.num_programs(0)
    rows = x_ref.shape[0]

    def issue(slots, which):
        def body(r, carry):
            for k in range(TOP_K):
                slot = slots[0, 0, TOP_K * r + k]
                pltpu.make_async_copy(ys_ref.at[pl.ds(slot, 1)],
                                      buf_ref.at[which, k, pl.ds(r, 1)], sem.at[which]).start()
            return carry
        lax.fori_loop(0, rows, body, 0, unroll=8)

    cur = i % 2

    @pl.when(i == 0)
    def _():
        issue(slots_ref, 0)

    @pl.when(i + 1 < n)
    def _():
        issue(next_slots_ref, 1 - cur)

    for k in range(TOP_K):
        pltpu.make_async_copy(ys_ref.at[pl.ds(0, rows)], buf_ref.at[cur, k], sem.at[cur]).wait()

    route = route_ref[...]
    g1 = route[:, ROUTE_G1:ROUTE_G1 + 1]
    g2 = route[:, ROUTE_G2:ROUTE_G2 + 1]
    y = x_ref[...] + (g1 * buf_ref[cur, 0] + g2 * buf_ref[cur, 1])
    o_ref[...] = _rms_norm(y, g_ref[...])


def _combine(x2d, route2d, slots, ys, final_norm):
    t, d = x2d.shape
    n_tiles = t // COMBINE_ROWS
    slots3 = slots.reshape(n_tiles, 1, COMBINE_ROWS * TOP_K)
    slot_block = (1, 1, COMBINE_ROWS * TOP_K)
    return pl.pallas_call(
        _combine_kernel,
        grid=(n_tiles,),
        in_specs=[pl.BlockSpec(slot_block, lambda i: (i, 0, 0), memory_space=pltpu.SMEM),
                  pl.BlockSpec(slot_block, lambda i: (jnp.minimum(i + 1, n_tiles - 1), 0, 0),
                               memory_space=pltpu.SMEM),
                  pl.BlockSpec((COMBINE_ROWS, d), lambda i: (i, 0)),
                  pl.BlockSpec((COMBINE_ROWS, V7X_LANES), lambda i: (i, 0)),
                  pl.BlockSpec((1, d), lambda i: (0, 0)),
                  pl.BlockSpec(memory_space=pl.ANY)],
        out_specs=pl.BlockSpec((COMBINE_ROWS, d), lambda i: (i, 0)),
        out_shape=jax.ShapeDtypeStruct((t, d), F32),
        scratch_shapes=[pltpu.VMEM((2, TOP_K, COMBINE_ROWS, d), F32),
                        pltpu.SemaphoreType.DMA((2,))],
        compiler_params=_params("arbitrary"),
        name="moe_combine",
    )(slots3, slots3, x2d, route2d, final_norm, ys)


def kernel(x, attn_norm, attn_w_qkv, attn_sinks, attn_w_o, ffn_norm, ffn_w_gate, ffn_w_up,
           ffn_w_down, pool_norm, pool_w, pool_scale, moe_norm, moe_w_router, moe_w_gate, moe_w_up,
           moe_w_down, final_norm):
    b, s, d = x.shape
    t = b * s
    n_experts = moe_w_router.shape[-1]
    assert s % ATTN_ROWS == 0 and s % POOL_ROWS == 0 and t % DISPATCH_ROWS == 0
    assert attn_norm.shape[0] == 1 and pool_norm.shape[0] == 1

    qkv = _qkv_proj(x.reshape(t, d), attn_norm[0][None], attn_w_qkv[0].astype(BF16))
    x1 = _attention(x, qkv.reshape(b, s, -1), attn_sinks[0].astype(F32), attn_w_o[0].astype(BF16))
    x2 = _dense_ffn(x1.reshape(t, d), ffn_norm[0][None], ffn_w_gate[0].astype(BF16),
                    ffn_w_up[0].astype(BF16), ffn_w_down[0].astype(BF16))

    w_router = jnp.pad(moe_w_router[0], ((0, 0), (0, V7X_LANES - n_experts)))
    x3, h3, route, counts = _pool_route(x2.reshape(b, s, d), pool_norm[0][None], pool_scale[0][None],
                                        moe_norm[0][None], pool_w[0].astype(BF16), w_router, n_experts)
    route2d = route.reshape(t, V7X_LANES)
    counts = counts[0, :n_experts].astype(jnp.int32)
    starts = jnp.cumsum(counts) - counts
    experts = route2d[:, ROUTE_E1:ROUTE_E2 + 1].astype(jnp.int32)
    ranks = route2d[:, ROUTE_R1:ROUTE_R2 + 1].astype(jnp.int32)
    slots = (starts[experts] + ranks).reshape(t * TOP_K)

    xs = _dispatch(h3.reshape(t, d), slots)
    visits = _plan_visits(counts, t * TOP_K)
    ys = _moe_experts(xs, visits, moe_w_gate[0].astype(BF16), moe_w_up[0].astype(BF16),
                      moe_w_down[0].astype(BF16))
    out = _combine(x3.reshape(t, d), route2d, slots, ys, final_norm[None])
    return out.reshape(b, s, d)
```

```python
import functools

import jax
import jax.numpy as jnp
from jax import lax
from jax.experimental import pallas as pl
from jax.experimental.pallas import tpu as pltpu

F32 = jnp.float32
BF16 = jnp.bfloat16
U32 = jnp.uint32

HEAD_DIM = 64
N_KV_HEADS = 4
WINDOW = 128
POOL_WINDOWS = (2, 4, 8, 16)
TOP_K = 2
RMS_EPS = 1e-5
NEG_INF = -1e30

V7X_LANES = 128
V7X_SUBLANES = 8
V7X_MXU_DIM = 256
V7X_VMEM_LIMIT_BYTES = 56 * 1024 * 1024

QKV_ROWS = 1024
ATTN_ROWS = 512
FFN_ROWS = 1024
ROUTE_ROWS = 512
MOE_ROWS = 512
MOE_FF_SPLIT = 2
POOL_HALO = 16

RUN_ALIGN = V7X_SUBLANES
RUN_ALIGN_BITS = RUN_ALIGN.bit_length() - 1
ROUTE_SUBLANES = 16
ROUTE_L1, ROUTE_L2, ROUTE_G1, ROUTE_G2, ROUTE_E1, ROUTE_E2 = range(6)
INFO_BASE, INFO_COUNT, INFO_LSTART = range(3)
TABLE_WIDTH = 32


def _params(*semantics):
    return pltpu.CompilerParams(dimension_semantics=semantics,
                                vmem_limit_bytes=V7X_VMEM_LIMIT_BYTES)


def _rms_norm(x, gain):
    ms = jnp.mean(x * x, axis=-1, keepdims=True)
    return x * lax.rsqrt(ms + RMS_EPS) * gain


def _silu(x):
    return x / (1.0 + jnp.exp(-x))


def _split_bf16(x, pieces):
    out = []
    for _ in range(pieces - 1):
        top = x.astype(BF16)
        out.append(top)
        x = x - top.astype(F32)
    out.append(x.astype(BF16))
    return out


def _pack_bf16_pairs(x):
    half = x.shape[1] // 2
    bits = pltpu.bitcast(x, U32)
    return (bits[:, :half] >> 16) | bits[:, half:]


def _unpack_bf16_pairs(w):
    lo = pltpu.bitcast(w << 16, F32)
    hi = pltpu.bitcast(w & jnp.uint32(0xFFFF0000), F32)
    return jnp.concatenate([lo, hi], axis=1)


CONTRACT_LAST = (((1,), (1,)), ((), ()))
CONTRACT_FIRST = (((0,), (0,)), ((), ()))


def _qkv_kernel(x_ref, g_ref, w_ref, o_ref):
    h = _rms_norm(x_ref[...], g_ref[...]).astype(BF16)
    o_ref[...] = jnp.dot(h, w_ref[...], preferred_element_type=F32).astype(BF16)


def _qkv_proj(x2d, gain, w_qkv):
    t, d = x2d.shape
    n = w_qkv.shape[1]
    return pl.pallas_call(
        _qkv_kernel,
        grid=(t // QKV_ROWS,),
        in_specs=[pl.BlockSpec((QKV_ROWS, d), lambda i: (i, 0)),
                  pl.BlockSpec((1, d), lambda i: (0, 0)),
                  pl.BlockSpec((d, n), lambda i: (0, 0))],
        out_specs=pl.BlockSpec((QKV_ROWS, n), lambda i: (i, 0)),
        out_shape=jax.ShapeDtypeStruct((t, n), BF16),
        compiler_params=_params("parallel"),
        name="rms_qkv",
    )(x2d, gain, w_qkv)


def _attn_kernel(bias_ref, sink_ref, q_ref, kc_ref, kp_ref, vc_ref, vp_ref, x_ref, wo_ref, o_ref,
                 ot_ref, *, n_heads):
    group = n_heads // N_KV_HEADS
    gw = group * WINDOW
    j = pl.program_id(1)
    key = lax.broadcasted_iota(jnp.int32, (WINDOW, n_heads * WINDOW), 0)
    query = lax.broadcasted_iota(jnp.int32, (WINDOW, n_heads * WINDOW), 1) & (WINDOW - 1)
    from_prev = key > query
    from_prev_bf = from_prev.astype(BF16)
    no_prev = from_prev & (j == 0)
    sink = sink_ref[...]

    n_blocks = q_ref.shape[1] // WINDOW
    for blk in range(n_blocks):
        rows = slice(blk * WINDOW, (blk + 1) * WINDOW)
        q_blk = q_ref[0, rows, :] * jnp.asarray(HEAD_DIM ** -0.5, BF16)
        if blk == 0:
            k_prev, v_prev = kp_ref[0], vp_ref[0]
        else:
            prev = slice((blk - 1) * WINDOW, blk * WINDOW)
            k_prev, v_prev = kc_ref[0, prev, :], vc_ref[0, prev, :]
        k_cat = jnp.concatenate([k_prev, kc_ref[0, rows, :]], axis=0)
        v_cat = jnp.concatenate([v_prev, vc_ref[0, rows, :]], axis=0)
        scores = []
        for kvh in range(N_KV_HEADS):
            q_g = jnp.concatenate([q_blk[:, h * HEAD_DIM:(h + 1) * HEAD_DIM]
                                   for h in range(kvh * group, (kvh + 1) * group)], axis=0)
            k_h = k_cat[:, kvh * HEAD_DIM:(kvh + 1) * HEAD_DIM]
            scores.append(lax.dot_general(k_h, q_g, CONTRACT_LAST, preferred_element_type=F32))
        s_all = jnp.concatenate(scores, axis=1)
        s = jnp.where(from_prev, s_all[:WINDOW], s_all[WINDOW:]) + bias_ref[...]
        if blk == 0:
            s = jnp.where(no_prev, NEG_INF, s)
        m = jnp.maximum(jnp.max(s, axis=0, keepdims=True), sink)
        e = jnp.exp(s - m)
        inv_denom = 1.0 / (jnp.sum(e, axis=0, keepdims=True) + jnp.exp(sink - m))
        e_bf = e.astype(BF16)
        p_prev = e_bf * from_prev_bf
        p_cat = jnp.concatenate([p_prev, e_bf - p_prev], axis=0)
        for kvh in range(N_KV_HEADS):
            lanes = slice(kvh * gw, (kvh + 1) * gw)
            v_h = v_cat[:, kvh * HEAD_DIM:(kvh + 1) * HEAD_DIM]
            o_t = lax.dot_general(v_h, p_cat[:, lanes], CONTRACT_FIRST,
                                  preferred_element_type=F32)
            o_t = (o_t * inv_denom[:, lanes]).astype(BF16)
            for g in range(group):
                h = kvh * group + g
                ot_ref[h * HEAD_DIM:(h + 1) * HEAD_DIM, rows] = o_t[:, g * WINDOW:(g + 1) * WINDOW]
    o_ref[0] = x_ref[0] + lax.dot_general(ot_ref[...], wo_ref[...], CONTRACT_FIRST,
                                          preferred_element_type=F32)


def _alibi_band_bias(n_heads):
    slopes = jnp.exp2(-8.0 * jnp.arange(1, n_heads + 1, dtype=F32) / n_heads)
    c = jnp.arange(WINDOW)[:, None, None]
    r = jnp.arange(WINDOW)[None, None, :]
    dist = ((r - c) % WINDOW).astype(F32)
    return (-slopes[None, :, None] * dist).reshape(WINDOW, n_heads * WINDOW)


def _attention(x, qkv, sinks, w_o):
    b, s, d = x.shape
    n_heads = w_o.shape[0] // HEAD_DIM
    q_width = n_heads * HEAD_DIM
    kv_width = N_KV_HEADS * HEAD_DIM
    k_col = q_width // kv_width
    v_col = k_col + 1
    blocks_per_step = ATTN_ROWS // WINDOW
    sink_row = jnp.repeat(sinks, WINDOW)[None, :]

    def prev_block(bi, j):
        return jnp.maximum(j * blocks_per_step - 1, 0)

    kernel = functools.partial(_attn_kernel, n_heads=n_heads)
    return pl.pallas_call(
        kernel,
        grid=(b, s // ATTN_ROWS),
        in_specs=[
            pl.BlockSpec((WINDOW, n_heads * WINDOW), lambda bi, j: (0, 0)),
            pl.BlockSpec((1, n_heads * WINDOW), lambda bi, j: (0, 0)),
            pl.BlockSpec((1, ATTN_ROWS, q_width), lambda bi, j: (bi, j, 0)),
            pl.BlockSpec((1, ATTN_ROWS, kv_width), lambda bi, j: (bi, j, k_col)),
            pl.BlockSpec((1, WINDOW, kv_width), lambda bi, j: (bi, prev_block(bi, j), k_col)),
            pl.BlockSpec((1, ATTN_ROWS, kv_width), lambda bi, j: (bi, j, v_col)),
            pl.BlockSpec((1, WINDOW, kv_width), lambda bi, j: (bi, prev_block(bi, j), v_col)),
            pl.BlockSpec((1, ATTN_ROWS, d), lambda bi, j: (bi, j, 0)),
            pl.BlockSpec((q_width, d), lambda bi, j: (0, 0)),
        ],
        out_specs=pl.BlockSpec((1, ATTN_ROWS, d), lambda bi, j: (bi, j, 0)),
        out_shape=jax.ShapeDtypeStruct((b, s, d), F32),
        scratch_shapes=[pltpu.VMEM((q_width, ATTN_ROWS), BF16)],
        compiler_params=_params("parallel", "parallel"),
        name="swa_attention",
    )(_alibi_band_bias(n_heads), sink_row, qkv, qkv, qkv, qkv, qkv, x, w_o)


def _ffn_kernel(x_ref, g_ref, wg_ref, wu_ref, wd_ref, o_ref, act_ref):
    x = x_ref[...]
    h = _rms_norm(x, g_ref[...]).astype(BF16)
    d_ff = wg_ref.shape[1]
    for c in range(d_ff // V7X_MXU_DIM):
        cols = slice(c * V7X_MXU_DIM, (c + 1) * V7X_MXU_DIM)
        gate = jnp.dot(h, wg_ref[:, cols], preferred_element_type=F32)
        up = jnp.dot(h, wu_ref[:, cols], preferred_element_type=F32)
        act_ref[:, cols] = (_silu(gate) * up).astype(BF16)
    o_ref[...] = x + jnp.dot(act_ref[...], wd_ref[...], preferred_element_type=F32)


def _dense_ffn(x2d, gain, w_gate, w_up, w_down):
    t, d = x2d.shape
    d_ff = w_gate.shape[1]
    resident = dict(pipeline_mode=pl.Buffered(1))
    return pl.pallas_call(
        _ffn_kernel,
        grid=(t // FFN_ROWS,),
        in_specs=[pl.BlockSpec((FFN_ROWS, d), lambda i: (i, 0)),
                  pl.BlockSpec((1, d), lambda i: (0, 0)),
                  pl.BlockSpec((d, d_ff), lambda i: (0, 0), **resident),
                  pl.BlockSpec((d, d_ff), lambda i: (0, 0), **resident),
                  pl.BlockSpec((d_ff, d), lambda i: (0, 0), **resident)],
        out_specs=pl.BlockSpec((FFN_ROWS, d), lambda i: (i, 0)),
        out_shape=jax.ShapeDtypeStruct((t, d), F32),
        scratch_shapes=[pltpu.VMEM((FFN_ROWS, d_ff), BF16)],
        compiler_params=_params("parallel"),
        name="rms_swiglu",
    )(x2d, gain, w_gate, w_up, w_down)


def _pool_route_kernel(x_ref, halo_ref, pn_ref, ps_ref, mn_ref, wp_ref, wrt_ref,
                       x3_ref, h3_ref, route_ref, routet_ref, info_ref, running_ref, *, n_experts):
    bi, j = pl.program_id(0), pl.program_id(1)
    rows = x_ref.shape[1]

    @pl.when((bi == 0) & (j == 0))
    def _():
        running_ref[...] = jnp.zeros_like(running_ref)

    x = x_ref[0]
    h = _rms_norm(x, pn_ref[...])
    halo = jnp.where(j > 0, _rms_norm(halo_ref[0], pn_ref[...]), 0.0)
    y = jnp.concatenate([halo, h], axis=0)
    pos = j * rows + lax.broadcasted_iota(jnp.int32, (rows, 1), 0)
    gdim = wp_ref.shape[1]
    mixed = []
    for g, w in enumerate(POOL_WINDOWS):
        cols = slice(g * gdim, (g + 1) * gdim)
        acc = y[:, cols]
        span = 1
        while span < w:
            acc = acc + pltpu.roll(acc, span, axis=0)
            span *= 2
        count = jnp.minimum(pos + 1, w).astype(F32)
        p = acc[POOL_HALO:, :] / count - h[:, cols]
        mixed.append(jnp.dot(p.astype(BF16), wp_ref[g], preferred_element_type=F32))
    x3 = x + jnp.concatenate(mixed, axis=-1) * ps_ref[...]
    x3_ref[0] = x3

    h_hi, h_mid, h_lo = _split_bf16(_rms_norm(x3, mn_ref[...]), 3)
    h3_ref[0] = h_hi
    w_hi, w_mid, w_lo = _split_bf16(wrt_ref[...], 3)
    es = ROUTE_SUBLANES
    a = lax.dot_general(jnp.concatenate([w_hi, w_mid, w_lo], axis=0), h_hi, CONTRACT_LAST,
                        preferred_element_type=F32)
    b = lax.dot_general(jnp.concatenate([w_hi, w_mid], axis=0), h_mid, CONTRACT_LAST,
                        preferred_element_type=F32)
    c = lax.dot_general(w_hi, h_lo, CONTRACT_LAST, preferred_element_type=F32)
    logits = ((c + b[es:] + a[2 * es:]) + (b[:es] + a[es:2 * es])) + a[:es]

    eidx = lax.broadcasted_iota(jnp.int32, (es, rows), 0).astype(F32)
    logits = jnp.where(eidx < n_experts, logits, -jnp.inf)
    v1 = jnp.max(logits, axis=0, keepdims=True)
    e1 = jnp.min(jnp.where(logits == v1, eidx, float(es)), axis=0, keepdims=True)
    rest = jnp.where(eidx == e1, -jnp.inf, logits)
    v2 = jnp.max(rest, axis=0, keepdims=True)
    e2 = jnp.min(jnp.where(rest == v2, eidx, float(es)), axis=0, keepdims=True)
    ex = jnp.exp(v2 - v1)
    g1 = 1.0 / (1.0 + ex)
    g2 = ex / (1.0 + ex)

    chosen = ((eidx == e1) | (eidx == e2)).astype(F32)
    t_i = lax.broadcasted_iota(jnp.int32, (rows, rows), 0)
    t_j = lax.broadcasted_iota(jnp.int32, (rows, rows), 1)
    earlier = (t_i < t_j).astype(BF16)
    local_rank = jnp.dot(chosen.astype(BF16), earlier, preferred_element_type=F32)
    count = jnp.sum(chosen, axis=1, keepdims=True)
    padded = jnp.floor((count + (RUN_ALIGN - 1)) * (1.0 / RUN_ALIGN)) * RUN_ALIGN
    ecol = lax.broadcasted_iota(jnp.int32, (es, 1), 0)
    lstart = jnp.zeros((es, 1), F32)
    for e in range(n_experts):
        lstart = lstart + jnp.where(ecol > e, padded[e:e + 1, :], 0.0)
    lpos = lstart + local_rank
    l1 = jnp.sum(jnp.where(eidx == e1, lpos, 0.0), axis=0, keepdims=True)
    l2 = jnp.sum(jnp.where(eidx == e2, lpos, 0.0), axis=0, keepdims=True)

    base = running_ref[...]
    running_ref[...] = base + padded
    lane = lax.broadcasted_iota(jnp.int32, (es, V7X_LANES), 1)
    info_ref[0] = jnp.where(lane == INFO_BASE, base,
                            jnp.where(lane == INFO_COUNT, padded,
                                      jnp.where(lane == INFO_LSTART, lstart, 0.0)))

    rec = jnp.zeros((es, rows), F32)
    for idx, val in ((ROUTE_L1, l1), (ROUTE_L2, l2), (ROUTE_G1, g1), (ROUTE_G2, g2),
                     (ROUTE_E1, e1), (ROUTE_E2, e2)):
        rec = jnp.where(eidx == idx, val, rec)
    routet_ref[0] = rec
    rec_full = jnp.concatenate([rec, jnp.zeros((V7X_LANES - es, rows), F32)], axis=0)
    route_ref[0] = rec_full.T


def _pool_route(x, pool_norm, pool_scale, moe_norm, w_pool, w_router_t, n_experts):
    b, s, d = x.shape
    tiles_per_seq = s // ROUTE_ROWS
    n_tiles = b * tiles_per_seq
    halo_blocks = ROUTE_ROWS // POOL_HALO
    es = ROUTE_SUBLANES
    vec = pl.BlockSpec((1, d), lambda bi, j: (0, 0))
    tile = pl.BlockSpec((1, ROUTE_ROWS, d), lambda bi, j: (bi, j, 0))
    kernel = functools.partial(_pool_route_kernel, n_experts=n_experts)
    return pl.pallas_call(
        kernel,
        grid=(b, tiles_per_seq),
        in_specs=[tile,
                  pl.BlockSpec((1, POOL_HALO, d),
                               lambda bi, j: (bi, jnp.maximum(j * halo_blocks - 1, 0), 0)),
                  vec, vec, vec,
                  pl.BlockSpec(w_pool.shape, lambda bi, j: (0, 0, 0)),
                  pl.BlockSpec(w_router_t.shape, lambda bi, j: (0, 0))],
        out_specs=[tile, tile,
                   pl.BlockSpec((1, ROUTE_ROWS, V7X_LANES), lambda bi, j: (bi, j, 0)),
                   pl.BlockSpec((1, es, ROUTE_ROWS), lambda bi, j: (bi * tiles_per_seq + j, 0, 0)),
                   pl.BlockSpec((1, es, V7X_LANES), lambda bi, j: (bi * tiles_per_seq + j, 0, 0))],
        out_shape=[jax.ShapeDtypeStruct((b, s, d), F32),
                   jax.ShapeDtypeStruct((b, s, d), BF16),
                   jax.ShapeDtypeStruct((b, s, V7X_LANES), F32),
                   jax.ShapeDtypeStruct((n_tiles, es, ROUTE_ROWS), F32),
                   jax.ShapeDtypeStruct((n_tiles, es, V7X_LANES), F32)],
        scratch_shapes=[pltpu.VMEM((es, V7X_LANES), F32)],
        compiler_params=_params("arbitrary", "arbitrary"),
        name="pool_route",
    )(x, x, pool_norm, pool_scale, moe_norm, w_pool, w_router_t)


def _chunks(count, lo_bit, hi_bit):
    for bit in range(lo_bit, hi_bit):
        size = 1 << bit
        offset = (count >> (bit + 1)) << (bit + 1)
        yield offset, size, (count & size) != 0


def _for_each_run_piece(tab_ref, n_experts, max_rows, fn):
    for e in range(n_experts):
        local = tab_ref[0, 0, e]
        sorted_start = tab_ref[0, 0, n_experts + e]
        count = tab_ref[0, 0, 2 * n_experts + e]
        for offset, size, present in _chunks(count, RUN_ALIGN_BITS, max_rows.bit_length()):
            @pl.when(present)
            def _(offset=offset, size=size, local=local, sorted_start=sorted_start):
                fn(pl.multiple_of(local + offset, RUN_ALIGN),
                   pl.multiple_of(sorted_start + offset, RUN_ALIGN), size)


def _slot_rows(n_experts):
    pad = n_experts * (RUN_ALIGN - 1)
    return TOP_K * ROUTE_ROWS + (-(-pad // ROUTE_SUBLANES)) * ROUTE_SUBLANES


ZERO_ROWS = 512


def _dispatch_kernel(tab_ref, prev_tab_ref, h_ref, rt_ref, xs_ref, buf_ref, zero_ref, sem, tail_sem,
                     *, n_experts, max_tail):
    i, n = pl.program_id(0), pl.num_programs(0)
    rows, d = h_ref.shape
    slots = buf_ref.shape[1]
    cur = i % 2

    rt = rt_ref[0]
    l1 = rt[ROUTE_L1:ROUTE_L1 + 1].astype(jnp.int32)
    l2 = rt[ROUTE_L2:ROUTE_L2 + 1].astype(jnp.int32)
    g1 = rt[ROUTE_G1:ROUTE_G1 + 1]
    g2 = rt[ROUTE_G2:ROUTE_G2 + 1]
    pos = lax.broadcasted_iota(jnp.int32, (slots, rows), 0)
    first = pos == l1
    second = pos == l2
    perm = (first | second).astype(BF16)
    x_sorted = jnp.dot(perm, h_ref[...], preferred_element_type=F32)
    gate = jnp.sum(jnp.where(first, g1, 0.0) + jnp.where(second, g2, 0.0), axis=1, keepdims=True)
    gate_bits = pltpu.bitcast(jnp.broadcast_to(gate, (slots, V7X_LANES)), U32)
    buf_ref[cur] = jnp.concatenate([_pack_bf16_pairs(x_sorted), gate_bits], axis=1)

    def run_copy(which):
        def make(local, sorted_start, size):
            return pltpu.make_async_copy(buf_ref.at[which, pl.ds(local, size)],
                                         xs_ref.at[pl.ds(sorted_start, size)], sem.at[which])
        return make

    _for_each_run_piece(tab_ref, n_experts, rows,
                        lambda a, b, size: run_copy(cur)(a, b, size).start())

    @pl.when(i > 0)
    def _():
        _for_each_run_piece(prev_tab_ref, n_experts, rows,
                            lambda a, b, size: run_copy(1 - cur)(a, b, size).wait())

    @pl.when(i == n - 1)
    def _():
        _for_each_run_piece(tab_ref, n_experts, rows,
                            lambda a, b, size: run_copy(cur)(a, b, size).wait())
        zero_ref[...] = jnp.zeros_like(zero_ref)
        used = tab_ref[0, 0, 3 * n_experts]
        tail = xs_ref.shape[0] - used

        def tail_pieces(fn):
            for offset, size, present in _chunks(tail, RUN_ALIGN_BITS, max_tail.bit_length()):
                reps = max(size // ZERO_ROWS, 1)
                piece = min(size, ZERO_ROWS)

                @pl.when(present)
                def _(offset=offset, reps=reps, piece=piece):
                    for r in range(reps):
                        fn(pltpu.make_async_copy(
                            zero_ref.at[pl.ds(0, piece)],
                            xs_ref.at[pl.ds(pl.multiple_of(used + offset + r * piece, RUN_ALIGN),
                                            piece)], tail_sem))

        tail_pieces(lambda c: c.start())
        tail_pieces(lambda c: c.wait())


def _dispatch(h2d, route_t, table, n_rows, n_experts):
    t, d = h2d.shape
    n_tiles = t // ROUTE_ROWS
    slots = _slot_rows(n_experts)
    width = d // 2 + V7X_LANES
    tab_block = (1, 1, TABLE_WIDTH)
    kernel = functools.partial(_dispatch_kernel, n_experts=n_experts, max_tail=n_rows - TOP_K * t)
    return pl.pallas_call(
        kernel,
        grid=(n_tiles,),
        in_specs=[pl.BlockSpec(tab_block, lambda i: (i, 0, 0), memory_space=pltpu.SMEM),
                  pl.BlockSpec(tab_block, lambda i: (jnp.maximum(i - 1, 0), 0, 0),
                               memory_space=pltpu.SMEM),
                  pl.BlockSpec((ROUTE_ROWS, d), lambda i: (i, 0)),
                  pl.BlockSpec((1, ROUTE_SUBLANES, ROUTE_ROWS), lambda i: (i, 0, 0))],
        out_specs=pl.BlockSpec(memory_space=pl.ANY),
        out_shape=jax.ShapeDtypeStruct((n_rows, width), U32),
        scratch_shapes=[pltpu.VMEM((2, slots, width), U32),
                        pltpu.VMEM((ZERO_ROWS, width), U32),
                        pltpu.SemaphoreType.DMA((2,)),
                        pltpu.SemaphoreType.DMA(())],
        compiler_params=_params("arbitrary"),
        name="moe_dispatch",
    )(table, table, h2d, route_t)


def _moe_kernel(tile_ref, expert_ref, lo_ref, hi_ref, x_ref, wg_ref, wu_ref, wd_ref, o_ref,
                xb_ref, act_ref, acc_ref):
    v, f = pl.program_id(0), pl.program_id(1)
    n_visits, n_f = pl.num_programs(0), pl.num_programs(1)
    rows = xb_ref.shape[0]
    half = xb_ref.shape[1] // 2
    lo, hi = lo_ref[v], hi_ref[v]
    tile = tile_ref[v]
    first_visit = (v == 0) | (tile != tile_ref[jnp.maximum(v - 1, 0)])
    last_visit = (v == n_visits - 1) | (tile != tile_ref[jnp.minimum(v + 1, n_visits - 1)])

    @pl.when(first_visit & (f == 0))
    def _():
        acc_ref[...] = jnp.zeros_like(acc_ref)

    @pl.when(hi > lo)
    def _():
        @pl.when(f == 0)
        def _():
            xb_ref[...] = _unpack_bf16_pairs(x_ref[:, :half]).astype(BF16)

        xb = xb_ref[...]
        ff = wg_ref.shape[2]
        for c in range(ff // V7X_MXU_DIM):
            cols = slice(c * V7X_MXU_DIM, (c + 1) * V7X_MXU_DIM)
            gate = jnp.dot(xb, wg_ref[0, :, cols], preferred_element_type=F32)
            up = jnp.dot(xb, wu_ref[0, :, cols], preferred_element_type=F32)
            act_ref[:, cols] = (_silu(gate) * up).astype(BF16)
        row = tile * rows + lax.broadcasted_iota(jnp.int32, (rows, 1), 0)
        mine = (row >= lo) & (row < hi)
        y = jnp.dot(act_ref[...], wd_ref[0], preferred_element_type=F32)
        acc_ref[...] += jnp.where(mine, y, 0.0)

    @pl.when(last_visit & (f == n_f - 1))
    def _():
        gate = pltpu.bitcast(x_ref[:, half:], F32)[:, 0:1]
        weighted = (acc_ref[...] * gate).astype(BF16).astype(F32)
        o_ref[...] = _pack_bf16_pairs(weighted)


def _moe_experts(xs, visits, w_gate, w_up, w_down):
    n_rows, width = xs.shape
    n_experts, d, d_ff = w_gate.shape
    ff = d_ff // MOE_FF_SPLIT
    n_visits = visits[0].shape[0]
    last_f = MOE_FF_SPLIT - 1

    def f_eff(v, f, lo, hi):
        return jnp.where(hi[v] > lo[v], f, last_f)

    grid_spec = pltpu.PrefetchScalarGridSpec(
        num_scalar_prefetch=4,
        grid=(n_visits, MOE_FF_SPLIT),
        in_specs=[
            pl.BlockSpec((MOE_ROWS, width), lambda v, f, tile, ex, lo, hi: (tile[v], 0)),
            pl.BlockSpec((1, d, ff), lambda v, f, tile, ex, lo, hi: (ex[v], 0, f_eff(v, f, lo, hi))),
            pl.BlockSpec((1, d, ff), lambda v, f, tile, ex, lo, hi: (ex[v], 0, f_eff(v, f, lo, hi))),
            pl.BlockSpec((1, ff, d), lambda v, f, tile, ex, lo, hi: (ex[v], f_eff(v, f, lo, hi), 0)),
        ],
        out_specs=pl.BlockSpec((MOE_ROWS, d // 2), lambda v, f, tile, ex, lo, hi: (tile[v], 0)),
        scratch_shapes=[pltpu.VMEM((MOE_ROWS, d), BF16), pltpu.VMEM((MOE_ROWS, ff), BF16),
                        pltpu.VMEM((MOE_ROWS, d), F32)],
    )
    return pl.pallas_call(
        _moe_kernel,
        grid_spec=grid_spec,
        out_shape=jax.ShapeDtypeStruct((n_rows, d // 2), U32),
        compiler_params=_params("arbitrary", "arbitrary"),
        name="moe_experts",
    )(*visits, xs, w_gate, w_up, w_down)


def _plan_visits(group_rows, n_rows):
    n_experts = group_rows.shape[0]
    n_tiles = n_rows // MOE_ROWS
    ends = jnp.cumsum(group_rows)
    starts = ends - group_rows
    used = ends[-1]
    tile_starts = jnp.arange(n_tiles, dtype=jnp.int32) * MOE_ROWS
    needs_cut = (group_rows > 0) & (starts % MOE_ROWS != 0)
    cuts = jnp.where(needs_cut, starts, n_rows)
    lo = jnp.sort(jnp.concatenate([tile_starts, cuts]))
    hi = jnp.minimum(jnp.concatenate([lo[1:], jnp.full((1,), n_rows, jnp.int32)]), used)
    tile = jnp.minimum(lo, n_rows - 1) // MOE_ROWS
    probe = jnp.minimum(lo, used - 1)
    expert = jnp.sum(ends[None, :] <= probe[:, None], axis=1).astype(jnp.int32)
    expert = jnp.minimum(expert, n_experts - 1)
    return tile.astype(jnp.int32), expert, lo.astype(jnp.int32), hi.astype(jnp.int32)


def _combine_kernel(tab_ref, next_tab_ref, x_ref, route_ref, g_ref, ys_ref, o_ref, buf_ref, sem,
                    *, n_experts):
    i, n = pl.program_id(0), pl.num_programs(0)
    rows = x_ref.shape[0]
    slots = buf_ref.shape[1]
    cur = i % 2

    def run_copy(which):
        def make(local, sorted_start, size):
            return pltpu.make_async_copy(ys_ref.at[pl.ds(sorted_start, size)],
                                         buf_ref.at[which, pl.ds(local, size)], sem.at[which])
        return make

    @pl.when(i == 0)
    def _():
        buf_ref[...] = jnp.zeros_like(buf_ref)
        _for_each_run_piece(tab_ref, n_experts, rows,
                            lambda a, b, size: run_copy(0)(a, b, size).start())

    @pl.when(i + 1 < n)
    def _():
        _for_each_run_piece(next_tab_ref, n_experts, rows,
                            lambda a, b, size: run_copy(1 - cur)(a, b, size).start())

    _for_each_run_piece(tab_ref, n_experts, rows,
                        lambda a, b, size: run_copy(cur)(a, b, size).wait())

    route = route_ref[...]
    l1 = route[:, ROUTE_L1:ROUTE_L1 + 1].astype(jnp.int32)
    l2 = route[:, ROUTE_L2:ROUTE_L2 + 1].astype(jnp.int32)
    pos = lax.broadcasted_iota(jnp.int32, (rows, slots), 1)
    pick = ((pos == l1) | (pos == l2)).astype(BF16)
    y_sorted = _unpack_bf16_pairs(buf_ref[cur]).astype(BF16)
    y = x_ref[...] + jnp.dot(pick, y_sorted, preferred_element_type=F32)
    o_ref[...] = _rms_norm(y, g_ref[...])


def _combine(x2d, route2d, table, ys, final_norm, n_experts):
    t, d = x2d.shape
    n_tiles = t // ROUTE_ROWS
    slots = _slot_rows(n_experts)
    tab_block = (1, 1, TABLE_WIDTH)
    kernel = functools.partial(_combine_kernel, n_experts=n_experts)
    return pl.pallas_call(
        kernel,
        grid=(n_tiles,),
        in_specs=[pl.BlockSpec(tab_block, lambda i: (i, 0, 0), memory_space=pltpu.SMEM),
                  pl.BlockSpec(tab_block, lambda i: (jnp.minimum(i + 1, n_tiles - 1), 0, 0),
                               memory_space=pltpu.SMEM),
                  pl.BlockSpec((ROUTE_ROWS, d), lambda i: (i, 0)),
                  pl.BlockSpec((ROUTE_ROWS, V7X_LANES), lambda i: (i, 0)),
                  pl.BlockSpec((1, d), lambda i: (0, 0)),
                  pl.BlockSpec(memory_space=pl.ANY)],
        out_specs=pl.BlockSpec((ROUTE_ROWS, d), lambda i: (i, 0)),
        out_shape=jax.ShapeDtypeStruct((t, d), F32),
        scratch_shapes=[pltpu.VMEM((2, slots, d // 2), U32),
                        pltpu.SemaphoreType.DMA((2,))],
        compiler_params=_params("arbitrary"),
        name="moe_combine",
    )(table, table, x2d, route2d, final_norm, ys)


def kernel(x, attn_norm, attn_w_qkv, attn_sinks, attn_w_o, ffn_norm, ffn_w_gate, ffn_w_up,
           ffn_w_down, pool_norm, pool_w, pool_scale, moe_norm, moe_w_router, moe_w_gate, moe_w_up,
           moe_w_down, final_norm):
    b, s, d = x.shape
    t = b * s
    n_experts = moe_w_router.shape[-1]
    assert s % ATTN_ROWS == 0 and s % ROUTE_ROWS == 0
    assert n_experts <= V7X_SUBLANES and 3 * n_experts + 1 <= TABLE_WIDTH
    assert attn_norm.shape[0] == 1 and pool_norm.shape[0] == 1

    qkv = _qkv_proj(x.reshape(t, d), attn_norm[0][None], attn_w_qkv[0].astype(BF16))
    x1 = _attention(x, qkv.reshape(b, s, -1), attn_sinks[0].astype(F32), attn_w_o[0].astype(BF16))
    x2 = _dense_ffn(x1.reshape(t, d), ffn_norm[0][None], ffn_w_gate[0].astype(BF16),
                    ffn_w_up[0].astype(BF16), ffn_w_down[0].astype(BF16))

    w_router_t = jnp.pad(moe_w_router[0].T, ((0, ROUTE_SUBLANES - n_experts), (0, 0)))
    x3, h3, route, route_t, info = _pool_route(
        x2.reshape(b, s, d), pool_norm[0][None], pool_scale[0][None], moe_norm[0][None],
        pool_w[0].astype(BF16), w_router_t, n_experts)

    n_tiles = t // ROUTE_ROWS
    info = info[:, :n_experts, :].astype(jnp.int32)
    base, padded, lstart = info[..., INFO_BASE], info[..., INFO_COUNT], info[..., INFO_LSTART]
    group_rows = jnp.sum(padded, axis=0)
    group_start = jnp.cumsum(group_rows) - group_rows
    used = jnp.sum(group_rows)
    max_rows = t * TOP_K + n_tiles * n_experts * (RUN_ALIGN - 1)
    n_rows = -(-max_rows // MOE_ROWS) * MOE_ROWS
    table = jnp.concatenate(
        [lstart, group_start[None, :] + base, padded,
         jnp.broadcast_to(used, (n_tiles, 1)),
         jnp.zeros((n_tiles, TABLE_WIDTH - 3 * n_experts - 1), jnp.int32)], axis=1)
    table = table.reshape(n_tiles, 1, TABLE_WIDTH)

    xs = _dispatch(h3.reshape(t, d), route_t, table, n_rows, n_experts)
    visits = _plan_visits(group_rows, n_rows)
    ys = _moe_experts(xs, visits, moe_w_gate[0].astype(BF16), moe_w_up[0].astype(BF16),
                      moe_w_down[0].astype(BF16))
    out = _combine(x3.reshape(t, d), route.reshape(t, V7X_LANES), table, ys, final_norm[None],
                   n_experts)
    return out.reshape(b, s, d)
```

```python
import functools

import jax
import jax.numpy as jnp
from jax import lax
from jax.experimental import pallas as pl
from jax.experimental.pallas import tpu as pltpu

F32 = jnp.float32
BF16 = jnp.bfloat16

HEAD_DIM = 64
N_KV_HEADS = 4
WINDOW = 128
POOL_WINDOWS = (2, 4, 8, 16)
TOP_K = 2
RMS_EPS = 1e-5
NEG_INF = -1e30

V7X_LANES = 128
V7X_SUBLANES = 8
V7X_BF16_TILE_ROWS = 16
V7X_MXU_DIM = 256
V7X_VMEM_LIMIT_BYTES = 56 * 1024 * 1024

QKV_ROWS = 1024
ATTN_ROWS = 512
FFN_ROWS = 1024
ROUTE_ROWS = 512
MOE_ROWS = 512
POOL_HALO = 16

RUN_ALIGN = V7X_BF16_TILE_ROWS
RUN_ALIGN_BITS = RUN_ALIGN.bit_length() - 1
GATE_PIECES = 3
ROUTE_SUBLANES = 16
ROUTE_L1, ROUTE_L2, ROUTE_G1, ROUTE_G2, ROUTE_E1, ROUTE_E2 = range(6)
INFO_BASE, INFO_COUNT, INFO_LSTART = range(3)
TABLE_FIELDS = 5
TABLE_WIDTH = 64


def _params(*semantics):
    return pltpu.CompilerParams(dimension_semantics=semantics,
                                vmem_limit_bytes=V7X_VMEM_LIMIT_BYTES)


def _rms_norm(x, gain):
    ms = jnp.mean(x * x, axis=-1, keepdims=True)
    return x * lax.rsqrt(ms + RMS_EPS) * gain


def _silu(x):
    return x / (1.0 + jnp.exp(-x))


def _split_bf16(x, pieces):
    out = []
    for _ in range(pieces - 1):
        top = x.astype(BF16)
        out.append(top)
        x = x - top.astype(F32)
    out.append(x.astype(BF16))
    return out


CONTRACT_LAST = (((1,), (1,)), ((), ()))
CONTRACT_FIRST = (((0,), (0,)), ((), ()))


def _qkv_kernel(x_ref, g_ref, w_ref, o_ref):
    h = _rms_norm(x_ref[...], g_ref[...]).astype(BF16)
    o_ref[...] = jnp.dot(h, w_ref[...], preferred_element_type=F32).astype(BF16)


def _qkv_proj(x2d, gain, w_qkv):
    t, d = x2d.shape
    n = w_qkv.shape[1]
    return pl.pallas_call(
        _qkv_kernel,
        grid=(t // QKV_ROWS,),
        in_specs=[pl.BlockSpec((QKV_ROWS, d), lambda i: (i, 0)),
                  pl.BlockSpec((1, d), lambda i: (0, 0)),
                  pl.BlockSpec((d, n), lambda i: (0, 0))],
        out_specs=pl.BlockSpec((QKV_ROWS, n), lambda i: (i, 0)),
        out_shape=jax.ShapeDtypeStruct((t, n), BF16),
        compiler_params=_params("parallel"),
        name="rms_qkv",
    )(x2d, gain, w_qkv)


def _attn_kernel(bias_ref, sink_ref, q_ref, kc_ref, kp_ref, vc_ref, vp_ref, x_ref, wo_ref, o_ref,
                 ot_ref, *, n_heads):
    group = n_heads // N_KV_HEADS
    gw = group * WINDOW
    j = pl.program_id(1)
    key = lax.broadcasted_iota(jnp.int32, (WINDOW, n_heads * WINDOW), 0)
    query = lax.broadcasted_iota(jnp.int32, (WINDOW, n_heads * WINDOW), 1) & (WINDOW - 1)
    from_prev = key > query
    from_prev_bf = from_prev.astype(BF16)
    no_prev = from_prev & (j == 0)
    sink = sink_ref[...]

    n_blocks = q_ref.shape[1] // WINDOW
    for blk in range(n_blocks):
        rows = slice(blk * WINDOW, (blk + 1) * WINDOW)
        q_blk = q_ref[0, rows, :] * jnp.asarray(HEAD_DIM ** -0.5, BF16)
        if blk == 0:
            k_prev, v_prev = kp_ref[0], vp_ref[0]
        else:
            prev = slice((blk - 1) * WINDOW, blk * WINDOW)
            k_prev, v_prev = kc_ref[0, prev, :], vc_ref[0, prev, :]
        k_cat = jnp.concatenate([k_prev, kc_ref[0, rows, :]], axis=0)
        v_cat = jnp.concatenate([v_prev, vc_ref[0, rows, :]], axis=0)
        scores = []
        for kvh in range(N_KV_HEADS):
            q_g = jnp.concatenate([q_blk[:, h * HEAD_DIM:(h + 1) * HEAD_DIM]
                                   for h in range(kvh * group, (kvh + 1) * group)], axis=0)
            k_h = k_cat[:, kvh * HEAD_DIM:(kvh + 1) * HEAD_DIM]
            scores.append(lax.dot_general(k_h, q_g, CONTRACT_LAST, preferred_element_type=F32))
        s_all = jnp.concatenate(scores, axis=1)
        s = jnp.where(from_prev, s_all[:WINDOW], s_all[WINDOW:]) + bias_ref[...]
        if blk == 0:
            s = jnp.where(no_prev, NEG_INF, s)
        m = jnp.maximum(jnp.max(s, axis=0, keepdims=True), sink)
        e = jnp.exp(s - m)
        inv_denom = 1.0 / (jnp.sum(e, axis=0, keepdims=True) + jnp.exp(sink - m))
        e_bf = e.astype(BF16)
        p_prev = e_bf * from_prev_bf
        p_cat = jnp.concatenate([p_prev, e_bf - p_prev], axis=0)
        for kvh in range(N_KV_HEADS):
            lanes = slice(kvh * gw, (kvh + 1) * gw)
            v_h = v_cat[:, kvh * HEAD_DIM:(kvh + 1) * HEAD_DIM]
            o_t = lax.dot_general(v_h, p_cat[:, lanes], CONTRACT_FIRST,
                                  preferred_element_type=F32)
            o_t = (o_t * inv_denom[:, lanes]).astype(BF16)
            for g in range(group):
                h = kvh * group + g
                ot_ref[h * HEAD_DIM:(h + 1) * HEAD_DIM, rows] = o_t[:, g * WINDOW:(g + 1) * WINDOW]
    o_ref[0] = x_ref[0] + lax.dot_general(ot_ref[...], wo_ref[...], CONTRACT_FIRST,
                                          preferred_element_type=F32)


def _alibi_band_bias(n_heads):
    slopes = jnp.exp2(-8.0 * jnp.arange(1, n_heads + 1, dtype=F32) / n_heads)
    c = jnp.arange(WINDOW)[:, None, None]
    r = jnp.arange(WINDOW)[None, None, :]
    dist = ((r - c) % WINDOW).astype(F32)
    return (-slopes[None, :, None] * dist).reshape(WINDOW, n_heads * WINDOW)


def _attention(x, qkv, sinks, w_o):
    b, s, d = x.shape
    n_heads = w_o.shape[0] // HEAD_DIM
    q_width = n_heads * HEAD_DIM
    kv_width = N_KV_HEADS * HEAD_DIM
    k_col = q_width // kv_width
    v_col = k_col + 1
    blocks_per_step = ATTN_ROWS // WINDOW
    sink_row = jnp.repeat(sinks, WINDOW)[None, :]

    def prev_block(bi, j):
        return jnp.maximum(j * blocks_per_step - 1, 0)

    kernel = functools.partial(_attn_kernel, n_heads=n_heads)
    return pl.pallas_call(
        kernel,
        grid=(b, s // ATTN_ROWS),
        in_specs=[
            pl.BlockSpec((WINDOW, n_heads * WINDOW), lambda bi, j: (0, 0)),
            pl.BlockSpec((1, n_heads * WINDOW), lambda bi, j: (0, 0)),
            pl.BlockSpec((1, ATTN_ROWS, q_width), lambda bi, j: (bi, j, 0)),
            pl.BlockSpec((1, ATTN_ROWS, kv_width), lambda bi, j: (bi, j, k_col)),
            pl.BlockSpec((1, WINDOW, kv_width), lambda bi, j: (bi, prev_block(bi, j), k_col)),
            pl.BlockSpec((1, ATTN_ROWS, kv_width), lambda bi, j: (bi, j, v_col)),
            pl.BlockSpec((1, WINDOW, kv_width), lambda bi, j: (bi, prev_block(bi, j), v_col)),
            pl.BlockSpec((1, ATTN_ROWS, d), lambda bi, j: (bi, j, 0)),
            pl.BlockSpec((q_width, d), lambda bi, j: (0, 0)),
        ],
        out_specs=pl.BlockSpec((1, ATTN_ROWS, d), lambda bi, j: (bi, j, 0)),
        out_shape=jax.ShapeDtypeStruct((b, s, d), F32),
        scratch_shapes=[pltpu.VMEM((q_width, ATTN_ROWS), BF16)],
        compiler_params=_params("parallel", "parallel"),
        name="swa_attention",
    )(_alibi_band_bias(n_heads), sink_row, qkv, qkv, qkv, qkv, qkv, x, w_o)


def _ffn_kernel(x_ref, g_ref, wg_ref, wu_ref, wd_ref, o_ref, act_ref):
    x = x_ref[...]
    h = _rms_norm(x, g_ref[...]).astype(BF16)
    d_ff = wg_ref.shape[1]
    for c in range(d_ff // V7X_MXU_DIM):
        cols = slice(c * V7X_MXU_DIM, (c + 1) * V7X_MXU_DIM)
        gate = jnp.dot(h, wg_ref[:, cols], preferred_element_type=F32)
        up = jnp.dot(h, wu_ref[:, cols], preferred_element_type=F32)
        act_ref[:, cols] = (_silu(gate) * up).astype(BF16)
    o_ref[...] = x + jnp.dot(act_ref[...], wd_ref[...], preferred_element_type=F32)


def _dense_ffn(x2d, gain, w_gate, w_up, w_down):
    t, d = x2d.shape
    d_ff = w_gate.shape[1]
    resident = dict(pipeline_mode=pl.Buffered(1))
    return pl.pallas_call(
        _ffn_kernel,
        grid=(t // FFN_ROWS,),
        in_specs=[pl.BlockSpec((FFN_ROWS, d), lambda i: (i, 0)),
                  pl.BlockSpec((1, d), lambda i: (0, 0)),
                  pl.BlockSpec((d, d_ff), lambda i: (0, 0), **resident),
                  pl.BlockSpec((d, d_ff), lambda i: (0, 0), **resident),
                  pl.BlockSpec((d_ff, d), lambda i: (0, 0), **resident)],
        out_specs=pl.BlockSpec((FFN_ROWS, d), lambda i: (i, 0)),
        out_shape=jax.ShapeDtypeStruct((t, d), F32),
        scratch_shapes=[pltpu.VMEM((FFN_ROWS, d_ff), BF16)],
        compiler_params=_params("parallel"),
        name="rms_swiglu",
    )(x2d, gain, w_gate, w_up, w_down)


def _pool_route_kernel(x_ref, halo_ref, pn_ref, ps_ref, mn_ref, wp_ref, wrt_ref,
                       x3_ref, h3_ref, route_ref, routet_ref, info_ref, running_ref, *, n_experts):
    bi, j = pl.program_id(0), pl.program_id(1)
    rows = x_ref.shape[1]

    @pl.when((bi == 0) & (j == 0))
    def _():
        running_ref[...] = jnp.zeros_like(running_ref)

    x = x_ref[0]
    h = _rms_norm(x, pn_ref[...])
    halo = jnp.where(j > 0, _rms_norm(halo_ref[0], pn_ref[...]), 0.0)
    y = jnp.concatenate([halo, h], axis=0)
    pos = j * rows + lax.broadcasted_iota(jnp.int32, (rows, 1), 0)
    gdim = wp_ref.shape[1]
    mixed = []
    for g, w in enumerate(POOL_WINDOWS):
        cols = slice(g * gdim, (g + 1) * gdim)
        acc = y[:, cols]
        span = 1
        while span < w:
            acc = acc + pltpu.roll(acc, span, axis=0)
            span *= 2
        count = jnp.minimum(pos + 1, w).astype(F32)
        p = acc[POOL_HALO:, :] / count - h[:, cols]
        mixed.append(jnp.dot(p.astype(BF16), wp_ref[g], preferred_element_type=F32))
    x3 = x + jnp.concatenate(mixed, axis=-1) * ps_ref[...]
    x3_ref[0] = x3

    h_hi, h_mid, h_lo = _split_bf16(_rms_norm(x3, mn_ref[...]), 3)
    h3_ref[0] = h_hi
    w_hi, w_mid, w_lo = _split_bf16(wrt_ref[...], 3)
    es = ROUTE_SUBLANES
    a = lax.dot_general(jnp.concatenate([w_hi, w_mid, w_lo], axis=0), h_hi, CONTRACT_LAST,
                        preferred_element_type=F32)
    b = lax.dot_general(jnp.concatenate([w_hi, w_mid], axis=0), h_mid, CONTRACT_LAST,
                        preferred_element_type=F32)
    c = lax.dot_general(w_hi, h_lo, CONTRACT_LAST, preferred_element_type=F32)
    logits = ((c + b[es:] + a[2 * es:]) + (b[:es] + a[es:2 * es])) + a[:es]

    eidx = lax.broadcasted_iota(jnp.int32, (es, rows), 0).astype(F32)
    logits = jnp.where(eidx < n_experts, logits, -jnp.inf)
    v1 = jnp.max(logits, axis=0, keepdims=True)
    e1 = jnp.min(jnp.where(logits == v1, eidx, float(es)), axis=0, keepdims=True)
    rest = jnp.where(eidx == e1, -jnp.inf, logits)
    v2 = jnp.max(rest, axis=0, keepdims=True)
    e2 = jnp.min(jnp.where(rest == v2, eidx, float(es)), axis=0, keepdims=True)
    ex = jnp.exp(v2 - v1)
    g1 = 1.0 / (1.0 + ex)
    g2 = ex / (1.0 + ex)

    chosen = ((eidx == e1) | (eidx == e2)).astype(F32)
    t_i = lax.broadcasted_iota(jnp.int32, (rows, rows), 0)
    t_j = lax.broadcasted_iota(jnp.int32, (rows, rows), 1)
    earlier = (t_i < t_j).astype(BF16)
    local_rank = jnp.dot(chosen.astype(BF16), earlier, preferred_element_type=F32)
    count = jnp.sum(chosen, axis=1, keepdims=True)
    padded = jnp.floor((count + (RUN_ALIGN - 1)) * (1.0 / RUN_ALIGN)) * RUN_ALIGN
    ecol = lax.broadcasted_iota(jnp.int32, (es, 1), 0)
    lstart = jnp.zeros((es, 1), F32)
    for e in range(n_experts):
        lstart = lstart + jnp.where(ecol > e, padded[e:e + 1, :], 0.0)
    lpos = lstart + local_rank
    l1 = jnp.sum(jnp.where(eidx == e1, lpos, 0.0), axis=0, keepdims=True)
    l2 = jnp.sum(jnp.where(eidx == e2, lpos, 0.0), axis=0, keepdims=True)

    base = running_ref[...]
    running_ref[...] = base + padded
    lane = lax.broadcasted_iota(jnp.int32, (es, V7X_LANES), 1)
    info_ref[0] = jnp.where(lane == INFO_BASE, base,
                            jnp.where(lane == INFO_COUNT, padded,
                                      jnp.where(lane == INFO_LSTART, lstart, 0.0)))

    rec = jnp.zeros((es, rows), F32)
    for idx, val in ((ROUTE_L1, l1), (ROUTE_L2, l2), (ROUTE_G1, g1), (ROUTE_G2, g2),
                     (ROUTE_E1, e1), (ROUTE_E2, e2)):
        rec = jnp.where(eidx == idx, val, rec)
    routet_ref[0] = rec
    rec_full = jnp.concatenate([rec, jnp.zeros((V7X_LANES - es, rows), F32)], axis=0)
    route_ref[0] = rec_full.T


def _pool_route(x, pool_norm, pool_scale, moe_norm, w_pool, w_router_t, n_experts):
    b, s, d = x.shape
    tiles_per_seq = s // ROUTE_ROWS
    n_tiles = b * tiles_per_seq
    halo_blocks = ROUTE_ROWS // POOL_HALO
    es = ROUTE_SUBLANES
    vec = pl.BlockSpec((1, d), lambda bi, j: (0, 0))
    tile = pl.BlockSpec((1, ROUTE_ROWS, d), lambda bi, j: (bi, j, 0))
    kernel = functools.partial(_pool_route_kernel, n_experts=n_experts)
    return pl.pallas_call(
        kernel,
        grid=(b, tiles_per_seq),
        in_specs=[tile,
                  pl.BlockSpec((1, POOL_HALO, d),
                               lambda bi, j: (bi, jnp.maximum(j * halo_blocks - 1, 0), 0)),
                  vec, vec, vec,
                  pl.BlockSpec(w_pool.shape, lambda bi, j: (0, 0, 0)),
                  pl.BlockSpec(w_router_t.shape, lambda bi, j: (0, 0))],
        out_specs=[tile, tile,
                   pl.BlockSpec((1, ROUTE_ROWS, V7X_LANES), lambda bi, j: (bi, j, 0)),
                   pl.BlockSpec((1, es, ROUTE_ROWS), lambda bi, j: (bi * tiles_per_seq + j, 0, 0)),
                   pl.BlockSpec((1, es, V7X_LANES), lambda bi, j: (bi * tiles_per_seq + j, 0, 0))],
        out_shape=[jax.ShapeDtypeStruct((b, s, d), F32),
                   jax.ShapeDtypeStruct((b, s, d), BF16),
                   jax.ShapeDtypeStruct((b, s, V7X_LANES), F32),
                   jax.ShapeDtypeStruct((n_tiles, es, ROUTE_ROWS), F32),
                   jax.ShapeDtypeStruct((n_tiles, es, V7X_LANES), F32)],
        scratch_shapes=[pltpu.VMEM((es, V7X_LANES), F32)],
        compiler_params=_params("arbitrary", "arbitrary"),
        name="pool_route",
    )(x, x, pool_norm, pool_scale, moe_norm, w_pool, w_router_t)


def _chunks(count, lo_bit, hi_bit):
    for bit in range(lo_bit, hi_bit):
        size = 1 << bit
        offset = (count >> (bit + 1)) << (bit + 1)
        yield offset, size, (count & size) != 0


def _for_each_run_piece(tab_ref, n_experts, max_rows, fn):
    for e in range(n_experts):
        local = tab_ref[0, 0, e]
        sorted_start = tab_ref[0, 0, n_experts + e]
        count = tab_ref[0, 0, 2 * n_experts + e]
        for offset, size, present in _chunks(count, RUN_ALIGN_BITS, max_rows.bit_length()):
            @pl.when(present)
            def _(offset=offset, size=size, local=local, sorted_start=sorted_start):
                fn(pl.multiple_of(local + offset, RUN_ALIGN),
                   pl.multiple_of(sorted_start + offset, RUN_ALIGN), size)


def _wait_tile_runs(tab_ref, n_experts, max_rows, make_copy):
    total = tab_ref[0, 0, TABLE_FIELDS * n_experts]
    for _, size, present in _chunks(total, RUN_ALIGN_BITS, max_rows.bit_length()):
        @pl.when(present)
        def _(size=size):
            make_copy(0, 0, size).wait()


def _slot_rows(n_experts):
    pad = n_experts * (RUN_ALIGN - 1)
    return TOP_K * ROUTE_ROWS + (-(-pad // ROUTE_SUBLANES)) * ROUTE_SUBLANES


ZERO_ROWS = MOE_ROWS // 2


def _dispatch_kernel(tab_ref, prev_tab_ref, h_ref, rt_ref, xs_ref, buf_ref, zero_ref, sem, fill_sem,
                     *, n_experts, max_tail):
    i, n = pl.program_id(0), pl.num_programs(0)
    rows, d = h_ref.shape
    slots = buf_ref.shape[1]
    cur = i % 2

    rt = rt_ref[0]
    l1 = rt[ROUTE_L1:ROUTE_L1 + 1].astype(jnp.int32)
    l2 = rt[ROUTE_L2:ROUTE_L2 + 1].astype(jnp.int32)
    g1 = rt[ROUTE_G1:ROUTE_G1 + 1]
    g2 = rt[ROUTE_G2:ROUTE_G2 + 1]
    pos = lax.broadcasted_iota(jnp.int32, (slots, rows), 0)
    first = pos == l1
    second = pos == l2
    perm = (first | second).astype(BF16)
    x_sorted = jnp.dot(perm, h_ref[...], preferred_element_type=F32)
    gate = jnp.sum(jnp.where(first, g1, 0.0) + jnp.where(second, g2, 0.0), axis=1, keepdims=True)
    lane = lax.broadcasted_iota(jnp.int32, (slots, V7X_LANES), 1)
    gate_lanes = jnp.zeros((slots, V7X_LANES), F32)
    for idx, piece in enumerate(_split_bf16(gate, GATE_PIECES)):
        gate_lanes = jnp.where(lane == idx, piece.astype(F32), gate_lanes)
    buf_ref[cur] = jnp.concatenate([x_sorted, gate_lanes], axis=1).astype(BF16)

    def run_copy(which):
        def make(local, sorted_start, size):
            return pltpu.make_async_copy(buf_ref.at[which, pl.ds(local, size)],
                                         xs_ref.at[pl.ds(sorted_start, size)], sem.at[which])
        return make

    _for_each_run_piece(tab_ref, n_experts, rows,
                        lambda a, b, size: run_copy(cur)(a, b, size).start())

    @pl.when(i > 0)
    def _():
        _wait_tile_runs(prev_tab_ref, n_experts, slots, run_copy(1 - cur))

    @pl.when(i == n - 1)
    def _():
        _wait_tile_runs(tab_ref, n_experts, slots, run_copy(cur))
        zero_ref[...] = jnp.zeros_like(zero_ref)

        def fill_pieces(fn):
            for e in range(n_experts):
                start = tab_ref[0, 0, 3 * n_experts + e]
                length = tab_ref[0, 0, 4 * n_experts + e]
                for offset, size, present in _chunks(length, RUN_ALIGN_BITS, ZERO_ROWS.bit_length()):
                    @pl.when(present)
                    def _(offset=offset, size=size, start=start):
                        fn(pltpu.make_async_copy(
                            zero_ref.at[pl.ds(0, size)],
                            xs_ref.at[pl.ds(pl.multiple_of(start + offset, RUN_ALIGN), size)],
                            fill_sem))
            used_rows = tab_ref[0, 0, TABLE_FIELDS * n_experts + 1]
            for k in range(0, max_tail, ZERO_ROWS):
                @pl.when(used_rows + k < xs_ref.shape[0])
                def _(k=k):
                    fn(pltpu.make_async_copy(
                        zero_ref,
                        xs_ref.at[pl.ds(pl.multiple_of(used_rows + k, RUN_ALIGN), ZERO_ROWS)],
                        fill_sem))

        fill_pieces(lambda c: c.start())
        fill_pieces(lambda c: c.wait())


def _dispatch(h2d, route_t, table, n_rows, n_experts):
    t, d = h2d.shape
    n_tiles = t // ROUTE_ROWS
    slots = _slot_rows(n_experts)
    width = d + V7X_LANES
    tab_block = (1, 1, TABLE_WIDTH)
    kernel = functools.partial(_dispatch_kernel, n_experts=n_experts, max_tail=n_rows - TOP_K * t)
    return pl.pallas_call(
        kernel,
        grid=(n_tiles,),
        in_specs=[pl.BlockSpec(tab_block, lambda i: (i, 0, 0), memory_space=pltpu.SMEM),
                  pl.BlockSpec(tab_block, lambda i: (jnp.maximum(i - 1, 0), 0, 0),
                               memory_space=pltpu.SMEM),
                  pl.BlockSpec((ROUTE_ROWS, d), lambda i: (i, 0)),
                  pl.BlockSpec((1, ROUTE_SUBLANES, ROUTE_ROWS), lambda i: (i, 0, 0))],
        out_specs=pl.BlockSpec(memory_space=pl.ANY),
        out_shape=jax.ShapeDtypeStruct((n_rows, width), BF16),
        scratch_shapes=[pltpu.VMEM((2, slots, width), BF16),
                        pltpu.VMEM((ZERO_ROWS, width), BF16),
                        pltpu.SemaphoreType.DMA((2,)),
                        pltpu.SemaphoreType.DMA(())],
        compiler_params=_params("arbitrary"),
        name="moe_dispatch",
    )(table, table, h2d, route_t)


def _moe_kernel(block_ref, expert_ref, used_ref, x_ref, wg_ref, wu_ref, wd_ref, o_ref, act_ref):
    v = pl.program_id(0)
    d = o_ref.shape[1]

    @pl.when(used_ref[v] == 0)
    def _():
        o_ref[...] = jnp.zeros_like(o_ref)

    @pl.when(used_ref[v] != 0)
    def _():
        xb = x_ref[:, :d]
        d_ff = wg_ref.shape[2]
        for c in range(d_ff // V7X_MXU_DIM):
            cols = slice(c * V7X_MXU_DIM, (c + 1) * V7X_MXU_DIM)
            gate = jnp.dot(xb, wg_ref[0, :, cols], preferred_element_type=F32)
            up = jnp.dot(xb, wu_ref[0, :, cols], preferred_element_type=F32)
            act_ref[:, cols] = (_silu(gate) * up).astype(BF16)
        y = jnp.dot(act_ref[...], wd_ref[0], preferred_element_type=F32)
        pieces = x_ref[:, d:].astype(F32)
        weight = pieces[:, 0:1]
        for idx in range(1, GATE_PIECES):
            weight = weight + pieces[:, idx:idx + 1]
        o_ref[...] = (y * weight).astype(BF16)


def _moe_experts(xs, visits, w_gate, w_up, w_down):
    n_rows, width = xs.shape
    n_experts, d, d_ff = w_gate.shape
    n_tiles = n_rows // MOE_ROWS
    grid_spec = pltpu.PrefetchScalarGridSpec(
        num_scalar_prefetch=3,
        grid=(n_tiles,),
        in_specs=[
            pl.BlockSpec((MOE_ROWS, width), lambda v, blk, ex, used: (blk[v], 0)),
            pl.BlockSpec((1, d, d_ff), lambda v, blk, ex, used: (ex[v], 0, 0)),
            pl.BlockSpec((1, d, d_ff), lambda v, blk, ex, used: (ex[v], 0, 0)),
            pl.BlockSpec((1, d_ff, d), lambda v, blk, ex, used: (ex[v], 0, 0)),
        ],
        out_specs=pl.BlockSpec((MOE_ROWS, d), lambda v, blk, ex, used: (v, 0)),
        scratch_shapes=[pltpu.VMEM((MOE_ROWS, d_ff), BF16)],
    )
    return pl.pallas_call(
        _moe_kernel,
        grid_spec=grid_spec,
        out_shape=jax.ShapeDtypeStruct((n_rows, d), BF16),
        compiler_params=_params("arbitrary"),
        name="moe_experts",
    )(*visits, xs, w_gate, w_up, w_down)


def _plan_tiles(group_tiles, n_tiles):
    n_experts = group_tiles.shape[0]
    ends = jnp.cumsum(group_tiles)
    used_tiles = ends[-1]
    tile = jnp.arange(n_tiles, dtype=jnp.int32)
    used = tile < used_tiles
    block = jnp.minimum(tile, used_tiles - 1)
    expert = jnp.sum(ends[None, :] <= block[:, None], axis=1).astype(jnp.int32)
    expert = jnp.minimum(expert, n_experts - 1)
    return block.astype(jnp.int32), expert, used.astype(jnp.int32)


def _combine_kernel(tab_ref, next_tab_ref, x_ref, route_ref, g_ref, ys_ref, o_ref, buf_ref, sem,
                    *, n_experts):
    i, n = pl.program_id(0), pl.num_programs(0)
    rows = x_ref.shape[0]
    slots = buf_ref.shape[1]
    cur = i % 2

    def run_copy(which):
        def make(local, sorted_start, size):
            return pltpu.make_async_copy(ys_ref.at[pl.ds(sorted_start, size)],
                                         buf_ref.at[which, pl.ds(local, size)], sem.at[which])
        return make

    @pl.when(i == 0)
    def _():
        buf_ref[...] = jnp.zeros_like(buf_ref)
        _for_each_run_piece(tab_ref, n_experts, rows,
                            lambda a, b, size: run_copy(0)(a, b, size).start())

    @pl.when(i + 1 < n)
    def _():
        _for_each_run_piece(next_tab_ref, n_experts, rows,
                            lambda a, b, size: run_copy(1 - cur)(a, b, size).start())

    _wait_tile_runs(tab_ref, n_experts, slots, run_copy(cur))

    route = route_ref[...]
    l1 = route[:, ROUTE_L1:ROUTE_L1 + 1].astype(jnp.int32)
    l2 = route[:, ROUTE_L2:ROUTE_L2 + 1].astype(jnp.int32)
    pos = lax.broadcasted_iota(jnp.int32, (rows, slots), 1)
    pick = ((pos == l1) | (pos == l2)).astype(BF16)
    y = x_ref[...] + jnp.dot(pick, buf_ref[cur], preferred_element_type=F32)
    o_ref[...] = _rms_norm(y, g_ref[...])


def _combine(x2d, route2d, table, ys, final_norm, n_experts):
    t, d = x2d.shape
    n_tiles = t // ROUTE_ROWS
    slots = _slot_rows(n_experts)
    tab_block = (1, 1, TABLE_WIDTH)
    kernel = functools.partial(_combine_kernel, n_experts=n_experts)
    return pl.pallas_call(
        kernel,
        grid=(n_tiles,),
        in_specs=[pl.BlockSpec(tab_block, lambda i: (i, 0, 0), memory_space=pltpu.SMEM),
                  pl.BlockSpec(tab_block, lambda i: (jnp.minimum(i + 1, n_tiles - 1), 0, 0),
                               memory_space=pltpu.SMEM),
                  pl.BlockSpec((ROUTE_ROWS, d), lambda i: (i, 0)),
                  pl.BlockSpec((ROUTE_ROWS, V7X_LANES), lambda i: (i, 0)),
                  pl.BlockSpec((1, d), lambda i: (0, 0)),
                  pl.BlockSpec(memory_space=pl.ANY)],
        out_specs=pl.BlockSpec((ROUTE_ROWS, d), lambda i: (i, 0)),
        out_shape=jax.ShapeDtypeStruct((t, d), F32),
        scratch_shapes=[pltpu.VMEM((2, slots, d), BF16),
                        pltpu.SemaphoreType.DMA((2,))],
        compiler_params=_params("arbitrary"),
        name="moe_combine",
    )(table, table, x2d, route2d, final_norm, ys)


def kernel(x, attn_norm, attn_w_qkv, attn_sinks, attn_w_o, ffn_norm, ffn_w_gate, ffn_w_up,
           ffn_w_down, pool_norm, pool_w, pool_scale, moe_norm, moe_w_router, moe_w_gate, moe_w_up,
           moe_w_down, final_norm):
    b, s, d = x.shape
    t = b * s
    n_experts = moe_w_router.shape[-1]
    assert s % ATTN_ROWS == 0 and s % ROUTE_ROWS == 0
    assert n_experts <= V7X_SUBLANES and TABLE_FIELDS * n_experts + 2 <= TABLE_WIDTH
    assert attn_norm.shape[0] == 1 and pool_norm.shape[0] == 1

    qkv = _qkv_proj(x.reshape(t, d), attn_norm[0][None], attn_w_qkv[0].astype(BF16))
    x1 = _attention(x, qkv.reshape(b, s, -1), attn_sinks[0].astype(F32), attn_w_o[0].astype(BF16))
    x2 = _dense_ffn(x1.reshape(t, d), ffn_norm[0][None], ffn_w_gate[0].astype(BF16),
                    ffn_w_up[0].astype(BF16), ffn_w_down[0].astype(BF16))

    w_router_t = jnp.pad(moe_w_router[0].T, ((0, ROUTE_SUBLANES - n_experts), (0, 0)))
    x3, h3, route, route_t, info = _pool_route(
        x2.reshape(b, s, d), pool_norm[0][None], pool_scale[0][None], moe_norm[0][None],
        pool_w[0].astype(BF16), w_router_t, n_experts)

    n_tiles = t // ROUTE_ROWS
    info = info[:, :n_experts, :].astype(jnp.int32)
    base, padded, lstart = info[..., INFO_BASE], info[..., INFO_COUNT], info[..., INFO_LSTART]
    group_rows = jnp.sum(padded, axis=0)
    group_tiles = -(-group_rows // MOE_ROWS)
    group_start = (jnp.cumsum(group_tiles) - group_tiles) * MOE_ROWS
    max_rows = (t * TOP_K + n_tiles * n_experts * (RUN_ALIGN - 1)
                + n_experts * (MOE_ROWS - RUN_ALIGN))
    n_rows = -(-max_rows // MOE_ROWS) * MOE_ROWS
    per_tile = lambda v: jnp.broadcast_to(v[None, :], (n_tiles, n_experts))
    table = jnp.concatenate(
        [lstart, group_start[None, :] + base, padded,
         per_tile(group_start + group_rows), per_tile(group_tiles * MOE_ROWS - group_rows),
         jnp.sum(padded, axis=1, keepdims=True),
         jnp.broadcast_to(jnp.sum(group_tiles) * MOE_ROWS, (n_tiles, 1)),
         jnp.zeros((n_tiles, TABLE_WIDTH - TABLE_FIELDS * n_experts - 2), jnp.int32)], axis=1)
    table = table.reshape(n_tiles, 1, TABLE_WIDTH)

    xs = _dispatch(h3.reshape(t, d), route_t, table, n_rows, n_experts)
    ys = _moe_experts(xs, _plan_tiles(group_tiles, n_rows // MOE_ROWS), moe_w_gate[0].astype(BF16),
                      moe_w_up[0].astype(BF16), moe_w_down[0].astype(BF16))
    out = _combine(x3.reshape(t, d), route.reshape(t, V7X_LANES), table, ys, final_norm[None],
                   n_experts)
    return out.reshape(b, s, d)
```

```python
import functools

import jax
import jax.numpy as jnp
from jax import lax
from jax.experimental import pallas as pl
from jax.experimental.pallas import tpu as pltpu

F32 = jnp.float32
BF16 = jnp.bfloat16

HEAD_DIM = 64
N_KV_HEADS = 4
WINDOW = 128
POOL_WINDOWS = (2, 4, 8, 16)
TOP_K = 2
RMS_EPS = 1e-5
NEG_INF = -1e30

V7X_LANES = 128
V7X_SUBLANES = 8
V7X_BF16_TILE_ROWS = 16
V7X_MXU_DIM = 256
V7X_VMEM_LIMIT_BYTES = 60 * 1024 * 1024

QKV_ROWS = 1024
ATTN_ROWS = 512
FFN_ROWS = 1024
ROUTE_ROWS = 512
MOE_ROWS = 512
POOL_HALO = 16

RUN_ALIGN = V7X_BF16_TILE_ROWS
RUN_ALIGN_BITS = RUN_ALIGN.bit_length() - 1
GATE_PIECES = 3
ROUTE_SUBLANES = 16
ROUTE_L1, ROUTE_L2, ROUTE_G1, ROUTE_G2, ROUTE_E1, ROUTE_E2 = range(6)
INFO_BASE, INFO_COUNT, INFO_LSTART = range(3)
TABLE_FIELDS = 5
TABLE_WIDTH = 64


def _params(*semantics):
    return pltpu.CompilerParams(dimension_semantics=semantics,
                                vmem_limit_bytes=V7X_VMEM_LIMIT_BYTES)


def _rms_norm(x, gain):
    ms = jnp.mean(x * x, axis=-1, keepdims=True)
    return x * lax.rsqrt(ms + RMS_EPS) * gain


def _silu(x):
    return x / (1.0 + jnp.exp(-x))


def _split_bf16(x, pieces):
    out = []
    for _ in range(pieces - 1):
        top = x.astype(BF16)
        out.append(top)
        x = x - top.astype(F32)
    out.append(x.astype(BF16))
    return out


CONTRACT_LAST = (((1,), (1,)), ((), ()))
CONTRACT_FIRST = (((0,), (0,)), ((), ()))


def _qkv_kernel(x_ref, g_ref, w_ref, o_ref):
    h = _rms_norm(x_ref[...], g_ref[...]).astype(BF16)
    o_ref[...] = jnp.dot(h, w_ref[...], preferred_element_type=F32).astype(BF16)


def _qkv_proj(x2d, gain, w_qkv):
    t, d = x2d.shape
    n = w_qkv.shape[1]
    return pl.pallas_call(
        _qkv_kernel,
        grid=(t // QKV_ROWS,),
        in_specs=[pl.BlockSpec((QKV_ROWS, d), lambda i: (i, 0)),
                  pl.BlockSpec((1, d), lambda i: (0, 0)),
                  pl.BlockSpec((d, n), lambda i: (0, 0))],
        out_specs=pl.BlockSpec((QKV_ROWS, n), lambda i: (i, 0)),
        out_shape=jax.ShapeDtypeStruct((t, n), BF16),
        compiler_params=_params("parallel"),
        name="rms_qkv",
    )(x2d, gain, w_qkv)


def _attn_kernel(bias_ref, sink_ref, q_ref, kc_ref, kp_ref, vc_ref, vp_ref, x_ref, wo_ref, o_ref,
                 ot_ref, *, n_heads):
    group = n_heads // N_KV_HEADS
    gw = group * WINDOW
    j = pl.program_id(1)
    key = lax.broadcasted_iota(jnp.int32, (WINDOW, n_heads * WINDOW), 0)
    query = lax.broadcasted_iota(jnp.int32, (WINDOW, n_heads * WINDOW), 1) & (WINDOW - 1)
    from_prev = key > query
    from_prev_bf = from_prev.astype(BF16)
    no_prev = from_prev & (j == 0)
    sink = sink_ref[...]

    n_blocks = q_ref.shape[1] // WINDOW
    for blk in range(n_blocks):
        rows = slice(blk * WINDOW, (blk + 1) * WINDOW)
        q_blk = q_ref[0, rows, :] * jnp.asarray(HEAD_DIM ** -0.5, BF16)
        if blk == 0:
            k_prev, v_prev = kp_ref[0], vp_ref[0]
        else:
            prev = slice((blk - 1) * WINDOW, blk * WINDOW)
            k_prev, v_prev = kc_ref[0, prev, :], vc_ref[0, prev, :]
        k_cat = jnp.concatenate([k_prev, kc_ref[0, rows, :]], axis=0)
        v_cat = jnp.concatenate([v_prev, vc_ref[0, rows, :]], axis=0)
        scores = []
        for kvh in range(N_KV_HEADS):
            q_g = jnp.concatenate([q_blk[:, h * HEAD_DIM:(h + 1) * HEAD_DIM]
                                   for h in range(kvh * group, (kvh + 1) * group)], axis=0)
            k_h = k_cat[:, kvh * HEAD_DIM:(kvh + 1) * HEAD_DIM]
            scores.append(lax.dot_general(k_h, q_g, CONTRACT_LAST, preferred_element_type=F32))
        s_all = jnp.concatenate(scores, axis=1)
        s = jnp.where(from_prev, s_all[:WINDOW], s_all[WINDOW:]) + bias_ref[...]
        if blk == 0:
            s = jnp.where(no_prev, NEG_INF, s)
        m = jnp.maximum(jnp.max(s, axis=0, keepdims=True), sink)
        e = jnp.exp(s - m)
        inv_denom = 1.0 / (jnp.sum(e, axis=0, keepdims=True) + jnp.exp(sink - m))
        e_bf = e.astype(BF16)
        p_prev = e_bf * from_prev_bf
        p_cat = jnp.concatenate([p_prev, e_bf - p_prev], axis=0)
        for kvh in range(N_KV_HEADS):
            lanes = slice(kvh * gw, (kvh + 1) * gw)
            v_h = v_cat[:, kvh * HEAD_DIM:(kvh + 1) * HEAD_DIM]
            o_t = lax.dot_general(v_h, p_cat[:, lanes], CONTRACT_FIRST,
                                  preferred_element_type=F32)
            o_t = (o_t * inv_denom[:, lanes]).astype(BF16)
            for g in range(group):
                h = kvh * group + g
                ot_ref[h * HEAD_DIM:(h + 1) * HEAD_DIM, rows] = o_t[:, g * WINDOW:(g + 1) * WINDOW]
    o_ref[0] = x_ref[0] + lax.dot_general(ot_ref[...], wo_ref[...], CONTRACT_FIRST,
                                          preferred_element_type=F32)


def _alibi_band_bias(n_heads):
    slopes = jnp.exp2(-8.0 * jnp.arange(1, n_heads + 1, dtype=F32) / n_heads)
    c = jnp.arange(WINDOW)[:, None, None]
    r = jnp.arange(WINDOW)[None, None, :]
    dist = ((r - c) % WINDOW).astype(F32)
    return (-slopes[None, :, None] * dist).reshape(WINDOW, n_heads * WINDOW)


def _attention(x, qkv, sinks, w_o):
    b, s, d = x.shape
    n_heads = w_o.shape[0] // HEAD_DIM
    q_width = n_heads * HEAD_DIM
    kv_width = N_KV_HEADS * HEAD_DIM
    k_col = q_width // kv_width
    v_col = k_col + 1
    blocks_per_step = ATTN_ROWS // WINDOW
    sink_row = jnp.repeat(sinks, WINDOW)[None, :]

    def prev_block(bi, j):
        return jnp.maximum(j * blocks_per_step - 1, 0)

    kernel = functools.partial(_attn_kernel, n_heads=n_heads)
    return pl.pallas_call(
        kernel,
        grid=(b, s // ATTN_ROWS),
        in_specs=[
            pl.BlockSpec((WINDOW, n_heads * WINDOW), lambda bi, j: (0, 0)),
            pl.BlockSpec((1, n_heads * WINDOW), lambda bi, j: (0, 0)),
            pl.BlockSpec((1, ATTN_ROWS, q_width), lambda bi, j: (bi, j, 0)),
            pl.BlockSpec((1, ATTN_ROWS, kv_width), lambda bi, j: (bi, j, k_col)),
            pl.BlockSpec((1, WINDOW, kv_width), lambda bi, j: (bi, prev_block(bi, j), k_col)),
            pl.BlockSpec((1, ATTN_ROWS, kv_width), lambda bi, j: (bi, j, v_col)),
            pl.BlockSpec((1, WINDOW, kv_width), lambda bi, j: (bi, prev_block(bi, j), v_col)),
            pl.BlockSpec((1, ATTN_ROWS, d), lambda bi, j: (bi, j, 0)),
            pl.BlockSpec((q_width, d), lambda bi, j: (0, 0)),
        ],
        out_specs=pl.BlockSpec((1, ATTN_ROWS, d), lambda bi, j: (bi, j, 0)),
        out_shape=jax.ShapeDtypeStruct((b, s, d), F32),
        scratch_shapes=[pltpu.VMEM((q_width, ATTN_ROWS), BF16)],
        compiler_params=_params("parallel", "parallel"),
        name="swa_attention",
    )(_alibi_band_bias(n_heads), sink_row, qkv, qkv, qkv, qkv, qkv, x, w_o)


def _ffn_kernel(x_ref, g_ref, wg_ref, wu_ref, wd_ref, o_ref, act_ref):
    x = x_ref[...]
    h = _rms_norm(x, g_ref[...]).astype(BF16)
    d_ff = wg_ref.shape[1]
    for c in range(d_ff // V7X_MXU_DIM):
        cols = slice(c * V7X_MXU_DIM, (c + 1) * V7X_MXU_DIM)
        gate = jnp.dot(h, wg_ref[:, cols], preferred_element_type=F32)
        up = jnp.dot(h, wu_ref[:, cols], preferred_element_type=F32)
        act_ref[:, cols] = (_silu(gate) * up).astype(BF16)
    o_ref[...] = x + jnp.dot(act_ref[...], wd_ref[...], preferred_element_type=F32)


def _dense_ffn(x2d, gain, w_gate, w_up, w_down):
    t, d = x2d.shape
    d_ff = w_gate.shape[1]
    resident = dict(pipeline_mode=pl.Buffered(1))
    return pl.pallas_call(
        _ffn_kernel,
        grid=(t // FFN_ROWS,),
        in_specs=[pl.BlockSpec((FFN_ROWS, d), lambda i: (i, 0)),
                  pl.BlockSpec((1, d), lambda i: (0, 0)),
                  pl.BlockSpec((d, d_ff), lambda i: (0, 0), **resident),
                  pl.BlockSpec((d, d_ff), lambda i: (0, 0), **resident),
                  pl.BlockSpec((d_ff, d), lambda i: (0, 0), **resident)],
        out_specs=pl.BlockSpec((FFN_ROWS, d), lambda i: (i, 0)),
        out_shape=jax.ShapeDtypeStruct((t, d), F32),
        scratch_shapes=[pltpu.VMEM((FFN_ROWS, d_ff), BF16)],
        compiler_params=_params("parallel"),
        name="rms_swiglu",
    )(x2d, gain, w_gate, w_up, w_down)


def _pool_route_kernel(x_ref, halo_ref, pn_ref, ps_ref, mn_ref, wp_ref, wrt_ref,
                       x3_ref, h3_ref, route_ref, routet_ref, info_ref, running_ref, *, n_experts):
    bi, j = pl.program_id(0), pl.program_id(1)
    rows = x_ref.shape[1]

    @pl.when((bi == 0) & (j == 0))
    def _():
        running_ref[...] = jnp.zeros_like(running_ref)

    x = x_ref[0]
    h = _rms_norm(x, pn_ref[...])
    halo = jnp.where(j > 0, _rms_norm(halo_ref[0], pn_ref[...]), 0.0)
    y = jnp.concatenate([halo, h], axis=0)
    pos = j * rows + lax.broadcasted_iota(jnp.int32, (rows, 1), 0)
    gdim = wp_ref.shape[1]
    mixed = []
    for g, w in enumerate(POOL_WINDOWS):
        cols = slice(g * gdim, (g + 1) * gdim)
        acc = y[:, cols]
        span = 1
        while span < w:
            acc = acc + pltpu.roll(acc, span, axis=0)
            span *= 2
        count = jnp.minimum(pos + 1, w).astype(F32)
        p = acc[POOL_HALO:, :] / count - h[:, cols]
        mixed.append(jnp.dot(p.astype(BF16), wp_ref[g], preferred_element_type=F32))
    x3 = x + jnp.concatenate(mixed, axis=-1) * ps_ref[...]
    x3_ref[0] = x3

    h_hi, h_mid, h_lo = _split_bf16(_rms_norm(x3, mn_ref[...]), 3)
    h3_ref[0] = h_hi
    w_hi, w_mid, w_lo = _split_bf16(wrt_ref[...], 3)
    es = ROUTE_SUBLANES
    a = lax.dot_general(jnp.concatenate([w_hi, w_mid, w_lo], axis=0), h_hi, CONTRACT_LAST,
                        preferred_element_type=F32)
    b = lax.dot_general(jnp.concatenate([w_hi, w_mid], axis=0), h_mid, CONTRACT_LAST,
                        preferred_element_type=F32)
    c = lax.dot_general(w_hi, h_lo, CONTRACT_LAST, preferred_element_type=F32)
    logits = ((c + b[es:] + a[2 * es:]) + (b[:es] + a[es:2 * es])) + a[:es]

    eidx = lax.broadcasted_iota(jnp.int32, (es, rows), 0).astype(F32)
    logits = jnp.where(eidx < n_experts, logits, -jnp.inf)
    v1 = jnp.max(logits, axis=0, keepdims=True)
    e1 = jnp.min(jnp.where(logits == v1, eidx, float(es)), axis=0, keepdims=True)
    rest = jnp.where(eidx == e1, -jnp.inf, logits)
    v2 = jnp.max(rest, axis=0, keepdims=True)
    e2 = jnp.min(jnp.where(rest == v2, eidx, float(es)), axis=0, keepdims=True)
    ex = jnp.exp(v2 - v1)
    g1 = 1.0 / (1.0 + ex)
    g2 = ex / (1.0 + ex)

    chosen = ((eidx == e1) | (eidx == e2)).astype(F32)
    t_i = lax.broadcasted_iota(jnp.int32, (rows, rows), 0)
    t_j = lax.broadcasted_iota(jnp.int32, (rows, rows), 1)
    earlier = (t_i < t_j).astype(BF16)
    local_rank = jnp.dot(chosen.astype(BF16), earlier, preferred_element_type=F32)
    count = jnp.sum(chosen, axis=1, keepdims=True)
    padded = jnp.floor((count + (RUN_ALIGN - 1)) * (1.0 / RUN_ALIGN)) * RUN_ALIGN
    ecol = lax.broadcasted_iota(jnp.int32, (es, 1), 0)
    lstart = jnp.zeros((es, 1), F32)
    for e in range(n_experts):
        lstart = lstart + jnp.where(ecol > e, padded[e:e + 1, :], 0.0)
    lpos = lstart + local_rank
    l1 = jnp.sum(jnp.where(eidx == e1, lpos, 0.0), axis=0, keepdims=True)
    l2 = jnp.sum(jnp.where(eidx == e2, lpos, 0.0), axis=0, keepdims=True)

    base = running_ref[...]
    running_ref[...] = base + padded
    lane = lax.broadcasted_iota(jnp.int32, (es, V7X_LANES), 1)
    info_ref[0] = jnp.where(lane == INFO_BASE, base,
                            jnp.where(lane == INFO_COUNT, padded,
                                      jnp.where(lane == INFO_LSTART, lstart, 0.0)))

    rec = jnp.zeros((es, rows), F32)
    for idx, val in ((ROUTE_L1, l1), (ROUTE_L2, l2), (ROUTE_G1, g1), (ROUTE_G2, g2),
                     (ROUTE_E1, e1), (ROUTE_E2, e2)):
        rec = jnp.where(eidx == idx, val, rec)
    routet_ref[0] = rec
    rec_full = jnp.concatenate([rec, jnp.zeros((V7X_LANES - es, rows), F32)], axis=0)
    route_ref[0] = rec_full.T


def _pool_route(x, pool_norm, pool_scale, moe_norm, w_pool, w_router_t, n_experts):
    b, s, d = x.shape
    tiles_per_seq = s // ROUTE_ROWS
    n_tiles = b * tiles_per_seq
    halo_blocks = ROUTE_ROWS // POOL_HALO
    es = ROUTE_SUBLANES
    vec = pl.BlockSpec((1, d), lambda bi, j: (0, 0))
    tile = pl.BlockSpec((1, ROUTE_ROWS, d), lambda bi, j: (bi, j, 0))
    kernel = functools.partial(_pool_route_kernel, n_experts=n_experts)
    return pl.pallas_call(
        kernel,
        grid=(b, tiles_per_seq),
        in_specs=[tile,
                  pl.BlockSpec((1, POOL_HALO, d),
                               lambda bi, j: (bi, jnp.maximum(j * halo_blocks - 1, 0), 0)),
                  vec, vec, vec,
                  pl.BlockSpec(w_pool.shape, lambda bi, j: (0, 0, 0)),
                  pl.BlockSpec(w_router_t.shape, lambda bi, j: (0, 0))],
        out_specs=[tile, tile,
                   pl.BlockSpec((1, ROUTE_ROWS, V7X_LANES), lambda bi, j: (bi, j, 0)),
                   pl.BlockSpec((1, es, ROUTE_ROWS), lambda bi, j: (bi * tiles_per_seq + j, 0, 0)),
                   pl.BlockSpec((1, es, V7X_LANES), lambda bi, j: (bi * tiles_per_seq + j, 0, 0))],
        out_shape=[jax.ShapeDtypeStruct((b, s, d), F32),
                   jax.ShapeDtypeStruct((b, s, d), BF16),
                   jax.ShapeDtypeStruct((b, s, V7X_LANES), F32),
                   jax.ShapeDtypeStruct((n_tiles, es, ROUTE_ROWS), F32),
                   jax.ShapeDtypeStruct((n_tiles, es, V7X_LANES), F32)],
        scratch_shapes=[pltpu.VMEM((es, V7X_LANES), F32)],
        compiler_params=_params("arbitrary", "arbitrary"),
        name="pool_route",
    )(x, x, pool_norm, pool_scale, moe_norm, w_pool, w_router_t)


def _chunks(count, lo_bit, hi_bit):
    for bit in range(lo_bit, hi_bit):
        size = 1 << bit
        offset = (count >> (bit + 1)) << (bit + 1)
        yield offset, size, (count & size) != 0


def _for_each_run_piece(tab_ref, n_experts, max_rows, fn):
    for e in range(n_experts):
        local = tab_ref[0, 0, e]
        sorted_start = tab_ref[0, 0, n_experts + e]
        count = tab_ref[0, 0, 2 * n_experts + e]
        for offset, size, present in _chunks(count, RUN_ALIGN_BITS, max_rows.bit_length()):
            @pl.when(present)
            def _(offset=offset, size=size, local=local, sorted_start=sorted_start):
                fn(pl.multiple_of(local + offset, RUN_ALIGN),
                   pl.multiple_of(sorted_start + offset, RUN_ALIGN), size)


def _wait_tile_runs(tab_ref, n_experts, max_rows, make_copy):
    total = tab_ref[0, 0, TABLE_FIELDS * n_experts]
    for _, size, present in _chunks(total, RUN_ALIGN_BITS, max_rows.bit_length()):
        @pl.when(present)
        def _(size=size):
            make_copy(0, 0, size).wait()


def _slot_rows(n_experts):
    pad = n_experts * (RUN_ALIGN - 1)
    return TOP_K * ROUTE_ROWS + (-(-pad // ROUTE_SUBLANES)) * ROUTE_SUBLANES


ZERO_ROWS = MOE_ROWS // 2


def _dispatch_kernel(tab_ref, prev_tab_ref, h_ref, rt_ref, xs_ref, buf_ref, zero_ref, sem, fill_sem,
                     *, n_experts, max_tail):
    i, n = pl.program_id(0), pl.num_programs(0)
    rows, d = h_ref.shape
    slots = buf_ref.shape[1]
    cur = i % 2

    rt = rt_ref[0]
    l1 = rt[ROUTE_L1:ROUTE_L1 + 1].astype(jnp.int32)
    l2 = rt[ROUTE_L2:ROUTE_L2 + 1].astype(jnp.int32)
    g1 = rt[ROUTE_G1:ROUTE_G1 + 1]
    g2 = rt[ROUTE_G2:ROUTE_G2 + 1]
    pos = lax.broadcasted_iota(jnp.int32, (slots, rows), 0)
    first = pos == l1
    second = pos == l2
    perm = (first | second).astype(BF16)
    x_sorted = jnp.dot(perm, h_ref[...], preferred_element_type=F32)
    gate = jnp.sum(jnp.where(first, g1, 0.0) + jnp.where(second, g2, 0.0), axis=1, keepdims=True)
    lane = lax.broadcasted_iota(jnp.int32, (slots, V7X_LANES), 1)
    gate_lanes = jnp.zeros((slots, V7X_LANES), F32)
    for idx, piece in enumerate(_split_bf16(gate, GATE_PIECES)):
        gate_lanes = jnp.where(lane == idx, piece.astype(F32), gate_lanes)
    buf_ref[cur] = jnp.concatenate([x_sorted, gate_lanes], axis=1).astype(BF16)

    def run_copy(which):
        def make(local, sorted_start, size):
            return pltpu.make_async_copy(buf_ref.at[which, pl.ds(local, size)],
                                         xs_ref.at[pl.ds(sorted_start, size)], sem.at[which])
        return make

    _for_each_run_piece(tab_ref, n_experts, rows,
                        lambda a, b, size: run_copy(cur)(a, b, size).start())

    @pl.when(i > 0)
    def _():
        _wait_tile_runs(prev_tab_ref, n_experts, slots, run_copy(1 - cur))

    @pl.when(i == n - 1)
    def _():
        _wait_tile_runs(tab_ref, n_experts, slots, run_copy(cur))
        zero_ref[...] = jnp.zeros_like(zero_ref)

        def fill_pieces(fn):
            for e in range(n_experts):
                start = tab_ref[0, 0, 3 * n_experts + e]
                length = tab_ref[0, 0, 4 * n_experts + e]
                for offset, size, present in _chunks(length, RUN_ALIGN_BITS, ZERO_ROWS.bit_length()):
                    @pl.when(present)
                    def _(offset=offset, size=size, start=start):
                        fn(pltpu.make_async_copy(
                            zero_ref.at[pl.ds(0, size)],
                            xs_ref.at[pl.ds(pl.multiple_of(start + offset, RUN_ALIGN), size)],
                            fill_sem))
            used_rows = tab_ref[0, 0, TABLE_FIELDS * n_experts + 1]
            for k in range(0, max_tail, ZERO_ROWS):
                @pl.when(used_rows + k < xs_ref.shape[0])
                def _(k=k):
                    fn(pltpu.make_async_copy(
                        zero_ref,
                        xs_ref.at[pl.ds(pl.multiple_of(used_rows + k, RUN_ALIGN), ZERO_ROWS)],
                        fill_sem))

        fill_pieces(lambda c: c.start())
        fill_pieces(lambda c: c.wait())


def _dispatch(h2d, route_t, table, n_rows, n_experts):
    t, d = h2d.shape
    n_tiles = t // ROUTE_ROWS
    slots = _slot_rows(n_experts)
    width = d + V7X_LANES
    tab_block = (1, 1, TABLE_WIDTH)
    kernel = functools.partial(_dispatch_kernel, n_experts=n_experts, max_tail=n_rows - TOP_K * t)
    return pl.pallas_call(
        kernel,
        grid=(n_tiles,),
        in_specs=[pl.BlockSpec(tab_block, lambda i: (i, 0, 0), memory_space=pltpu.SMEM),
                  pl.BlockSpec(tab_block, lambda i: (jnp.maximum(i - 1, 0), 0, 0),
                               memory_space=pltpu.SMEM),
                  pl.BlockSpec((ROUTE_ROWS, d), lambda i: (i, 0)),
                  pl.BlockSpec((1, ROUTE_SUBLANES, ROUTE_ROWS), lambda i: (i, 0, 0))],
        out_specs=pl.BlockSpec(memory_space=pl.ANY),
        out_shape=jax.ShapeDtypeStruct((n_rows, width), BF16),
        scratch_shapes=[pltpu.VMEM((2, slots, width), BF16),
                        pltpu.VMEM((ZERO_ROWS, width), BF16),
                        pltpu.SemaphoreType.DMA((2,)),
                        pltpu.SemaphoreType.DMA(())],
        compiler_params=_params("arbitrary"),
        name="moe_dispatch",
    )(table, table, h2d, route_t)


CHUNKS_PER_TILE = 2


def _moe_kernel(block_ref, expert_ref, used_ref, first_ref, slot_ref, step_ref, next_ref, ready_ref,
                x_ref, wg_hbm, wu_hbm, wd_hbm, o_ref,
                wg_buf, wu_buf, wd_buf, stage_g, stage_u, stage_d, act_ref, sem):
    v = pl.program_id(0)
    d = o_ref.shape[1]
    n_chunks = wg_buf.shape[1]
    cw = wg_buf.shape[3]

    def chunk_copies(e, j):
        at = pl.multiple_of(j * cw, cw)
        return (pltpu.make_async_copy(wg_hbm.at[e, :, pl.ds(at, cw)], stage_g, sem.at[0]),
                pltpu.make_async_copy(wu_hbm.at[e, :, pl.ds(at, cw)], stage_u, sem.at[1]),
                pltpu.make_async_copy(wd_hbm.at[e, pl.ds(at, cw), :], stage_d, sem.at[2]))

    def start(e, j):
        for copy in chunk_copies(e, j):
            copy.start()

    def finish(e, j, s):
        for copy in chunk_copies(e, j):
            copy.wait()
        wg_buf[s, j] = stage_g[...].astype(BF16)
        wu_buf[s, j] = stage_u[...].astype(BF16)
        wd_buf[s, pl.ds(pl.multiple_of(j * cw, cw), cw), :] = stage_d[...].astype(BF16)

    @pl.when(used_ref[v] == 0)
    def _():
        o_ref[...] = jnp.zeros_like(o_ref)

    @pl.when(used_ref[v] != 0)
    def _():
        e, s, nxt = expert_ref[v], slot_ref[v], next_ref[v]

        @pl.when(first_ref[v] != 0)
        def _():
            def fetch(j, carry):
                start(e, j)
                finish(e, j, s)
                return carry
            lax.fori_loop(ready_ref[v], n_chunks, fetch, 0)

        ahead = [jnp.minimum(CHUNKS_PER_TILE * step_ref[v] + k, n_chunks - 1)
                 for k in range(CHUNKS_PER_TILE)]
        xb = x_ref[:, :d]

        def hidden_chunk(c):
            cols = slice(c * cw, (c + 1) * cw)
            gate = jnp.dot(xb, wg_buf[s, c], preferred_element_type=F32)
            up = jnp.dot(xb, wu_buf[s, c], preferred_element_type=F32)
            act_ref[:, cols] = (_silu(gate) * up).astype(BF16)

        bounds = [n_chunks * k // CHUNKS_PER_TILE for k in range(CHUNKS_PER_TILE + 1)]
        start(nxt, ahead[0])
        for k in range(CHUNKS_PER_TILE):
            for c in range(bounds[k], bounds[k + 1]):
                hidden_chunk(c)
            if k + 1 < CHUNKS_PER_TILE:
                finish(nxt, ahead[k], 1 - s)
                start(nxt, ahead[k + 1])
        y = jnp.dot(act_ref[...], wd_buf[s], preferred_element_type=F32)
        pieces = x_ref[:, d:].astype(F32)
        weight = pieces[:, 0:1]
        for idx in range(1, GATE_PIECES):
            weight = weight + pieces[:, idx:idx + 1]
        o_ref[...] = (y * weight).astype(BF16)
        finish(nxt, ahead[-1], 1 - s)


def _moe_experts(xs, plan, w_gate, w_up, w_down):
    n_rows, width = xs.shape
    n_experts, d, d_ff = w_gate.shape
    n_tiles = n_rows // MOE_ROWS
    cw = V7X_MXU_DIM
    n_chunks = d_ff // cw
    grid_spec = pltpu.PrefetchScalarGridSpec(
        num_scalar_prefetch=len(plan),
        grid=(n_tiles,),
        in_specs=[
            pl.BlockSpec((MOE_ROWS, width), lambda v, blk, *_: (blk[v], 0)),
            pl.BlockSpec(memory_space=pl.ANY),
            pl.BlockSpec(memory_space=pl.ANY),
            pl.BlockSpec(memory_space=pl.ANY),
        ],
        out_specs=pl.BlockSpec((MOE_ROWS, d), lambda v, *_: (v, 0)),
        scratch_shapes=[pltpu.VMEM((2, n_chunks, d, cw), BF16),
                        pltpu.VMEM((2, n_chunks, d, cw), BF16),
                        pltpu.VMEM((2, d_ff, d), BF16),
                        pltpu.VMEM((d, cw), F32), pltpu.VMEM((d, cw), F32), pltpu.VMEM((cw, d), F32),
                        pltpu.VMEM((MOE_ROWS, d_ff), BF16),
                        pltpu.SemaphoreType.DMA((3,))],
    )
    return pl.pallas_call(
        _moe_kernel,
        grid_spec=grid_spec,
        out_shape=jax.ShapeDtypeStruct((n_rows, d), BF16),
        compiler_params=_params("arbitrary"),
        name="moe_experts",
    )(*plan, xs, w_gate, w_up, w_down)


def _plan_tiles(group_tiles, n_tiles, n_chunks):
    n_experts = group_tiles.shape[0]
    ends = jnp.cumsum(group_tiles)
    used_tiles = ends[-1]
    tile = jnp.arange(n_tiles, dtype=jnp.int32)
    used = tile < used_tiles
    block = jnp.minimum(tile, used_tiles - 1)
    expert = jnp.sum(ends[None, :] <= block[:, None], axis=1).astype(jnp.int32)
    expert = jnp.minimum(expert, n_experts - 1)

    idx = jnp.arange(n_experts, dtype=jnp.int32)
    present = group_tiles > 0
    later = present[None, :] & (idx[None, :] > idx[:, None])
    next_expert = jnp.min(jnp.where(later, idx[None, :], n_experts), axis=1)
    next_expert = jnp.where(next_expert == n_experts, idx, next_expert)
    earlier = present[None, :] & (idx[None, :] < idx[:, None])
    prev_expert = jnp.max(jnp.where(earlier, idx[None, :], -1), axis=1)
    ready = jnp.where(prev_expert >= 0,
                      jnp.minimum(CHUNKS_PER_TILE * group_tiles[jnp.maximum(prev_expert, 0)], n_chunks),
                      0)
    run_slot = jnp.sum(earlier, axis=1) % 2
    step = block - (ends - group_tiles)[expert]
    first = used & (step == 0)
    cast = lambda a: a.astype(jnp.int32)
    return (cast(block), expert, cast(used), cast(first), cast(run_slot[expert]), cast(step),
            cast(next_expert[expert]), cast(ready[expert]))


def _combine_kernel(tab_ref, next_tab_ref, x_ref, route_ref, g_ref, ys_ref, o_ref, buf_ref, sem,
                    *, n_experts):
    i, n = pl.program_id(0), pl.num_programs(0)
    rows = x_ref.shape[0]
    slots = buf_ref.shape[1]
    cur = i % 2

    def run_copy(which):
        def make(local, sorted_start, size):
            return pltpu.make_async_copy(ys_ref.at[pl.ds(sorted_start, size)],
                                         buf_ref.at[which, pl.ds(local, size)], sem.at[which])
        return make

    @pl.when(i == 0)
    def _():
        buf_ref[...] = jnp.zeros_like(buf_ref)
        _for_each_run_piece(tab_ref, n_experts, rows,
                            lambda a, b, size: run_copy(0)(a, b, size).start())

    @pl.when(i + 1 < n)
    def _():
        _for_each_run_piece(next_tab_ref, n_experts, rows,
                            lambda a, b, size: run_copy(1 - cur)(a, b, size).start())

    _wait_tile_runs(tab_ref, n_experts, slots, run_copy(cur))

    route = route_ref[...]
    l1 = route[:, ROUTE_L1:ROUTE_L1 + 1].astype(jnp.int32)
    l2 = route[:, ROUTE_L2:ROUTE_L2 + 1].astype(jnp.int32)
    pos = lax.broadcasted_iota(jnp.int32, (rows, slots), 1)
    pick = ((pos == l1) | (pos == l2)).astype(BF16)
    y = x_ref[...] + jnp.dot(pick, buf_ref[cur], preferred_element_type=F32)
    o_ref[...] = _rms_norm(y, g_ref[...])


def _combine(x2d, route2d, table, ys, final_norm, n_experts):
    t, d = x2d.shape
    n_tiles = t // ROUTE_ROWS
    slots = _slot_rows(n_experts)
    tab_block = (1, 1, TABLE_WIDTH)
    kernel = functools.partial(_combine_kernel, n_experts=n_experts)
    return pl.pallas_call(
        kernel,
        grid=(n_tiles,),
        in_specs=[pl.BlockSpec(tab_block, lambda i: (i, 0, 0), memory_space=pltpu.SMEM),
                  pl.BlockSpec(tab_block, lambda i: (jnp.minimum(i + 1, n_tiles - 1), 0, 0),
                               memory_space=pltpu.SMEM),
                  pl.BlockSpec((ROUTE_ROWS, d), lambda i: (i, 0)),
                  pl.BlockSpec((ROUTE_ROWS, V7X_LANES), lambda i: (i, 0)),
                  pl.BlockSpec((1, d), lambda i: (0, 0)),
                  pl.BlockSpec(memory_space=pl.ANY)],
        out_specs=pl.BlockSpec((ROUTE_ROWS, d), lambda i: (i, 0)),
        out_shape=jax.ShapeDtypeStruct((t, d), F32),
        scratch_shapes=[pltpu.VMEM((2, slots, d), BF16),
                        pltpu.SemaphoreType.DMA((2,))],
        compiler_params=_params("arbitrary"),
        name="moe_combine",
    )(table, table, x2d, route2d, final_norm, ys)


def kernel(x, attn_norm, attn_w_qkv, attn_sinks, attn_w_o, ffn_norm, ffn_w_gate, ffn_w_up,
           ffn_w_down, pool_norm, pool_w, pool_scale, moe_norm, moe_w_router, moe_w_gate, moe_w_up,
           moe_w_down, final_norm):
    b, s, d = x.shape
    t = b * s
    n_experts = moe_w_router.shape[-1]
    assert s % ATTN_ROWS == 0 and s % ROUTE_ROWS == 0
    assert n_experts <= V7X_SUBLANES and TABLE_FIELDS * n_experts + 2 <= TABLE_WIDTH
    assert attn_norm.shape[0] == 1 and pool_norm.shape[0] == 1

    qkv = _qkv_proj(x.reshape(t, d), attn_norm[0][None], attn_w_qkv[0].astype(BF16))
    x1 = _attention(x, qkv.reshape(b, s, -1), attn_sinks[0].astype(F32), attn_w_o[0].astype(BF16))
    x2 = _dense_ffn(x1.reshape(t, d), ffn_norm[0][None], ffn_w_gate[0].astype(BF16),
                    ffn_w_up[0].astype(BF16), ffn_w_down[0].astype(BF16))

    w_router_t = jnp.pad(moe_w_router[0].T, ((0, ROUTE_SUBLANES - n_experts), (0, 0)))
    x3, h3, route, route_t, info = _pool_route(
        x2.reshape(b, s, d), pool_norm[0][None], pool_scale[0][None], moe_norm[0][None],
        pool_w[0].astype(BF16), w_router_t, n_experts)

    n_tiles = t // ROUTE_ROWS
    info = info[:, :n_experts, :].astype(jnp.int32)
    base, padded, lstart = info[..., INFO_BASE], info[..., INFO_COUNT], info[..., INFO_LSTART]
    group_rows = jnp.sum(padded, axis=0)
    group_tiles = -(-group_rows // MOE_ROWS)
    group_start = (jnp.cumsum(group_tiles) - group_tiles) * MOE_ROWS
    max_rows = (t * TOP_K + n_tiles * n_experts * (RUN_ALIGN - 1)
                + n_experts * (MOE_ROWS - RUN_ALIGN))
    n_rows = -(-max_rows // MOE_ROWS) * MOE_ROWS
    per_tile = lambda v: jnp.broadcast_to(v[None, :], (n_tiles, n_experts))
    table = jnp.concatenate(
        [lstart, group_start[None, :] + base, padded,
         per_tile(group_start + group_rows), per_tile(group_tiles * MOE_ROWS - group_rows),
         jnp.sum(padded, axis=1, keepdims=True),
         jnp.broadcast_to(jnp.sum(group_tiles) * MOE_ROWS, (n_tiles, 1)),
         jnp.zeros((n_tiles, TABLE_WIDTH - TABLE_FIELDS * n_experts - 2), jnp.int32)], axis=1)
    table = table.reshape(n_tiles, 1, TABLE_WIDTH)

    xs = _dispatch(h3.reshape(t, d), route_t, table, n_rows, n_experts)
    plan = _plan_tiles(group_tiles, n_rows // MOE_ROWS, moe_w_gate.shape[-1] // V7X_MXU_DIM)
    ys = _moe_experts(xs, plan, moe_w_gate[0], moe_w_up[0], moe_w_down[0])
    out = _combine(x3.reshape(t, d), route.reshape(t, V7X_LANES), table, ys, final_norm[None],
                   n_experts)
    return out.reshape(b, s, d)
```

```python
import functools

import jax
import jax.numpy as jnp
from jax import lax
from jax.experimental import pallas as pl
from jax.experimental.pallas import tpu as pltpu

F32 = jnp.float32
BF16 = jnp.bfloat16

HEAD_DIM = 64
N_KV_HEADS = 4
WINDOW = 128
POOL_WINDOWS = (2, 4, 8, 16)
TOP_K = 2
RMS_EPS = 1e-5
NEG_INF = -1e30

V7X_LANES = 128
V7X_SUBLANES = 8
V7X_BF16_TILE_ROWS = 16
V7X_MXU_DIM = 256
V7X_VMEM_LIMIT_BYTES = 60 * 1024 * 1024

QKV_ROWS = 1024
ATTN_ROWS = 512
FFN_ROWS = 1024
ROUTE_ROWS = 512
MOE_ROWS = 512
POOL_HALO = 16

RUN_ALIGN = V7X_BF16_TILE_ROWS
RUN_ALIGN_BITS = RUN_ALIGN.bit_length() - 1
GATE_PIECES = 3
ROUTE_SUBLANES = 16
ROUTE_L1, ROUTE_L2, ROUTE_G1, ROUTE_G2, ROUTE_E1, ROUTE_E2 = range(6)
INFO_BASE, INFO_COUNT, INFO_LSTART = range(3)
TABLE_FIELDS = 5
TABLE_WIDTH = 64


def _params(*semantics):
    return pltpu.CompilerParams(dimension_semantics=semantics,
                                vmem_limit_bytes=V7X_VMEM_LIMIT_BYTES)


def _rms_norm(x, gain):
    ms = jnp.mean(x * x, axis=-1, keepdims=True)
    return x * lax.rsqrt(ms + RMS_EPS) * gain


def _silu(x):
    return x / (1.0 + jnp.exp(-x))


def _split_bf16(x, pieces):
    out = []
    for _ in range(pieces - 1):
        top = x.astype(BF16)
        out.append(top)
        x = x - top.astype(F32)
    out.append(x.astype(BF16))
    return out


CONTRACT_LAST = (((1,), (1,)), ((), ()))
CONTRACT_FIRST = (((0,), (0,)), ((), ()))


def _qkv_kernel(x_ref, g_ref, w_ref, o_ref):
    h = _rms_norm(x_ref[...], g_ref[...]).astype(BF16)
    o_ref[...] = jnp.dot(h, w_ref[...], preferred_element_type=F32).astype(BF16)


def _qkv_proj(x2d, gain, w_qkv):
    t, d = x2d.shape
    n = w_qkv.shape[1]
    return pl.pallas_call(
        _qkv_kernel,
        grid=(t // QKV_ROWS,),
        in_specs=[pl.BlockSpec((QKV_ROWS, d), lambda i: (i, 0)),
                  pl.BlockSpec((1, d), lambda i: (0, 0)),
                  pl.BlockSpec((d, n), lambda i: (0, 0))],
        out_specs=pl.BlockSpec((QKV_ROWS, n), lambda i: (i, 0)),
        out_shape=jax.ShapeDtypeStruct((t, n), BF16),
        compiler_params=_params("parallel"),
        name="rms_qkv",
    )(x2d, gain, w_qkv)


def _attn_kernel(bias_ref, sink_ref, q_ref, kc_ref, kp_ref, vc_ref, vp_ref, x_ref, wo_ref, o_ref,
                 ot_ref, *, n_heads):
    group = n_heads // N_KV_HEADS
    gw = group * WINDOW
    j = pl.program_id(1)
    key = lax.broadcasted_iota(jnp.int32, (WINDOW, n_heads * WINDOW), 0)
    query = lax.broadcasted_iota(jnp.int32, (WINDOW, n_heads * WINDOW), 1) & (WINDOW - 1)
    from_prev = key > query
    from_prev_bf = from_prev.astype(BF16)
    no_prev = from_prev & (j == 0)
    sink = sink_ref[...]

    n_blocks = q_ref.shape[1] // WINDOW
    for blk in range(n_blocks):
        rows = slice(blk * WINDOW, (blk + 1) * WINDOW)
        q_blk = q_ref[0, rows, :] * jnp.asarray(HEAD_DIM ** -0.5, BF16)
        if blk == 0:
            k_prev, v_prev = kp_ref[0], vp_ref[0]
        else:
            prev = slice((blk - 1) * WINDOW, blk * WINDOW)
            k_prev, v_prev = kc_ref[0, prev, :], vc_ref[0, prev, :]
        k_cat = jnp.concatenate([k_prev, kc_ref[0, rows, :]], axis=0)
        v_cat = jnp.concatenate([v_prev, vc_ref[0, rows, :]], axis=0)
        scores = []
        for kvh in range(N_KV_HEADS):
            q_g = jnp.concatenate([q_blk[:, h * HEAD_DIM:(h + 1) * HEAD_DIM]
                                   for h in range(kvh * group, (kvh + 1) * group)], axis=0)
            k_h = k_cat[:, kvh * HEAD_DIM:(kvh + 1) * HEAD_DIM]
            scores.append(lax.dot_general(k_h, q_g, CONTRACT_LAST, preferred_element_type=F32))
        s_all = jnp.concatenate(scores, axis=1)
        s = jnp.where(from_prev, s_all[:WINDOW], s_all[WINDOW:]) + bias_ref[...]
        if blk == 0:
            s = jnp.where(no_prev, NEG_INF, s)
        m = jnp.maximum(jnp.max(s, axis=0, keepdims=True), sink)
        e = jnp.exp(s - m)
        inv_denom = 1.0 / (jnp.sum(e, axis=0, keepdims=True) + jnp.exp(sink - m))
        e_bf = e.astype(BF16)
        p_prev = e_bf * from_prev_bf
        p_cat = jnp.concatenate([p_prev, e_bf - p_prev], axis=0)
        for kvh in range(N_KV_HEADS):
            lanes = slice(kvh * gw, (kvh + 1) * gw)
            v_h = v_cat[:, kvh * HEAD_DIM:(kvh + 1) * HEAD_DIM]
            o_t = lax.dot_general(v_h, p_cat[:, lanes], CONTRACT_FIRST,
                                  preferred_element_type=F32)
            o_t = (o_t * inv_denom[:, lanes]).astype(BF16)
            for g in range(group):
                h = kvh * group + g
                ot_ref[h * HEAD_DIM:(h + 1) * HEAD_DIM, rows] = o_t[:, g * WINDOW:(g + 1) * WINDOW]
    o_ref[0] = x_ref[0] + lax.dot_general(ot_ref[...], wo_ref[...], CONTRACT_FIRST,
                                          preferred_element_type=F32)


def _alibi_band_bias(n_heads):
    slopes = jnp.exp2(-8.0 * jnp.arange(1, n_heads + 1, dtype=F32) / n_heads)
    c = jnp.arange(WINDOW)[:, None, None]
    r = jnp.arange(WINDOW)[None, None, :]
    dist = ((r - c) % WINDOW).astype(F32)
    return (-slopes[None, :, None] * dist).reshape(WINDOW, n_heads * WINDOW)


def _attention(x, qkv, sinks, w_o):
    b, s, d = x.shape
    n_heads = w_o.shape[0] // HEAD_DIM
    q_width = n_heads * HEAD_DIM
    kv_width = N_KV_HEADS * HEAD_DIM
    k_col = q_width // kv_width
    v_col = k_col + 1
    blocks_per_step = ATTN_ROWS // WINDOW
    sink_row = jnp.repeat(sinks, WINDOW)[None, :]

    def prev_block(bi, j):
        return jnp.maximum(j * blocks_per_step - 1, 0)

    kernel = functools.partial(_attn_kernel, n_heads=n_heads)
    return pl.pallas_call(
        kernel,
        grid=(b, s // ATTN_ROWS),
        in_specs=[
            pl.BlockSpec((WINDOW, n_heads * WINDOW), lambda bi, j: (0, 0)),
            pl.BlockSpec((1, n_heads * WINDOW), lambda bi, j: (0, 0)),
            pl.BlockSpec((1, ATTN_ROWS, q_width), lambda bi, j: (bi, j, 0)),
            pl.BlockSpec((1, ATTN_ROWS, kv_width), lambda bi, j: (bi, j, k_col)),
            pl.BlockSpec((1, WINDOW, kv_width), lambda bi, j: (bi, prev_block(bi, j), k_col)),
            pl.BlockSpec((1, ATTN_ROWS, kv_width), lambda bi, j: (bi, j, v_col)),
            pl.BlockSpec((1, WINDOW, kv_width), lambda bi, j: (bi, prev_block(bi, j), v_col)),
            pl.BlockSpec((1, ATTN_ROWS, d), lambda bi, j: (bi, j, 0)),
            pl.BlockSpec((q_width, d), lambda bi, j: (0, 0)),
        ],
        out_specs=pl.BlockSpec((1, ATTN_ROWS, d), lambda bi, j: (bi, j, 0)),
        out_shape=jax.ShapeDtypeStruct((b, s, d), F32),
        scratch_shapes=[pltpu.VMEM((q_width, ATTN_ROWS), BF16)],
        compiler_params=_params("parallel", "parallel"),
        name="swa_attention",
    )(_alibi_band_bias(n_heads), sink_row, qkv, qkv, qkv, qkv, qkv, x, w_o)


def _ffn_kernel(x_ref, g_ref, wg_ref, wu_ref, wd_ref, o_ref, act_ref):
    x = x_ref[...]
    h = _rms_norm(x, g_ref[...]).astype(BF16)
    d_ff = wg_ref.shape[1]
    for c in range(d_ff // V7X_MXU_DIM):
        cols = slice(c * V7X_MXU_DIM, (c + 1) * V7X_MXU_DIM)
        gate = jnp.dot(h, wg_ref[:, cols], preferred_element_type=F32)
        up = jnp.dot(h, wu_ref[:, cols], preferred_element_type=F32)
        act_ref[:, cols] = (_silu(gate) * up).astype(BF16)
    o_ref[...] = x + jnp.dot(act_ref[...], wd_ref[...], preferred_element_type=F32)


def _dense_ffn(x2d, gain, w_gate, w_up, w_down):
    t, d = x2d.shape
    d_ff = w_gate.shape[1]
    resident = dict(pipeline_mode=pl.Buffered(1))
    return pl.pallas_call(
        _ffn_kernel,
        grid=(t // FFN_ROWS,),
        in_specs=[pl.BlockSpec((FFN_ROWS, d), lambda i: (i, 0)),
                  pl.BlockSpec((1, d), lambda i: (0, 0)),
                  pl.BlockSpec((d, d_ff), lambda i: (0, 0), **resident),
                  pl.BlockSpec((d, d_ff), lambda i: (0, 0), **resident),
                  pl.BlockSpec((d_ff, d), lambda i: (0, 0), **resident)],
        out_specs=pl.BlockSpec((FFN_ROWS, d), lambda i: (i, 0)),
        out_shape=jax.ShapeDtypeStruct((t, d), F32),
        scratch_shapes=[pltpu.VMEM((FFN_ROWS, d_ff), BF16)],
        compiler_params=_params("parallel"),
        name="rms_swiglu",
    )(x2d, gain, w_gate, w_up, w_down)


def _pool_route_kernel(x_ref, halo_ref, pn_ref, ps_ref, mn_ref, wp_ref, wrt_ref,
                       x3_ref, h3_ref, route_ref, routet_ref, info_ref, running_ref, *, n_experts):
    bi, j = pl.program_id(0), pl.program_id(1)
    rows = x_ref.shape[1]

    @pl.when((bi == 0) & (j == 0))
    def _():
        running_ref[...] = jnp.zeros_like(running_ref)

    x = x_ref[0]
    h = _rms_norm(x, pn_ref[...])
    halo = jnp.where(j > 0, _rms_norm(halo_ref[0], pn_ref[...]), 0.0)
    y = jnp.concatenate([halo, h], axis=0)
    pos = j * rows + lax.broadcasted_iota(jnp.int32, (rows, 1), 0)
    gdim = wp_ref.shape[1]
    mixed = []
    for g, w in enumerate(POOL_WINDOWS):
        cols = slice(g * gdim, (g + 1) * gdim)
        acc = y[:, cols]
        span = 1
        while span < w:
            acc = acc + pltpu.roll(acc, span, axis=0)
            span *= 2
        count = jnp.minimum(pos + 1, w).astype(F32)
        p = acc[POOL_HALO:, :] / count - h[:, cols]
        mixed.append(jnp.dot(p.astype(BF16), wp_ref[g], preferred_element_type=F32))
    x3 = x + jnp.concatenate(mixed, axis=-1) * ps_ref[...]
    x3_ref[0] = x3

    h_hi, h_mid, h_lo = _split_bf16(_rms_norm(x3, mn_ref[...]), 3)
    h3_ref[0] = h_hi
    w_hi, w_mid, w_lo = _split_bf16(wrt_ref[...], 3)
    es = ROUTE_SUBLANES
    a = lax.dot_general(jnp.concatenate([w_hi, w_mid, w_lo], axis=0), h_hi, CONTRACT_LAST,
                        preferred_element_type=F32)
    b = lax.dot_general(jnp.concatenate([w_hi, w_mid], axis=0), h_mid, CONTRACT_LAST,
                        preferred_element_type=F32)
    c = lax.dot_general(w_hi, h_lo, CONTRACT_LAST, preferred_element_type=F32)
    logits = ((c + b[es:] + a[2 * es:]) + (b[:es] + a[es:2 * es])) + a[:es]

    eidx = lax.broadcasted_iota(jnp.int32, (es, rows), 0).astype(F32)
    logits = jnp.where(eidx < n_experts, logits, -jnp.inf)
    v1 = jnp.max(logits, axis=0, keepdims=True)
    e1 = jnp.min(jnp.where(logits == v1, eidx, float(es)), axis=0, keepdims=True)
    rest = jnp.where(eidx == e1, -jnp.inf, logits)
    v2 = jnp.max(rest, axis=0, keepdims=True)
    e2 = jnp.min(jnp.where(rest == v2, eidx, float(es)), axis=0, keepdims=True)
    ex = jnp.exp(v2 - v1)
    g1 = 1.0 / (1.0 + ex)
    g2 = ex / (1.0 + ex)

    chosen = ((eidx == e1) | (eidx == e2)).astype(F32)
    t_i = lax.broadcasted_iota(jnp.int32, (rows, rows), 0)
    t_j = lax.broadcasted_iota(jnp.int32, (rows, rows), 1)
    earlier = (t_i < t_j).astype(BF16)
    local_rank = jnp.dot(chosen.astype(BF16), earlier, preferred_element_type=F32)
    count = jnp.sum(chosen, axis=1, keepdims=True)
    padded = jnp.floor((count + (RUN_ALIGN - 1)) * (1.0 / RUN_ALIGN)) * RUN_ALIGN
    ecol = lax.broadcasted_iota(jnp.int32, (es, 1), 0)
    lstart = jnp.zeros((es, 1), F32)
    for e in range(n_experts):
        lstart = lstart + jnp.where(ecol > e, padded[e:e + 1, :], 0.0)
    lpos = lstart + local_rank
    l1 = jnp.sum(jnp.where(eidx == e1, lpos, 0.0), axis=0, keepdims=True)
    l2 = jnp.sum(jnp.where(eidx == e2, lpos, 0.0), axis=0, keepdims=True)

    base = running_ref[...]
    running_ref[...] = base + padded
    lane = lax.broadcasted_iota(jnp.int32, (es, V7X_LANES), 1)
    info_ref[0] = jnp.where(lane == INFO_BASE, base,
                            jnp.where(lane == INFO_COUNT, padded,
                                      jnp.where(lane == INFO_LSTART, lstart, 0.0)))

    rec = jnp.zeros((es, rows), F32)
    for idx, val in ((ROUTE_L1, l1), (ROUTE_L2, l2), (ROUTE_G1, g1), (ROUTE_G2, g2),
                     (ROUTE_E1, e1), (ROUTE_E2, e2)):
        rec = jnp.where(eidx == idx, val, rec)
    routet_ref[0] = rec
    rec_full = jnp.concatenate([rec, jnp.zeros((V7X_LANES - es, rows), F32)], axis=0)
    route_ref[0] = rec_full.T


def _pool_route(x, pool_norm, pool_scale, moe_norm, w_pool, w_router_t, n_experts):
    b, s, d = x.shape
    tiles_per_seq = s // ROUTE_ROWS
    n_tiles = b * tiles_per_seq
    halo_blocks = ROUTE_ROWS // POOL_HALO
    es = ROUTE_SUBLANES
    vec = pl.BlockSpec((1, d), lambda bi, j: (0, 0))
    tile = pl.BlockSpec((1, ROUTE_ROWS, d), lambda bi, j: (bi, j, 0))
    kernel = functools.partial(_pool_route_kernel, n_experts=n_experts)
    return pl.pallas_call(
        kernel,
        grid=(b, tiles_per_seq),
        in_specs=[tile,
                  pl.BlockSpec((1, POOL_HALO, d),
                               lambda bi, j: (bi, jnp.maximum(j * halo_blocks - 1, 0), 0)),
                  vec, vec, vec,
                  pl.BlockSpec(w_pool.shape, lambda bi, j: (0, 0, 0)),
                  pl.BlockSpec(w_router_t.shape, lambda bi, j: (0, 0))],
        out_specs=[tile, tile,
                   pl.BlockSpec((1, ROUTE_ROWS, V7X_LANES), lambda bi, j: (bi, j, 0)),
                   pl.BlockSpec((1, es, ROUTE_ROWS), lambda bi, j: (bi * tiles_per_seq + j, 0, 0)),
                   pl.BlockSpec((1, es, V7X_LANES), lambda bi, j: (bi * tiles_per_seq + j, 0, 0))],
        out_shape=[jax.ShapeDtypeStruct((b, s, d), F32),
                   jax.ShapeDtypeStruct((b, s, d), BF16),
                   jax.ShapeDtypeStruct((b, s, V7X_LANES), F32),
                   jax.ShapeDtypeStruct((n_tiles, es, ROUTE_ROWS), F32),
                   jax.ShapeDtypeStruct((n_tiles, es, V7X_LANES), F32)],
        scratch_shapes=[pltpu.VMEM((es, V7X_LANES), F32)],
        compiler_params=_params("arbitrary", "arbitrary"),
        name="pool_route",
    )(x, x, pool_norm, pool_scale, moe_norm, w_pool, w_router_t)


def _chunks(count, lo_bit, hi_bit):
    for bit in range(lo_bit, hi_bit):
        size = 1 << bit
        offset = (count >> (bit + 1)) << (bit + 1)
        yield offset, size, (count & size) != 0


def _for_each_run_piece(tab_ref, n_experts, max_rows, fn):
    for e in range(n_experts):
        local = tab_ref[0, 0, e]
        sorted_start = tab_ref[0, 0, n_experts + e]
        count = tab_ref[0, 0, 2 * n_experts + e]
        for offset, size, present in _chunks(count, RUN_ALIGN_BITS, max_rows.bit_length()):
            @pl.when(present)
            def _(offset=offset, size=size, local=local, sorted_start=sorted_start):
                fn(pl.multiple_of(local + offset, RUN_ALIGN),
                   pl.multiple_of(sorted_start + offset, RUN_ALIGN), size)


def _wait_tile_runs(tab_ref, n_experts, max_rows, make_copy):
    total = tab_ref[0, 0, TABLE_FIELDS * n_experts]
    for _, size, present in _chunks(total, RUN_ALIGN_BITS, max_rows.bit_length()):
        @pl.when(present)
        def _(size=size):
            make_copy(0, 0, size).wait()


def _slot_rows(n_experts):
    pad = n_experts * (RUN_ALIGN - 1)
    return TOP_K * ROUTE_ROWS + (-(-pad // ROUTE_SUBLANES)) * ROUTE_SUBLANES


ZERO_ROWS = MOE_ROWS // 2


def _dispatch_kernel(tab_ref, prev_tab_ref, h_ref, rt_ref, xs_ref, buf_ref, zero_ref, sem, fill_sem,
                     *, n_experts, max_tail):
    i, n = pl.program_id(0), pl.num_programs(0)
    rows, d = h_ref.shape
    slots = buf_ref.shape[1]
    cur = i % 2

    rt = rt_ref[0]
    l1 = rt[ROUTE_L1:ROUTE_L1 + 1].astype(jnp.int32)
    l2 = rt[ROUTE_L2:ROUTE_L2 + 1].astype(jnp.int32)
    g1 = rt[ROUTE_G1:ROUTE_G1 + 1]
    g2 = rt[ROUTE_G2:ROUTE_G2 + 1]
    pos = lax.broadcasted_iota(jnp.int32, (slots, rows), 0)
    first = pos == l1
    second = pos == l2
    perm = (first | second).astype(BF16)
    x_sorted = jnp.dot(perm, h_ref[...], preferred_element_type=F32)
    gate = jnp.sum(jnp.where(first, g1, 0.0) + jnp.where(second, g2, 0.0), axis=1, keepdims=True)
    lane = lax.broadcasted_iota(jnp.int32, (slots, V7X_LANES), 1)
    gate_lanes = jnp.zeros((slots, V7X_LANES), F32)
    for idx, piece in enumerate(_split_bf16(gate, GATE_PIECES)):
        gate_lanes = jnp.where(lane == idx, piece.astype(F32), gate_lanes)
    buf_ref[cur] = jnp.concatenate([x_sorted, gate_lanes], axis=1).astype(BF16)

    def run_copy(which):
        def make(local, sorted_start, size):
            return pltpu.make_async_copy(buf_ref.at[which, pl.ds(local, size)],
                                         xs_ref.at[pl.ds(sorted_start, size)], sem.at[which])
        return make

    _for_each_run_piece(tab_ref, n_experts, rows,
                        lambda a, b, size: run_copy(cur)(a, b, size).start())

    @pl.when(i > 0)
    def _():
        _wait_tile_runs(prev_tab_ref, n_experts, slots, run_copy(1 - cur))

    @pl.when(i == n - 1)
    def _():
        _wait_tile_runs(tab_ref, n_experts, slots, run_copy(cur))
        zero_ref[...] = jnp.zeros_like(zero_ref)

        def fill_pieces(fn):
            for e in range(n_experts):
                start = tab_ref[0, 0, 3 * n_experts + e]
                length = tab_ref[0, 0, 4 * n_experts + e]
                for offset, size, present in _chunks(length, RUN_ALIGN_BITS, ZERO_ROWS.bit_length()):
                    @pl.when(present)
                    def _(offset=offset, size=size, start=start):
                        fn(pltpu.make_async_copy(
                            zero_ref.at[pl.ds(0, size)],
                            xs_ref.at[pl.ds(pl.multiple_of(start + offset, RUN_ALIGN), size)],
                            fill_sem))
            used_rows = tab_ref[0, 0, TABLE_FIELDS * n_experts + 1]
            for k in range(0, max_tail, ZERO_ROWS):
                @pl.when(used_rows + k < xs_ref.shape[0])
                def _(k=k):
                    fn(pltpu.make_async_copy(
                        zero_ref,
                        xs_ref.at[pl.ds(pl.multiple_of(used_rows + k, RUN_ALIGN), ZERO_ROWS)],
                        fill_sem))

        fill_pieces(lambda c: c.start())
        fill_pieces(lambda c: c.wait())


def _dispatch(h2d, route_t, table, n_rows, n_experts):
    t, d = h2d.shape
    n_tiles = t // ROUTE_ROWS
    slots = _slot_rows(n_experts)
    width = d + V7X_LANES
    tab_block = (1, 1, TABLE_WIDTH)
    kernel = functools.partial(_dispatch_kernel, n_experts=n_experts, max_tail=n_rows - TOP_K * t)
    return pl.pallas_call(
        kernel,
        grid=(n_tiles,),
        in_specs=[pl.BlockSpec(tab_block, lambda i: (i, 0, 0), memory_space=pltpu.SMEM),
                  pl.BlockSpec(tab_block, lambda i: (jnp.maximum(i - 1, 0), 0, 0),
                               memory_space=pltpu.SMEM),
                  pl.BlockSpec((ROUTE_ROWS, d), lambda i: (i, 0)),
                  pl.BlockSpec((1, ROUTE_SUBLANES, ROUTE_ROWS), lambda i: (i, 0, 0))],
        out_specs=pl.BlockSpec(memory_space=pl.ANY),
        out_shape=jax.ShapeDtypeStruct((n_rows, width), BF16),
        scratch_shapes=[pltpu.VMEM((2, slots, width), BF16),
                        pltpu.VMEM((ZERO_ROWS, width), BF16),
                        pltpu.SemaphoreType.DMA((2,)),
                        pltpu.SemaphoreType.DMA(())],
        compiler_params=_params("arbitrary"),
        name="moe_dispatch",
    )(table, table, h2d, route_t)


WEIGHT_CHUNKS = 16
CHUNKS_PER_TILE = 2


def _moe_kernel(block_ref, expert_ref, used_ref, first_ref, slot_ref, step_ref, next_ref, ready_ref,
                x_ref, wg_hbm, wu_hbm, wd_hbm, o_ref,
                wg_buf, wu_buf, wd_buf, stage_g, stage_u, stage_d, act_ref, sem):
    v = pl.program_id(0)
    d = o_ref.shape[1]
    n_chunks = WEIGHT_CHUNKS
    cw = V7X_MXU_DIM
    in_rows = stage_g.shape[1]
    out_rows = stage_d.shape[1]

    def blocks(j):
        return (pl.ds(pl.multiple_of(j * in_rows, in_rows), in_rows),
                pl.ds(pl.multiple_of(j * out_rows, out_rows), out_rows))

    def chunk_copies(e, j, k):
        rin, rout = blocks(j)
        return (pltpu.make_async_copy(wg_hbm.at[e, rin, :], stage_g.at[k], sem.at[k, 0]),
                pltpu.make_async_copy(wu_hbm.at[e, rin, :], stage_u.at[k], sem.at[k, 1]),
                pltpu.make_async_copy(wd_hbm.at[e, rout, :], stage_d.at[k], sem.at[k, 2]))

    def start(e, j, k):
        for copy in chunk_copies(e, j, k):
            copy.start()

    def finish(e, j, k, s):
        for copy in chunk_copies(e, j, k):
            copy.wait()
        rin, rout = blocks(j)
        wg_buf[s, rin, :] = stage_g[k].astype(BF16)
        wu_buf[s, rin, :] = stage_u[k].astype(BF16)
        wd_buf[s, rout, :] = stage_d[k].astype(BF16)

    @pl.when(used_ref[v] == 0)
    def _():
        o_ref[...] = jnp.zeros_like(o_ref)

    @pl.when(used_ref[v] != 0)
    def _():
        e, s, nxt = expert_ref[v], slot_ref[v], next_ref[v]

        @pl.when(first_ref[v] != 0)
        def _():
            def fetch(j, carry):
                start(e, j, 0)
                finish(e, j, 0, s)
                return carry
            lax.fori_loop(ready_ref[v], n_chunks, fetch, 0)

        ahead = [jnp.minimum(CHUNKS_PER_TILE * step_ref[v] + k, n_chunks - 1)
                 for k in range(CHUNKS_PER_TILE)]
        xb = x_ref[:, :d]

        def hidden_chunk(c):
            cols = slice(c * cw, (c + 1) * cw)
            gate = jnp.dot(xb, wg_buf[s, :, cols], preferred_element_type=F32)
            up = jnp.dot(xb, wu_buf[s, :, cols], preferred_element_type=F32)
            act_ref[:, cols] = (_silu(gate) * up).astype(BF16)

        n_cols = act_ref.shape[1] // cw
        bounds = [n_cols * k // CHUNKS_PER_TILE for k in range(CHUNKS_PER_TILE + 1)]
        for k in range(CHUNKS_PER_TILE):
            start(nxt, ahead[k], k)
        for k in range(CHUNKS_PER_TILE):
            for c in range(bounds[k], bounds[k + 1]):
                hidden_chunk(c)
            if k + 1 < CHUNKS_PER_TILE:
                finish(nxt, ahead[k], k, 1 - s)
        y = jnp.dot(act_ref[...], wd_buf[s], preferred_element_type=F32)
        pieces = x_ref[:, d:].astype(F32)
        weight = pieces[:, 0:1]
        for idx in range(1, GATE_PIECES):
            weight = weight + pieces[:, idx:idx + 1]
        o_ref[...] = (y * weight).astype(BF16)
        finish(nxt, ahead[-1], CHUNKS_PER_TILE - 1, 1 - s)


def _moe_experts(xs, plan, w_gate, w_up, w_down):
    n_rows, width = xs.shape
    n_experts, d, d_ff = w_gate.shape
    n_tiles = n_rows // MOE_ROWS
    in_rows, out_rows = d // WEIGHT_CHUNKS, d_ff // WEIGHT_CHUNKS
    assert in_rows % V7X_BF16_TILE_ROWS == 0 and out_rows % V7X_BF16_TILE_ROWS == 0
    grid_spec = pltpu.PrefetchScalarGridSpec(
        num_scalar_prefetch=len(plan),
        grid=(n_tiles,),
        in_specs=[
            pl.BlockSpec((MOE_ROWS, width), lambda v, blk, *_: (blk[v], 0)),
            pl.BlockSpec(memory_space=pl.ANY),
            pl.BlockSpec(memory_space=pl.ANY),
            pl.BlockSpec(memory_space=pl.ANY),
        ],
        out_specs=pl.BlockSpec((MOE_ROWS, d), lambda v, *_: (v, 0)),
        scratch_shapes=[pltpu.VMEM((2, d, d_ff), BF16),
                        pltpu.VMEM((2, d, d_ff), BF16),
                        pltpu.VMEM((2, d_ff, d), BF16),
                        pltpu.VMEM((CHUNKS_PER_TILE, in_rows, d_ff), F32),
                        pltpu.VMEM((CHUNKS_PER_TILE, in_rows, d_ff), F32),
                        pltpu.VMEM((CHUNKS_PER_TILE, out_rows, d), F32),
                        pltpu.VMEM((MOE_ROWS, d_ff), BF16),
                        pltpu.SemaphoreType.DMA((CHUNKS_PER_TILE, 3))],
    )
    return pl.pallas_call(
        _moe_kernel,
        grid_spec=grid_spec,
        out_shape=jax.ShapeDtypeStruct((n_rows, d), BF16),
        compiler_params=_params("arbitrary"),
        name="moe_experts",
    )(*plan, xs, w_gate, w_up, w_down)


def _plan_tiles(group_tiles, n_tiles, n_chunks):
    n_experts = group_tiles.shape[0]
    ends = jnp.cumsum(group_tiles)
    used_tiles = ends[-1]
    tile = jnp.arange(n_tiles, dtype=jnp.int32)
    used = tile < used_tiles
    block = jnp.minimum(tile, used_tiles - 1)
    expert = jnp.sum(ends[None, :] <= block[:, None], axis=1).astype(jnp.int32)
    expert = jnp.minimum(expert, n_experts - 1)

    idx = jnp.arange(n_experts, dtype=jnp.int32)
    present = group_tiles > 0
    later = present[None, :] & (idx[None, :] > idx[:, None])
    next_expert = jnp.min(jnp.where(later, idx[None, :], n_experts), axis=1)
    next_expert = jnp.where(next_expert == n_experts, idx, next_expert)
    earlier = present[None, :] & (idx[None, :] < idx[:, None])
    prev_expert = jnp.max(jnp.where(earlier, idx[None, :], -1), axis=1)
    ready = jnp.where(prev_expert >= 0,
                      jnp.minimum(CHUNKS_PER_TILE * group_tiles[jnp.maximum(prev_expert, 0)], n_chunks),
                      0)
    run_slot = jnp.sum(earlier, axis=1) % 2
    step = block - (ends - group_tiles)[expert]
    first = used & (step == 0)
    cast = lambda a: a.astype(jnp.int32)
    return (cast(block), expert, cast(used), cast(first), cast(run_slot[expert]), cast(step),
            cast(next_expert[expert]), cast(ready[expert]))


def _combine_kernel(tab_ref, next_tab_ref, x_ref, route_ref, g_ref, ys_ref, o_ref, buf_ref, sem,
                    *, n_experts):
    i, n = pl.program_id(0), pl.num_programs(0)
    rows = x_ref.shape[0]
    slots = buf_ref.shape[1]
    cur = i % 2

    def run_copy(which):
        def make(local, sorted_start, size):
            return pltpu.make_async_copy(ys_ref.at[pl.ds(sorted_start, size)],
                                         buf_ref.at[which, pl.ds(local, size)], sem.at[which])
        return make

    @pl.when(i == 0)
    def _():
        buf_ref[...] = jnp.zeros_like(buf_ref)
        _for_each_run_piece(tab_ref, n_experts, rows,
                            lambda a, b, size: run_copy(0)(a, b, size).start())

    @pl.when(i + 1 < n)
    def _():
        _for_each_run_piece(next_tab_ref, n_experts, rows,
                            lambda a, b, size: run_copy(1 - cur)(a, b, size).start())

    _wait_tile_runs(tab_ref, n_experts, slots, run_copy(cur))

    route = route_ref[...]
    l1 = route[:, ROUTE_L1:ROUTE_L1 + 1].astype(jnp.int32)
    l2 = route[:, ROUTE_L2:ROUTE_L2 + 1].astype(jnp.int32)
    pos = lax.broadcasted_iota(jnp.int32, (rows, slots), 1)
    pick = ((pos == l1) | (pos == l2)).astype(BF16)
    y = x_ref[...] + jnp.dot(pick, buf_ref[cur], preferred_element_type=F32)
    o_ref[...] = _rms_norm(y, g_ref[...])


def _combine(x2d, route2d, table, ys, final_norm, n_experts):
    t, d = x2d.shape
    n_tiles = t // ROUTE_ROWS
    slots = _slot_rows(n_experts)
    tab_block = (1, 1, TABLE_WIDTH)
    kernel = functools.partial(_combine_kernel, n_experts=n_experts)
    return pl.pallas_call(
        kernel,
        grid=(n_tiles,),
        in_specs=[pl.BlockSpec(tab_block, lambda i: (i, 0, 0), memory_space=pltpu.SMEM),
                  pl.BlockSpec(tab_block, lambda i: (jnp.minimum(i + 1, n_tiles - 1), 0, 0),
                               memory_space=pltpu.SMEM),
                  pl.BlockSpec((ROUTE_ROWS, d), lambda i: (i, 0)),
                  pl.BlockSpec((ROUTE_ROWS, V7X_LANES), lambda i: (i, 0)),
                  pl.BlockSpec((1, d), lambda i: (0, 0)),
                  pl.BlockSpec(memory_space=pl.ANY)],
        out_specs=pl.BlockSpec((ROUTE_ROWS, d), lambda i: (i, 0)),
        out_shape=jax.ShapeDtypeStruct((t, d), F32),
        scratch_shapes=[pltpu.VMEM((2, slots, d), BF16),
                        pltpu.SemaphoreType.DMA((2,))],
        compiler_params=_params("arbitrary"),
        name="moe_combine",
    )(table, table, x2d, route2d, final_norm, ys)


def kernel(x, attn_norm, attn_w_qkv, attn_sinks, attn_w_o, ffn_norm, ffn_w_gate, ffn_w_up,
           ffn_w_down, pool_norm, pool_w, pool_scale, moe_norm, moe_w_router, moe_w_gate, moe_w_up,
           moe_w_down, final_norm):
    b, s, d = x.shape
    t = b * s
    n_experts = moe_w_router.shape[-1]
    assert s % ATTN_ROWS == 0 and s % ROUTE_ROWS == 0
    assert n_experts <= V7X_SUBLANES and TABLE_FIELDS * n_experts + 2 <= TABLE_WIDTH
    assert attn_norm.shape[0] == 1 and pool_norm.shape[0] == 1

    qkv = _qkv_proj(x.reshape(t, d), attn_norm[0][None], attn_w_qkv[0].astype(BF16))
    x1 = _attention(x, qkv.reshape(b, s, -1), attn_sinks[0].astype(F32), attn_w_o[0].astype(BF16))
    x2 = _dense_ffn(x1.reshape(t, d), ffn_norm[0][None], ffn_w_gate[0].astype(BF16),
                    ffn_w_up[0].astype(BF16), ffn_w_down[0].astype(BF16))

    w_router_t = jnp.pad(moe_w_router[0].T, ((0, ROUTE_SUBLANES - n_experts), (0, 0)))
    x3, h3, route, route_t, info = _pool_route(
        x2.reshape(b, s, d), pool_norm[0][None], pool_scale[0][None], moe_norm[0][None],
        pool_w[0].astype(BF16), w_router_t, n_experts)

    n_tiles = t // ROUTE_ROWS
    info = info[:, :n_experts, :].astype(jnp.int32)
    base, padded, lstart = info[..., INFO_BASE], info[..., INFO_COUNT], info[..., INFO_LSTART]
    group_rows = jnp.sum(padded, axis=0)
    group_tiles = -(-group_rows // MOE_ROWS)
    group_start = (jnp.cumsum(group_tiles) - group_tiles) * MOE_ROWS
    max_rows = (t * TOP_K + n_tiles * n_experts * (RUN_ALIGN - 1)
                + n_experts * (MOE_ROWS - RUN_ALIGN))
    n_rows = -(-max_rows // MOE_ROWS) * MOE_ROWS
    per_tile = lambda v: jnp.broadcast_to(v[None, :], (n_tiles, n_experts))
    table = jnp.concatenate(
        [lstart, group_start[None, :] + base, padded,
         per_tile(group_start + group_rows), per_tile(group_tiles * MOE_ROWS - group_rows),
         jnp.sum(padded, axis=1, keepdims=True),
         jnp.broadcast_to(jnp.sum(group_tiles) * MOE_ROWS, (n_tiles, 1)),
         jnp.zeros((n_tiles, TABLE_WIDTH - TABLE_FIELDS * n_experts - 2), jnp.int32)], axis=1)
    table = table.reshape(n_tiles, 1, TABLE_WIDTH)

    xs = _dispatch(h3.reshape(t, d), route_t, table, n_rows, n_experts)
    plan = _plan_tiles(group_tiles, n_rows // MOE_ROWS, WEIGHT_CHUNKS)
    ys = _moe_experts(xs, plan, moe_w_gate[0], moe_w_up[0], moe_w_down[0])
    out = _combine(x3.reshape(t, d), route.reshape(t, V7X_LANES), table, ys, final_norm[None],
                   n_experts)
    return out.reshape(b, s, d)
```

```python
import functools
import itertools

import jax
import jax.numpy as jnp
from jax import lax
from jax.experimental import pallas as pl
from jax.experimental.pallas import tpu as pltpu

F32 = jnp.float32
BF16 = jnp.bfloat16

HEAD_DIM = 64
N_KV_HEADS = 4
WINDOW = 128
POOL_WINDOWS = (2, 4, 8, 16)
TOP_K = 2
RMS_EPS = 1e-5
NEG_INF = -1e30

V7X_LANES = 128
V7X_SUBLANES = 8
V7X_BF16_TILE_ROWS = 16
V7X_MXU_DIM = 256
V7X_VMEM_LIMIT_BYTES = 60 * 1024 * 1024

QKV_ROWS = 1024
ATTN_ROWS = 512
ROUTE_ROWS = 512
MOE_ROWS = 512
POOL_HALO = 16

RUN_ALIGN = V7X_BF16_TILE_ROWS
RUN_ALIGN_BITS = RUN_ALIGN.bit_length() - 1
GATE_PIECES = 3
ROUTE_SUBLANES = 16
ROUTE_L1, ROUTE_L2, ROUTE_G1, ROUTE_G2, ROUTE_E1, ROUTE_E2 = range(6)
INFO_BASE, INFO_COUNT, INFO_LSTART = range(3)
TABLE_FIELDS = 5
TABLE_WIDTH = 64


def _params(*semantics):
    return pltpu.CompilerParams(dimension_semantics=semantics,
                                vmem_limit_bytes=V7X_VMEM_LIMIT_BYTES)


def _rms_norm(x, gain):
    ms = jnp.mean(x * x, axis=-1, keepdims=True)
    return x * lax.rsqrt(ms + RMS_EPS) * gain


def _silu(x):
    return x / (1.0 + jnp.exp(-x))


def _split_bf16(x, pieces):
    out = []
    for _ in range(pieces - 1):
        top = x.astype(BF16)
        out.append(top)
        x = x - top.astype(F32)
    out.append(x.astype(BF16))
    return out


CONTRACT_LAST = (((1,), (1,)), ((), ()))
CONTRACT_FIRST = (((0,), (0,)), ((), ()))


def _qkv_kernel(x_ref, g_ref, w_ref, o_ref):
    h = _rms_norm(x_ref[...], g_ref[...]).astype(BF16)
    o_ref[...] = jnp.dot(h, w_ref[...], preferred_element_type=F32).astype(BF16)


def _qkv_proj(x2d, gain, w_qkv):
    t, d = x2d.shape
    n = w_qkv.shape[1]
    return pl.pallas_call(
        _qkv_kernel,
        grid=(t // QKV_ROWS,),
        in_specs=[pl.BlockSpec((QKV_ROWS, d), lambda i: (i, 0)),
                  pl.BlockSpec((1, d), lambda i: (0, 0)),
                  pl.BlockSpec((d, n), lambda i: (0, 0))],
        out_specs=pl.BlockSpec((QKV_ROWS, n), lambda i: (i, 0)),
        out_shape=jax.ShapeDtypeStruct((t, n), BF16),
        compiler_params=_params("parallel"),
        name="rms_qkv",
    )(x2d, gain, w_qkv)


def _attn_kernel(bias_ref, sink_ref, q_ref, kc_ref, kp_ref, vc_ref, vp_ref, x_ref, wo_ref, o_ref,
                 ot_ref, *, n_heads):
    group = n_heads // N_KV_HEADS
    gw = group * WINDOW
    j = pl.program_id(1)
    key = lax.broadcasted_iota(jnp.int32, (WINDOW, n_heads * WINDOW), 0)
    query = lax.broadcasted_iota(jnp.int32, (WINDOW, n_heads * WINDOW), 1) & (WINDOW - 1)
    from_prev = key > query
    from_prev_bf = from_prev.astype(BF16)
    no_prev = from_prev & (j == 0)
    sink = sink_ref[...]

    n_blocks = q_ref.shape[1] // WINDOW
    for blk in range(n_blocks):
        rows = slice(blk * WINDOW, (blk + 1) * WINDOW)
        q_blk = q_ref[0, rows, :] * jnp.asarray(HEAD_DIM ** -0.5, BF16)
        if blk == 0:
            k_prev, v_prev = kp_ref[0], vp_ref[0]
        else:
            prev = slice((blk - 1) * WINDOW, blk * WINDOW)
            k_prev, v_prev = kc_ref[0, prev, :], vc_ref[0, prev, :]
        k_cat = jnp.concatenate([k_prev, kc_ref[0, rows, :]], axis=0)
        v_cat = jnp.concatenate([v_prev, vc_ref[0, rows, :]], axis=0)
        scores = []
        for kvh in range(N_KV_HEADS):
            q_g = jnp.concatenate([q_blk[:, h * HEAD_DIM:(h + 1) * HEAD_DIM]
                                   for h in range(kvh * group, (kvh + 1) * group)], axis=0)
            k_h = k_cat[:, kvh * HEAD_DIM:(kvh + 1) * HEAD_DIM]
            scores.append(lax.dot_general(k_h, q_g, CONTRACT_LAST, preferred_element_type=F32))
        s_all = jnp.concatenate(scores, axis=1)
        s = jnp.where(from_prev, s_all[:WINDOW], s_all[WINDOW:]) + bias_ref[...]
        if blk == 0:
            s = jnp.where(no_prev, NEG_INF, s)
        m = jnp.maximum(jnp.max(s, axis=0, keepdims=True), sink)
        e = jnp.exp(s - m)
        inv_denom = 1.0 / (jnp.sum(e, axis=0, keepdims=True) + jnp.exp(sink - m))
        e_bf = e.astype(BF16)
        p_prev = e_bf * from_prev_bf
        p_cat = jnp.concatenate([p_prev, e_bf - p_prev], axis=0)
        for kvh in range(N_KV_HEADS):
            lanes = slice(kvh * gw, (kvh + 1) * gw)
            v_h = v_cat[:, kvh * HEAD_DIM:(kvh + 1) * HEAD_DIM]
            o_t = lax.dot_general(v_h, p_cat[:, lanes], CONTRACT_FIRST,
                                  preferred_element_type=F32)
            o_t = (o_t * inv_denom[:, lanes]).astype(BF16)
            for g in range(group):
                h = kvh * group + g
                ot_ref[h * HEAD_DIM:(h + 1) * HEAD_DIM, rows] = o_t[:, g * WINDOW:(g + 1) * WINDOW]
    o_ref[0] = x_ref[0] + lax.dot_general(ot_ref[...], wo_ref[...], CONTRACT_FIRST,
                                          preferred_element_type=F32)


def _alibi_band_bias(n_heads):
    slopes = jnp.exp2(-8.0 * jnp.arange(1, n_heads + 1, dtype=F32) / n_heads)
    c = jnp.arange(WINDOW)[:, None, None]
    r = jnp.arange(WINDOW)[None, None, :]
    dist = ((r - c) % WINDOW).astype(F32)
    return (-slopes[None, :, None] * dist).reshape(WINDOW, n_heads * WINDOW)


def _attention(x, qkv, sinks, w_o):
    b, s, d = x.shape
    n_heads = w_o.shape[0] // HEAD_DIM
    q_width = n_heads * HEAD_DIM
    kv_width = N_KV_HEADS * HEAD_DIM
    k_col = q_width // kv_width
    v_col = k_col + 1
    blocks_per_step = ATTN_ROWS // WINDOW
    sink_row = jnp.repeat(sinks, WINDOW)[None, :]

    def prev_block(bi, j):
        return jnp.maximum(j * blocks_per_step - 1, 0)

    kernel = functools.partial(_attn_kernel, n_heads=n_heads)
    return pl.pallas_call(
        kernel,
        grid=(b, s // ATTN_ROWS),
        in_specs=[
            pl.BlockSpec((WINDOW, n_heads * WINDOW), lambda bi, j: (0, 0)),
            pl.BlockSpec((1, n_heads * WINDOW), lambda bi, j: (0, 0)),
            pl.BlockSpec((1, ATTN_ROWS, q_width), lambda bi, j: (bi, j, 0)),
            pl.BlockSpec((1, ATTN_ROWS, kv_width), lambda bi, j: (bi, j, k_col)),
            pl.BlockSpec((1, WINDOW, kv_width), lambda bi, j: (bi, prev_block(bi, j), k_col)),
            pl.BlockSpec((1, ATTN_ROWS, kv_width), lambda bi, j: (bi, j, v_col)),
            pl.BlockSpec((1, WINDOW, kv_width), lambda bi, j: (bi, prev_block(bi, j), v_col)),
            pl.BlockSpec((1, ATTN_ROWS, d), lambda bi, j: (bi, j, 0)),
            pl.BlockSpec((q_width, d), lambda bi, j: (0, 0)),
        ],
        out_specs=pl.BlockSpec((1, ATTN_ROWS, d), lambda bi, j: (bi, j, 0)),
        out_shape=jax.ShapeDtypeStruct((b, s, d), F32),
        scratch_shapes=[pltpu.VMEM((q_width, ATTN_ROWS), BF16)],
        compiler_params=_params("parallel", "parallel"),
        name="swa_attention",
    )(_alibi_band_bias(n_heads), sink_row, qkv, qkv, qkv, qkv, qkv, x, w_o)


def _ffn_pool_route_kernel(x_ref, fg_ref, wg_ref, wu_ref, wd_ref, pn_ref, ps_ref, mn_ref, wp_ref, wrt_ref,
                           x3_ref, h3_ref, route_ref, routet_ref, info_ref,
                           act_ref, x2_ref, halo_ref, running_ref, *, n_experts, tiles_per_seq):
    i = pl.program_id(0)
    rows = x_ref.shape[0]

    @pl.when(i == 0)
    def _():
        x2_ref[...] = jnp.zeros_like(x2_ref)
        halo_ref[...] = jnp.zeros_like(halo_ref)
        running_ref[...] = jnp.zeros_like(running_ref)

    x = x_ref[...]
    hf = _rms_norm(x, fg_ref[...]).astype(BF16)
    pending = iter(range(wg_ref.shape[1] // V7X_MXU_DIM))

    def swiglu_chunks(n):
        for c in itertools.islice(pending, n):
            cols = slice(c * V7X_MXU_DIM, (c + 1) * V7X_MXU_DIM)
            gate = jnp.dot(hf, wg_ref[:, cols], preferred_element_type=F32)
            up = jnp.dot(hf, wu_ref[:, cols], preferred_element_type=F32)
            act_ref[:, cols] = (_silu(gate) * up).astype(BF16)

    live = i > 0
    j = jnp.maximum(i - 1, 0) % tiles_per_seq
    xp = x2_ref[(i + 1) % 2]
    h = _rms_norm(xp, pn_ref[...])
    halo = jnp.where(j > 0, halo_ref[...], 0.0)
    halo_ref[...] = h[rows - POOL_HALO:, :]
    y = jnp.concatenate([halo, h], axis=0)
    pos = j * rows + lax.broadcasted_iota(jnp.int32, (rows, 1), 0)
    gdim = wp_ref.shape[1]
    mixed = []
    swiglu_chunks(2)
    for g, w in enumerate(POOL_WINDOWS):
        swiglu_chunks(2)
        cols = slice(g * gdim, (g + 1) * gdim)
        acc = y[:, cols]
        span = 1
        while span < w:
            acc = acc + pltpu.roll(acc, span, axis=0)
            span *= 2
        count = jnp.minimum(pos + 1, w).astype(F32)
        p = acc[POOL_HALO:, :] / count - h[:, cols]
        mixed.append(jnp.dot(p.astype(BF16), wp_ref[g], preferred_element_type=F32))
    swiglu_chunks(wg_ref.shape[1] // V7X_MXU_DIM)
    x3 = xp + jnp.concatenate(mixed, axis=-1) * ps_ref[...]
    x3_ref[...] = x3
    x2_ref[i % 2] = x + jnp.dot(act_ref[...], wd_ref[...], preferred_element_type=F32)

    h_hi, h_mid, h_lo = _split_bf16(_rms_norm(x3, mn_ref[...]), 3)
    h3_ref[...] = h_hi
    w_hi, w_mid, w_lo = _split_bf16(wrt_ref[...], 3)
    es = ROUTE_SUBLANES
    a = lax.dot_general(jnp.concatenate([w_hi, w_mid, w_lo], axis=0), h_hi, CONTRACT_LAST,
                        preferred_element_type=F32)
    b = lax.dot_general(jnp.concatenate([w_hi, w_mid], axis=0), h_mid, CONTRACT_LAST,
                        preferred_element_type=F32)
    c = lax.dot_general(w_hi, h_lo, CONTRACT_LAST, preferred_element_type=F32)
    logits = ((c + b[es:] + a[2 * es:]) + (b[:es] + a[es:2 * es])) + a[:es]

    eidx = lax.broadcasted_iota(jnp.int32, (es, rows), 0).astype(F32)
    logits = jnp.where(eidx < n_experts, logits, -jnp.inf)
    v1 = jnp.max(logits, axis=0, keepdims=True)
    e1 = jnp.min(jnp.where(logits == v1, eidx, float(es)), axis=0, keepdims=True)
    rest = jnp.where(eidx == e1, -jnp.inf, logits)
    v2 = jnp.max(rest, axis=0, keepdims=True)
    e2 = jnp.min(jnp.where(rest == v2, eidx, float(es)), axis=0, keepdims=True)
    ex = jnp.exp(v2 - v1)
    g1 = 1.0 / (1.0 + ex)
    g2 = ex / (1.0 + ex)

    chosen = ((eidx == e1) | (eidx == e2)).astype(F32)
    t_i = lax.broadcasted_iota(jnp.int32, (rows, rows), 0)
    t_j = lax.broadcasted_iota(jnp.int32, (rows, rows), 1)
    earlier = (t_i < t_j).astype(BF16)
    local_rank = jnp.dot(chosen.astype(BF16), earlier, preferred_element_type=F32)
    count = jnp.sum(chosen, axis=1, keepdims=True)
    padded = jnp.floor((count + (RUN_ALIGN - 1)) * (1.0 / RUN_ALIGN)) * RUN_ALIGN
    ecol = lax.broadcasted_iota(jnp.int32, (es, 1), 0)
    lstart = jnp.zeros((es, 1), F32)
    for e in range(n_experts):
        lstart = lstart + jnp.where(ecol > e, padded[e:e + 1, :], 0.0)
    lpos = lstart + local_rank
    l1 = jnp.sum(jnp.where(eidx == e1, lpos, 0.0), axis=0, keepdims=True)
    l2 = jnp.sum(jnp.where(eidx == e2, lpos, 0.0), axis=0, keepdims=True)

    base = running_ref[...]
    running_ref[...] = base + jnp.where(live, padded, 0.0)
    lane = lax.broadcasted_iota(jnp.int32, (es, V7X_LANES), 1)
    info_ref[0] = jnp.where(lane == INFO_BASE, base,
                            jnp.where(lane == INFO_COUNT, padded,
                                      jnp.where(lane == INFO_LSTART, lstart, 0.0)))

    rec = jnp.zeros((es, rows), F32)
    for idx, val in ((ROUTE_L1, l1), (ROUTE_L2, l2), (ROUTE_G1, g1), (ROUTE_G2, g2),
                     (ROUTE_E1, e1), (ROUTE_E2, e2)):
        rec = jnp.where(eidx == idx, val, rec)
    routet_ref[0] = rec
    rec_full = jnp.concatenate([rec, jnp.zeros((V7X_LANES - es, rows), F32)], axis=0)
    route_ref[...] = rec_full.T


def _ffn_pool_route(x2d, seq_len, ffn_gain, w_gate, w_up, w_down, pool_norm, pool_scale, moe_norm,
                    w_pool, w_router_t, n_experts):
    t, d = x2d.shape
    d_ff = w_gate.shape[1]
    rows = ROUTE_ROWS
    n_tiles = t // rows
    es = ROUTE_SUBLANES
    resident = dict(pipeline_mode=pl.Buffered(1))
    cur = lambda i: (jnp.minimum(i, n_tiles - 1), 0)
    prev = lambda i: (jnp.maximum(i - 1, 0), 0)
    prev3 = lambda i: (jnp.maximum(i - 1, 0), 0, 0)
    vec = pl.BlockSpec((1, d), lambda i: (0, 0))
    kernel = functools.partial(_ffn_pool_route_kernel, n_experts=n_experts,
                               tiles_per_seq=seq_len // rows)
    return pl.pallas_call(
        kernel,
        grid=(n_tiles + 1,),
        in_specs=[pl.BlockSpec((rows, d), cur),
                  vec,
                  pl.BlockSpec((d, d_ff), lambda i: (0, 0), **resident),
                  pl.BlockSpec((d, d_ff), lambda i: (0, 0), **resident),
                  pl.BlockSpec((d_ff, d), lambda i: (0, 0), **resident),
                  vec, vec, vec,
                  pl.BlockSpec(w_pool.shape, lambda i: (0, 0, 0)),
                  pl.BlockSpec(w_router_t.shape, lambda i: (0, 0))],
        out_specs=[pl.BlockSpec((rows, d), prev),
                   pl.BlockSpec((rows, d), prev),
                   pl.BlockSpec((rows, V7X_LANES), prev),
                   pl.BlockSpec((1, es, rows), prev3),
                   pl.BlockSpec((1, es, V7X_LANES), prev3)],
        out_shape=[jax.ShapeDtypeStruct((t, d), F32),
                   jax.ShapeDtypeStruct((t, d), BF16),
                   jax.ShapeDtypeStruct((t, V7X_LANES), F32),
                   jax.ShapeDtypeStruct((n_tiles, es, rows), F32),
                   jax.ShapeDtypeStruct((n_tiles, es, V7X_LANES), F32)],
        scratch_shapes=[pltpu.VMEM((rows, d_ff), BF16),
                        pltpu.VMEM((2, rows, d), F32),
                        pltpu.VMEM((POOL_HALO, d), F32),
                        pltpu.VMEM((es, V7X_LANES), F32)],
        compiler_params=_params("arbitrary"),
        name="swiglu_pool_route",
    )(x2d, ffn_gain, w_gate, w_up, w_down, pool_norm, pool_scale, moe_norm, w_pool, w_router_t)


def _chunks(count, lo_bit, hi_bit):
    for bit in range(lo_bit, hi_bit):
        size = 1 << bit
        offset = (count >> (bit + 1)) << (bit + 1)
        yield offset, size, (count & size) != 0


def _for_each_run_piece(tab_ref, n_experts, max_rows, fn):
    for e in range(n_experts):
        local = tab_ref[0, 0, e]
        sorted_start = tab_ref[0, 0, n_experts + e]
        count = tab_ref[0, 0, 2 * n_experts + e]
        for offset, size, present in _chunks(count, RUN_ALIGN_BITS, max_rows.bit_length()):
            @pl.when(present)
            def _(offset=offset, size=size, local=local, sorted_start=sorted_start):
                fn(pl.multiple_of(local + offset, RUN_ALIGN),
                   pl.multiple_of(sorted_start + offset, RUN_ALIGN), size)


def _wait_tile_runs(tab_ref, n_experts, max_rows, make_copy):
    total = tab_ref[0, 0, TABLE_FIELDS * n_experts]
    for _, size, present in _chunks(total, RUN_ALIGN_BITS, max_rows.bit_length()):
        @pl.when(present)
        def _(size=size):
            make_copy(0, 0, size).wait()


def _slot_rows(n_experts):
    pad = n_experts * (RUN_ALIGN - 1)
    return TOP_K * ROUTE_ROWS + (-(-pad // ROUTE_SUBLANES)) * ROUTE_SUBLANES


ZERO_ROWS = MOE_ROWS // 2


def _dispatch_kernel(tab_ref, prev_tab_ref, h_ref, rt_ref, xs_ref, buf_ref, zero_ref, sem, fill_sem,
                     *, n_experts, max_tail):
    i, n = pl.program_id(0), pl.num_programs(0)
    rows, d = h_ref.shape
    slots = buf_ref.shape[1]
    cur = i % 2

    rt = rt_ref[0]
    l1 = rt[ROUTE_L1:ROUTE_L1 + 1].astype(jnp.int32)
    l2 = rt[ROUTE_L2:ROUTE_L2 + 1].astype(jnp.int32)
    g1 = rt[ROUTE_G1:ROUTE_G1 + 1]
    g2 = rt[ROUTE_G2:ROUTE_G2 + 1]
    pos = lax.broadcasted_iota(jnp.int32, (slots, rows), 0)
    first = pos == l1
    second = pos == l2
    perm = (first | second).astype(BF16)
    x_sorted = jnp.dot(perm, h_ref[...], preferred_element_type=F32)
    gate = jnp.sum(jnp.where(first, g1, 0.0) + jnp.where(second, g2, 0.0), axis=1, keepdims=True)
    lane = lax.broadcasted_iota(jnp.int32, (slots, V7X_LANES), 1)
    gate_lanes = jnp.zeros((slots, V7X_LANES), F32)
    for idx, piece in enumerate(_split_bf16(gate, GATE_PIECES)):
        gate_lanes = jnp.where(lane == idx, piece.astype(F32), gate_lanes)
    buf_ref[cur] = jnp.concatenate([x_sorted, gate_lanes], axis=1).astype(BF16)

    def run_copy(which):
        def make(local, sorted_start, size):
            return pltpu.make_async_copy(buf_ref.at[which, pl.ds(local, size)],
                                         xs_ref.at[pl.ds(sorted_start, size)], sem.at[which])
        return make

    _for_each_run_piece(tab_ref, n_experts, rows,
                        lambda a, b, size: run_copy(cur)(a, b, size).start())

    @pl.when(i > 0)
    def _():
        _wait_tile_runs(prev_tab_ref, n_experts, slots, run_copy(1 - cur))

    @pl.when(i == n - 1)
    def _():
        _wait_tile_runs(tab_ref, n_experts, slots, run_copy(cur))
        zero_ref[...] = jnp.zeros_like(zero_ref)

        def fill_pieces(fn):
            for e in range(n_experts):
                start = tab_ref[0, 0, 3 * n_experts + e]
                length = tab_ref[0, 0, 4 * n_experts + e]
                for offset, size, present in _chunks(length, RUN_ALIGN_BITS, ZERO_ROWS.bit_length()):
                    @pl.when(present)
                    def _(offset=offset, size=size, start=start):
                        fn(pltpu.make_async_copy(
                            zero_ref.at[pl.ds(0, size)],
                            xs_ref.at[pl.ds(pl.multiple_of(start + offset, RUN_ALIGN), size)],
                            fill_sem))
            used_rows = tab_ref[0, 0, TABLE_FIELDS * n_experts + 1]
            for k in range(0, max_tail, ZERO_ROWS):
                @pl.when(used_rows + k < xs_ref.shape[0])
                def _(k=k):
                    fn(pltpu.make_async_copy(
                        zero_ref,
                        xs_ref.at[pl.ds(pl.multiple_of(used_rows + k, RUN_ALIGN), ZERO_ROWS)],
                        fill_sem))

        fill_pieces(lambda c: c.start())
        fill_pieces(lambda c: c.wait())


def _dispatch(h2d, route_t, table, n_rows, n_experts):
    t, d = h2d.shape
    n_tiles = t // ROUTE_ROWS
    slots = _slot_rows(n_experts)
    width = d + V7X_LANES
    tab_block = (1, 1, TABLE_WIDTH)
    kernel = functools.partial(_dispatch_kernel, n_experts=n_experts, max_tail=n_rows - TOP_K * t)
    return pl.pallas_call(
        kernel,
        grid=(n_tiles,),
        in_specs=[pl.BlockSpec(tab_block, lambda i: (i, 0, 0), memory_space=pltpu.SMEM),
                  pl.BlockSpec(tab_block, lambda i: (jnp.maximum(i - 1, 0), 0, 0),
                               memory_space=pltpu.SMEM),
                  pl.BlockSpec((ROUTE_ROWS, d), lambda i: (i, 0)),
                  pl.BlockSpec((1, ROUTE_SUBLANES, ROUTE_ROWS), lambda i: (i, 0, 0))],
        out_specs=pl.BlockSpec(memory_space=pl.ANY),
        out_shape=jax.ShapeDtypeStruct((n_rows, width), BF16),
        scratch_shapes=[pltpu.VMEM((2, slots, width), BF16),
                        pltpu.VMEM((ZERO_ROWS, width), BF16),
                        pltpu.SemaphoreType.DMA((2,)),
                        pltpu.SemaphoreType.DMA(())],
        compiler_params=_params("arbitrary"),
        name="moe_dispatch",
    )(table, table, h2d, route_t)


WEIGHT_CHUNKS = 16
CHUNKS_PER_TILE = 2


def _moe_kernel(block_ref, expert_ref, used_ref, first_ref, slot_ref, step_ref, next_ref, ready_ref,
                x_ref, wg_hbm, wu_hbm, wd_hbm, o_ref,
                wg_buf, wu_buf, wd_buf, stage_g, stage_u, stage_d, act_ref, sem):
    v = pl.program_id(0)
    d = o_ref.shape[1]
    n_chunks = WEIGHT_CHUNKS
    cw = V7X_MXU_DIM
    in_rows = stage_g.shape[1]
    out_rows = stage_d.shape[1]

    def blocks(j):
        return (pl.ds(pl.multiple_of(j * in_rows, in_rows), in_rows),
                pl.ds(pl.multiple_of(j * out_rows, out_rows), out_rows))

    def chunk_copies(e, j, k):
        rin, rout = blocks(j)
        return (pltpu.make_async_copy(wg_hbm.at[e, rin, :], stage_g.at[k], sem.at[k, 0]),
                pltpu.make_async_copy(wu_hbm.at[e, rin, :], stage_u.at[k], sem.at[k, 1]),
                pltpu.make_async_copy(wd_hbm.at[e, rout, :], stage_d.at[k], sem.at[k, 2]))

    def start(e, j, k):
        for copy in chunk_copies(e, j, k):
            copy.start()

    def finish(e, j, k, s):
        for copy in chunk_copies(e, j, k):
            copy.wait()
        rin, rout = blocks(j)
        wg_buf[s, rin, :] = stage_g[k].astype(BF16)
        wu_buf[s, rin, :] = stage_u[k].astype(BF16)
        wd_buf[s, rout, :] = stage_d[k].astype(BF16)

    @pl.when(used_ref[v] == 0)
    def _():
        o_ref[...] = jnp.zeros_like(o_ref)

    @pl.when(used_ref[v] != 0)
    def _():
        e, s, nxt = expert_ref[v], slot_ref[v], next_ref[v]

        @pl.when(first_ref[v] != 0)
        def _():
            def fetch(j, carry):
                start(e, j, 0)
                finish(e, j, 0, s)
                return carry
            lax.fori_loop(ready_ref[v], n_chunks, fetch, 0)

        ahead = [jnp.minimum(CHUNKS_PER_TILE * step_ref[v] + k, n_chunks - 1)
                 for k in range(CHUNKS_PER_TILE)]
        xb = x_ref[:, :d]

        def hidden_chunk(c):
            cols = slice(c * cw, (c + 1) * cw)
            gate = jnp.dot(xb, wg_buf[s, :, cols], preferred_element_type=F32)
            up = jnp.dot(xb, wu_buf[s, :, cols], preferred_element_type=F32)
            act_ref[:, cols] = (_silu(gate) * up).astype(BF16)

        n_cols = act_ref.shape[1] // cw
        bounds = [n_cols * k // CHUNKS_PER_TILE for k in range(CHUNKS_PER_TILE + 1)]
        for k in range(CHUNKS_PER_TILE):
            start(nxt, ahead[k], k)
        for k in range(CHUNKS_PER_TILE):
            for c in range(bounds[k], bounds[k + 1]):
                hidden_chunk(c)
            if k + 1 < CHUNKS_PER_TILE:
                finish(nxt, ahead[k], k, 1 - s)
        y = jnp.dot(act_ref[...], wd_buf[s], preferred_element_type=F32)
        pieces = x_ref[:, d:].astype(F32)
        weight = pieces[:, 0:1]
        for idx in range(1, GATE_PIECES):
            weight = weight + pieces[:, idx:idx + 1]
        o_ref[...] = (y * weight).astype(BF16)
        finish(nxt, ahead[-1], CHUNKS_PER_TILE - 1, 1 - s)


def _moe_experts(xs, plan, w_gate, w_up, w_down):
    n_rows, width = xs.shape
    n_experts, d, d_ff = w_gate.shape
    n_tiles = n_rows // MOE_ROWS
    in_rows, out_rows = d // WEIGHT_CHUNKS, d_ff // WEIGHT_CHUNKS
    assert in_rows % V7X_BF16_TILE_ROWS == 0 and out_rows % V7X_BF16_TILE_ROWS == 0
    grid_spec = pltpu.PrefetchScalarGridSpec(
        num_scalar_prefetch=len(plan),
        grid=(n_tiles,),
        in_specs=[
            pl.BlockSpec((MOE_ROWS, width), lambda v, blk, *_: (blk[v], 0)),
            pl.BlockSpec(memory_space=pl.ANY),
            pl.BlockSpec(memory_space=pl.ANY),
            pl.BlockSpec(memory_space=pl.ANY),
        ],
        out_specs=pl.BlockSpec((MOE_ROWS, d), lambda v, *_: (v, 0)),
        scratch_shapes=[pltpu.VMEM((2, d, d_ff), BF16),
                        pltpu.VMEM((2, d, d_ff), BF16),
                        pltpu.VMEM((2, d_ff, d), BF16),
                        pltpu.VMEM((CHUNKS_PER_TILE, in_rows, d_ff), F32),
                        pltpu.VMEM((CHUNKS_PER_TILE, in_rows, d_ff), F32),
                        pltpu.VMEM((CHUNKS_PER_TILE, out_rows, d), F32),
                        pltpu.VMEM((MOE_ROWS, d_ff), BF16),
                        pltpu.SemaphoreType.DMA((CHUNKS_PER_TILE, 3))],
    )
    return pl.pallas_call(
        _moe_kernel,
        grid_spec=grid_spec,
        out_shape=jax.ShapeDtypeStruct((n_rows, d), BF16),
        compiler_params=_params("arbitrary"),
        name="moe_experts",
    )(*plan, xs, w_gate, w_up, w_down)


def _plan_tiles(group_tiles, n_tiles, n_chunks):
    n_experts = group_tiles.shape[0]
    ends = jnp.cumsum(group_tiles)
    used_tiles = ends[-1]
    tile = jnp.arange(n_tiles, dtype=jnp.int32)
    used = tile < used_tiles
    block = jnp.minimum(tile, used_tiles - 1)
    expert = jnp.sum(ends[None, :] <= block[:, None], axis=1).astype(jnp.int32)
    expert = jnp.minimum(expert, n_experts - 1)

    idx = jnp.arange(n_experts, dtype=jnp.int32)
    present = group_tiles > 0
    later = present[None, :] & (idx[None, :] > idx[:, None])
    next_expert = jnp.min(jnp.where(later, idx[None, :], n_experts), axis=1)
    next_expert = jnp.where(next_expert == n_experts, idx, next_expert)
    earlier = present[None, :] & (idx[None, :] < idx[:, None])
    prev_expert = jnp.max(jnp.where(earlier, idx[None, :], -1), axis=1)
    ready = jnp.where(prev_expert >= 0,
                      jnp.minimum(CHUNKS_PER_TILE * group_tiles[jnp.maximum(prev_expert, 0)], n_chunks),
                      0)
    run_slot = jnp.sum(earlier, axis=1) % 2
    step = block - (ends - group_tiles)[expert]
    first = used & (step == 0)
    cast = lambda a: a.astype(jnp.int32)
    return (cast(block), expert, cast(used), cast(first), cast(run_slot[expert]), cast(step),
            cast(next_expert[expert]), cast(ready[expert]))


def _combine_kernel(tab_ref, next_tab_ref, x_ref, route_ref, g_ref, ys_ref, o_ref, buf_ref, sem,
                    *, n_experts):
    i, n = pl.program_id(0), pl.num_programs(0)
    rows = x_ref.shape[0]
    slots = buf_ref.shape[1]
    cur = i % 2

    def run_copy(which):
        def make(local, sorted_start, size):
            return pltpu.make_async_copy(ys_ref.at[pl.ds(sorted_start, size)],
                                         buf_ref.at[which, pl.ds(local, size)], sem.at[which])
        return make

    @pl.when(i == 0)
    def _():
        buf_ref[...] = jnp.zeros_like(buf_ref)
        _for_each_run_piece(tab_ref, n_experts, rows,
                            lambda a, b, size: run_copy(0)(a, b, size).start())

    @pl.when(i + 1 < n)
    def _():
        _for_each_run_piece(next_tab_ref, n_experts, rows,
                            lambda a, b, size: run_copy(1 - cur)(a, b, size).start())

    _wait_tile_runs(tab_ref, n_experts, slots, run_copy(cur))

    route = route_ref[...]
    l1 = route[:, ROUTE_L1:ROUTE_L1 + 1].astype(jnp.int32)
    l2 = route[:, ROUTE_L2:ROUTE_L2 + 1].astype(jnp.int32)
    pos = lax.broadcasted_iota(jnp.int32, (rows, slots), 1)
    pick = ((pos == l1) | (pos == l2)).astype(BF16)
    y = x_ref[...] + jnp.dot(pick, buf_ref[cur], preferred_element_type=F32)
    o_ref[...] = _rms_norm(y, g_ref[...])


def _combine(x2d, route2d, table, ys, final_norm, n_experts):
    t, d = x2d.shape
    n_tiles = t // ROUTE_ROWS
    slots = _slot_rows(n_experts)
    tab_block = (1, 1, TABLE_WIDTH)
    kernel = functools.partial(_combine_kernel, n_experts=n_experts)
    return pl.pallas_call(
        kernel,
        grid=(n_tiles,),
        in_specs=[pl.BlockSpec(tab_block, lambda i: (i, 0, 0), memory_space=pltpu.SMEM),
                  pl.BlockSpec(tab_block, lambda i: (jnp.minimum(i + 1, n_tiles - 1), 0, 0),
                               memory_space=pltpu.SMEM),
                  pl.BlockSpec((ROUTE_ROWS, d), lambda i: (i, 0)),
                  pl.BlockSpec((ROUTE_ROWS, V7X_LANES), lambda i: (i, 0)),
                  pl.BlockSpec((1, d), lambda i: (0, 0)),
                  pl.BlockSpec(memory_space=pl.ANY)],
        out_specs=pl.BlockSpec((ROUTE_ROWS, d), lambda i: (i, 0)),
        out_shape=jax.ShapeDtypeStruct((t, d), F32),
        scratch_shapes=[pltpu.VMEM((2, slots, d), BF16),
                        pltpu.SemaphoreType.DMA((2,))],
        compiler_params=_params("arbitrary"),
        name="moe_combine",
    )(table, table, x2d, route2d, final_norm, ys)


def kernel(x, attn_norm, attn_w_qkv, attn_sinks, attn_w_o, ffn_norm, ffn_w_gate, ffn_w_up,
           ffn_w_down, pool_norm, pool_w, pool_scale, moe_norm, moe_w_router, moe_w_gate, moe_w_up,
           moe_w_down, final_norm):
    b, s, d = x.shape
    t = b * s
    n_experts = moe_w_router.shape[-1]
    assert s % ATTN_ROWS == 0 and s % ROUTE_ROWS == 0
    assert n_experts <= V7X_SUBLANES and TABLE_FIELDS * n_experts + 2 <= TABLE_WIDTH
    assert attn_norm.shape[0] == 1 and pool_norm.shape[0] == 1

    qkv = _qkv_proj(x.reshape(t, d), attn_norm[0][None], attn_w_qkv[0].astype(BF16))
    x1 = _attention(x, qkv.reshape(b, s, -1), attn_sinks[0].astype(F32), attn_w_o[0].astype(BF16))
    w_router_t = jnp.pad(moe_w_router[0].T, ((0, ROUTE_SUBLANES - n_experts), (0, 0)))
    x3, h3, route, route_t, info = _ffn_pool_route(
        x1.reshape(t, d), s, ffn_norm[0][None], ffn_w_gate[0].astype(BF16), ffn_w_up[0].astype(BF16),
        ffn_w_down[0].astype(BF16), pool_norm[0][None], pool_scale[0][None], moe_norm[0][None],
        pool_w[0].astype(BF16), w_router_t, n_experts)

    n_tiles = t // ROUTE_ROWS
    info = info[:, :n_experts, :].astype(jnp.int32)
    base, padded, lstart = info[..., INFO_BASE], info[..., INFO_COUNT], info[..., INFO_LSTART]
    group_rows = jnp.sum(padded, axis=0)
    group_tiles = -(-group_rows // MOE_ROWS)
    group_start = (jnp.cumsum(group_tiles) - group_tiles) * MOE_ROWS
    max_rows = (t * TOP_K + n_tiles * n_experts * (RUN_ALIGN - 1)
                + n_experts * (MOE_ROWS - RUN_ALIGN))
    n_rows = -(-max_rows // MOE_ROWS) * MOE_ROWS
    per_tile = lambda v: jnp.broadcast_to(v[None, :], (n_tiles, n_experts))
    table = jnp.concatenate(
        [lstart, group_start[None, :] + base, padded,
         per_tile(group_start + group_rows), per_tile(group_tiles * MOE_ROWS - group_rows),
         jnp.sum(padded, axis=1, keepdims=True),
         jnp.broadcast_to(jnp.sum(group_tiles) * MOE_ROWS, (n_tiles, 1)),
         jnp.zeros((n_tiles, TABLE_WIDTH - TABLE_FIELDS * n_experts - 2), jnp.int32)], axis=1)
    table = table.reshape(n_tiles, 1, TABLE_WIDTH)

    xs = _dispatch(h3, route_t, table, n_rows, n_experts)
    plan = _plan_tiles(group_tiles, n_rows // MOE_ROWS, WEIGHT_CHUNKS)
    ys = _moe_experts(xs, plan, moe_w_gate[0], moe_w_up[0], moe_w_down[0])
    out = _combine(x3, route, table, ys, final_norm[None],
                   n_experts)
    return out.reshape(b, s, d)
```

```python
import functools
import itertools

import jax
import jax.numpy as jnp
from jax import lax
from jax.experimental import pallas as pl
from jax.experimental.pallas import tpu as pltpu

F32 = jnp.float32
BF16 = jnp.bfloat16

HEAD_DIM = 64
N_KV_HEADS = 4
WINDOW = 128
POOL_WINDOWS = (2, 4, 8, 16)
TOP_K = 2
RMS_EPS = 1e-5
NEG_INF = -1e30

V7X_LANES = 128
V7X_SUBLANES = 8
V7X_BF16_TILE_ROWS = 16
V7X_MXU_DIM = 256
V7X_VMEM_LIMIT_BYTES = 60 * 1024 * 1024

QKV_ROWS = 1024
ATTN_ROWS = 1024
ROUTE_ROWS = 512
MOE_ROWS = 512
POOL_HALO = 16

RUN_ALIGN = V7X_BF16_TILE_ROWS
RUN_ALIGN_BITS = RUN_ALIGN.bit_length() - 1
GATE_PIECES = 3
ROUTE_SUBLANES = 16
ROUTE_L1, ROUTE_L2, ROUTE_G1, ROUTE_G2, ROUTE_E1, ROUTE_E2 = range(6)
INFO_BASE, INFO_COUNT, INFO_LSTART = range(3)
TABLE_FIELDS = 5
TABLE_WIDTH = 64


def _params(*semantics):
    return pltpu.CompilerParams(dimension_semantics=semantics,
                                vmem_limit_bytes=V7X_VMEM_LIMIT_BYTES)


def _rms_norm(x, gain):
    ms = jnp.mean(x * x, axis=-1, keepdims=True)
    return x * lax.rsqrt(ms + RMS_EPS) * gain


def _silu(x):
    return x / (1.0 + jnp.exp(-x))


def _split_bf16(x, pieces):
    out = []
    for _ in range(pieces - 1):
        top = x.astype(BF16)
        out.append(top)
        x = x - top.astype(F32)
    out.append(x.astype(BF16))
    return out


CONTRACT_LAST = (((1,), (1,)), ((), ()))
CONTRACT_FIRST = (((0,), (0,)), ((), ()))


def _qkv_kernel(x_ref, g_ref, w_ref, o_ref):
    h = _rms_norm(x_ref[...], g_ref[...]).astype(BF16)
    o_ref[...] = jnp.dot(h, w_ref[...], preferred_element_type=F32).astype(BF16)


def _qkv_proj(x2d, gain, w_qkv):
    t, d = x2d.shape
    n = w_qkv.shape[1]
    return pl.pallas_call(
        _qkv_kernel,
        grid=(t // QKV_ROWS,),
        in_specs=[pl.BlockSpec((QKV_ROWS, d), lambda i: (i, 0)),
                  pl.BlockSpec((1, d), lambda i: (0, 0)),
                  pl.BlockSpec((d, n), lambda i: (0, 0))],
        out_specs=pl.BlockSpec((QKV_ROWS, n), lambda i: (i, 0)),
        out_shape=jax.ShapeDtypeStruct((t, n), BF16),
        compiler_params=_params("parallel"),
        name="rms_qkv",
    )(x2d, gain, w_qkv)


def _attn_kernel(bias_ref, sink_ref, q_ref, kc_ref, kp_ref, vc_ref, vp_ref, x_ref, wo_ref, o_ref,
                 ot_ref, *, n_heads):
    group = n_heads // N_KV_HEADS
    gw = group * WINDOW
    j = pl.program_id(1)
    key = lax.broadcasted_iota(jnp.int32, (WINDOW, n_heads * WINDOW), 0)
    query = lax.broadcasted_iota(jnp.int32, (WINDOW, n_heads * WINDOW), 1) & (WINDOW - 1)
    from_prev = key > query
    from_prev_bf = from_prev.astype(BF16)
    no_prev = from_prev & (j == 0)
    sink = sink_ref[...]

    n_blocks = q_ref.shape[1] // WINDOW

    def score_stage(blk):
        rows = slice(blk * WINDOW, (blk + 1) * WINDOW)
        q_blk = q_ref[0, rows, :] * jnp.asarray(HEAD_DIM ** -0.5, BF16)
        if blk == 0:
            k_prev, v_prev = kp_ref[0], vp_ref[0]
        else:
            prev = slice((blk - 1) * WINDOW, blk * WINDOW)
            k_prev, v_prev = kc_ref[0, prev, :], vc_ref[0, prev, :]
        k_cat = jnp.concatenate([k_prev, kc_ref[0, rows, :]], axis=0)
        v_cat = jnp.concatenate([v_prev, vc_ref[0, rows, :]], axis=0)
        scores = []
        for kvh in range(N_KV_HEADS):
            q_g = jnp.concatenate([q_blk[:, h * HEAD_DIM:(h + 1) * HEAD_DIM]
                                   for h in range(kvh * group, (kvh + 1) * group)], axis=0)
            k_h = k_cat[:, kvh * HEAD_DIM:(kvh + 1) * HEAD_DIM]
            scores.append(lax.dot_general(k_h, q_g, CONTRACT_LAST, preferred_element_type=F32))
        return jnp.concatenate(scores, axis=1), v_cat

    def output_stage(blk, s_all, v_cat):
        rows = slice(blk * WINDOW, (blk + 1) * WINDOW)
        s = jnp.where(from_prev, s_all[:WINDOW], s_all[WINDOW:]) + bias_ref[...]
        if blk == 0:
            s = jnp.where(no_prev, NEG_INF, s)
        m = jnp.maximum(jnp.max(s, axis=0, keepdims=True), sink)
        e = jnp.exp(s - m)
        inv_denom = 1.0 / (jnp.sum(e, axis=0, keepdims=True) + jnp.exp(sink - m))
        e_bf = e.astype(BF16)
        p_prev = e_bf * from_prev_bf
        p_cat = jnp.concatenate([p_prev, e_bf - p_prev], axis=0)
        for kvh in range(N_KV_HEADS):
            lanes = slice(kvh * gw, (kvh + 1) * gw)
            v_h = v_cat[:, kvh * HEAD_DIM:(kvh + 1) * HEAD_DIM]
            o_t = lax.dot_general(v_h, p_cat[:, lanes], CONTRACT_FIRST,
                                  preferred_element_type=F32)
            o_t = (o_t * inv_denom[:, lanes]).astype(BF16)
            for g in range(group):
                h = kvh * group + g
                ot_ref[h * HEAD_DIM:(h + 1) * HEAD_DIM, rows] = o_t[:, g * WINDOW:(g + 1) * WINDOW]

    def project(first_blk, last_blk):
        rows = slice(first_blk * WINDOW, (last_blk + 1) * WINDOW)
        o_ref[0, rows, :] = x_ref[0, rows, :] + lax.dot_general(
            ot_ref[:, rows], wo_ref[...], CONTRACT_FIRST, preferred_element_type=F32)

    staged = score_stage(0)
    for blk in range(n_blocks):
        ahead = score_stage(blk + 1) if blk + 1 < n_blocks else None
        if blk % 2 == 0 and blk >= 2:
            project(blk - 2, blk - 1)
        output_stage(blk, *staged)
        staged = ahead
    project(n_blocks - 2 + n_blocks % 2, n_blocks - 1)


def _alibi_band_bias(n_heads):
    slopes = jnp.exp2(-8.0 * jnp.arange(1, n_heads + 1, dtype=F32) / n_heads)
    c = jnp.arange(WINDOW)[:, None, None]
    r = jnp.arange(WINDOW)[None, None, :]
    dist = ((r - c) % WINDOW).astype(F32)
    return (-slopes[None, :, None] * dist).reshape(WINDOW, n_heads * WINDOW)


def _attention(x, qkv, sinks, w_o):
    b, s, d = x.shape
    n_heads = w_o.shape[0] // HEAD_DIM
    q_width = n_heads * HEAD_DIM
    kv_width = N_KV_HEADS * HEAD_DIM
    k_col = q_width // kv_width
    v_col = k_col + 1
    blocks_per_step = ATTN_ROWS // WINDOW
    sink_row = jnp.repeat(sinks, WINDOW)[None, :]

    def prev_block(bi, j):
        return jnp.maximum(j * blocks_per_step - 1, 0)

    kernel = functools.partial(_attn_kernel, n_heads=n_heads)
    return pl.pallas_call(
        kernel,
        grid=(b, s // ATTN_ROWS),
        in_specs=[
            pl.BlockSpec((WINDOW, n_heads * WINDOW), lambda bi, j: (0, 0)),
            pl.BlockSpec((1, n_heads * WINDOW), lambda bi, j: (0, 0)),
            pl.BlockSpec((1, ATTN_ROWS, q_width), lambda bi, j: (bi, j, 0)),
            pl.BlockSpec((1, ATTN_ROWS, kv_width), lambda bi, j: (bi, j, k_col)),
            pl.BlockSpec((1, WINDOW, kv_width), lambda bi, j: (bi, prev_block(bi, j), k_col)),
            pl.BlockSpec((1, ATTN_ROWS, kv_width), lambda bi, j: (bi, j, v_col)),
            pl.BlockSpec((1, WINDOW, kv_width), lambda bi, j: (bi, prev_block(bi, j), v_col)),
            pl.BlockSpec((1, ATTN_ROWS, d), lambda bi, j: (bi, j, 0)),
            pl.BlockSpec((q_width, d), lambda bi, j: (0, 0)),
        ],
        out_specs=pl.BlockSpec((1, ATTN_ROWS, d), lambda bi, j: (bi, j, 0)),
        out_shape=jax.ShapeDtypeStruct((b, s, d), F32),
        scratch_shapes=[pltpu.VMEM((q_width, ATTN_ROWS), BF16)],
        compiler_params=_params("parallel", "parallel"),
        name="swa_attention",
    )(_alibi_band_bias(n_heads), sink_row, qkv, qkv, qkv, qkv, qkv, x, w_o)


def _ffn_pool_route_kernel(x_ref, fg_ref, wg_ref, wu_ref, wd_ref, pn_ref, ps_ref, mn_ref, wp_ref, wrt_ref,
                           x3_ref, h3_ref, route_ref, routet_ref, info_ref,
                           act_ref, x2_ref, halo_ref, running_ref, *, n_experts, tiles_per_seq):
    i = pl.program_id(0)
    rows = x_ref.shape[0]

    @pl.when(i == 0)
    def _():
        x2_ref[...] = jnp.zeros_like(x2_ref)
        halo_ref[...] = jnp.zeros_like(halo_ref)
        running_ref[...] = jnp.zeros_like(running_ref)

    x = x_ref[...]
    hf = _rms_norm(x, fg_ref[...]).astype(BF16)
    pending = iter(range(wg_ref.shape[1] // V7X_MXU_DIM))

    def swiglu_chunks(n):
        for c in itertools.islice(pending, n):
            cols = slice(c * V7X_MXU_DIM, (c + 1) * V7X_MXU_DIM)
            gate = jnp.dot(hf, wg_ref[:, cols], preferred_element_type=F32)
            up = jnp.dot(hf, wu_ref[:, cols], preferred_element_type=F32)
            act_ref[:, cols] = (_silu(gate) * up).astype(BF16)

    live = i > 0
    j = jnp.maximum(i - 1, 0) % tiles_per_seq
    xp = x2_ref[(i + 1) % 2]
    h = _rms_norm(xp, pn_ref[...])
    halo = jnp.where(j > 0, halo_ref[...], 0.0)
    halo_ref[...] = h[rows - POOL_HALO:, :]
    y = jnp.concatenate([halo, h], axis=0)
    pos = j * rows + lax.broadcasted_iota(jnp.int32, (rows, 1), 0)
    gdim = wp_ref.shape[1]
    mixed = []
    swiglu_chunks(2)
    for g, w in enumerate(POOL_WINDOWS):
        swiglu_chunks(2)
        cols = slice(g * gdim, (g + 1) * gdim)
        acc = y[:, cols]
        span = 1
        while span < w:
            acc = acc + pltpu.roll(acc, span, axis=0)
            span *= 2
        count = jnp.minimum(pos + 1, w).astype(F32)
        p = acc[POOL_HALO:, :] / count - h[:, cols]
        mixed.append(jnp.dot(p.astype(BF16), wp_ref[g], preferred_element_type=F32))
    swiglu_chunks(wg_ref.shape[1] // V7X_MXU_DIM)
    x3 = xp + jnp.concatenate(mixed, axis=-1) * ps_ref[...]
    x3_ref[...] = x3
    x2_ref[i % 2] = x + jnp.dot(act_ref[...], wd_ref[...], preferred_element_type=F32)

    h_hi, h_mid, h_lo = _split_bf16(_rms_norm(x3, mn_ref[...]), 3)
    h3_ref[...] = h_hi
    w_hi, w_mid, w_lo = _split_bf16(wrt_ref[...], 3)
    es = ROUTE_SUBLANES
    a = lax.dot_general(jnp.concatenate([w_hi, w_mid, w_lo], axis=0), h_hi, CONTRACT_LAST,
                        preferred_element_type=F32)
    b = lax.dot_general(jnp.concatenate([w_hi, w_mid], axis=0), h_mid, CONTRACT_LAST,
                        preferred_element_type=F32)
    c = lax.dot_general(w_hi, h_lo, CONTRACT_LAST, preferred_element_type=F32)
    logits = ((c + b[es:] + a[2 * es:]) + (b[:es] + a[es:2 * es])) + a[:es]

    eidx = lax.broadcasted_iota(jnp.int32, (es, rows), 0).astype(F32)
    logits = jnp.where(eidx < n_experts, logits, -jnp.inf)
    v1 = jnp.max(logits, axis=0, keepdims=True)
    e1 = jnp.min(jnp.where(logits == v1, eidx, float(es)), axis=0, keepdims=True)
    rest = jnp.where(eidx == e1, -jnp.inf, logits)
    v2 = jnp.max(rest, axis=0, keepdims=True)
    e2 = jnp.min(jnp.where(rest == v2, eidx, float(es)), axis=0, keepdims=True)
    ex = jnp.exp(v2 - v1)
    g1 = 1.0 / (1.0 + ex)
    g2 = ex / (1.0 + ex)

    chosen = ((eidx == e1) | (eidx == e2)).astype(F32)
    t_i = lax.broadcasted_iota(jnp.int32, (rows, rows), 0)
    t_j = lax.broadcasted_iota(jnp.int32, (rows, rows), 1)
    earlier = (t_i < t_j).astype(BF16)
    local_rank = jnp.dot(chosen.astype(BF16), earlier, preferred_element_type=F32)
    count = jnp.sum(chosen, axis=1, keepdims=True)
    padded = jnp.floor((count + (RUN_ALIGN - 1)) * (1.0 / RUN_ALIGN)) * RUN_ALIGN
    ecol = lax.broadcasted_iota(jnp.int32, (es, 1), 0)
    lstart = jnp.zeros((es, 1), F32)
    for e in range(n_experts):
        lstart = lstart + jnp.where(ecol > e, padded[e:e + 1, :], 0.0)
    lpos = lstart + local_rank
    l1 = jnp.sum(jnp.where(eidx == e1, lpos, 0.0), axis=0, keepdims=True)
    l2 = jnp.sum(jnp.where(eidx == e2, lpos, 0.0), axis=0, keepdims=True)

    base = running_ref[...]
    running_ref[...] = base + jnp.where(live, padded, 0.0)
    lane = lax.broadcasted_iota(jnp.int32, (es, V7X_LANES), 1)
    info_ref[0] = jnp.where(lane == INFO_BASE, base,
                            jnp.where(lane == INFO_COUNT, padded,
                                      jnp.where(lane == INFO_LSTART, lstart, 0.0)))

    rec = jnp.zeros((es, rows), F32)
    for idx, val in ((ROUTE_L1, l1), (ROUTE_L2, l2), (ROUTE_G1, g1), (ROUTE_G2, g2),
                     (ROUTE_E1, e1), (ROUTE_E2, e2)):
        rec = jnp.where(eidx == idx, val, rec)
    routet_ref[0] = rec
    rec_full = jnp.concatenate([rec, jnp.zeros((V7X_LANES - es, rows), F32)], axis=0)
    route_ref[...] = rec_full.T


def _ffn_pool_route(x2d, seq_len, ffn_gain, w_gate, w_up, w_down, pool_norm, pool_scale, moe_norm,
                    w_pool, w_router_t, n_experts):
    t, d = x2d.shape
    d_ff = w_gate.shape[1]
    rows = ROUTE_ROWS
    n_tiles = t // rows
    es = ROUTE_SUBLANES
    resident = dict(pipeline_mode=pl.Buffered(1))
    cur = lambda i: (jnp.minimum(i, n_tiles - 1), 0)
    prev = lambda i: (jnp.maximum(i - 1, 0), 0)
    prev3 = lambda i: (jnp.maximum(i - 1, 0), 0, 0)
    vec = pl.BlockSpec((1, d), lambda i: (0, 0))
    kernel = functools.partial(_ffn_pool_route_kernel, n_experts=n_experts,
                               tiles_per_seq=seq_len // rows)
    return pl.pallas_call(
        kernel,
        grid=(n_tiles + 1,),
        in_specs=[pl.BlockSpec((rows, d), cur),
                  vec,
                  pl.BlockSpec((d, d_ff), lambda i: (0, 0), **resident),
                  pl.BlockSpec((d, d_ff), lambda i: (0, 0), **resident),
                  pl.BlockSpec((d_ff, d), lambda i: (0, 0), **resident),
                  vec, vec, vec,
                  pl.BlockSpec(w_pool.shape, lambda i: (0, 0, 0)),
                  pl.BlockSpec(w_router_t.shape, lambda i: (0, 0))],
        out_specs=[pl.BlockSpec((rows, d), prev),
                   pl.BlockSpec((rows, d), prev),
                   pl.BlockSpec((rows, V7X_LANES), prev),
                   pl.BlockSpec((1, es, rows), prev3),
                   pl.BlockSpec((1, es, V7X_LANES), prev3)],
        out_shape=[jax.ShapeDtypeStruct((t, d), F32),
                   jax.ShapeDtypeStruct((t, d), BF16),
                   jax.ShapeDtypeStruct((t, V7X_LANES), F32),
                   jax.ShapeDtypeStruct((n_tiles, es, rows), F32),
                   jax.ShapeDtypeStruct((n_tiles, es, V7X_LANES), F32)],
        scratch_shapes=[pltpu.VMEM((rows, d_ff), BF16),
                        pltpu.VMEM((2, rows, d), F32),
                        pltpu.VMEM((POOL_HALO, d), F32),
                        pltpu.VMEM((es, V7X_LANES), F32)],
        compiler_params=_params("arbitrary"),
        name="swiglu_pool_route",
    )(x2d, ffn_gain, w_gate, w_up, w_down, pool_norm, pool_scale, moe_norm, w_pool, w_router_t)


def _chunks(count, lo_bit, hi_bit):
    for bit in range(lo_bit, hi_bit):
        size = 1 << bit
        offset = (count >> (bit + 1)) << (bit + 1)
        yield offset, size, (count & size) != 0


def _for_each_run_piece(tab_ref, n_experts, max_rows, fn):
    for e in range(n_experts):
        local = tab_ref[0, 0, e]
        sorted_start = tab_ref[0, 0, n_experts + e]
        count = tab_ref[0, 0, 2 * n_experts + e]
        for offset, size, present in _chunks(count, RUN_ALIGN_BITS, max_rows.bit_length()):
            @pl.when(present)
            def _(offset=offset, size=size, local=local, sorted_start=sorted_start):
                fn(pl.multiple_of(local + offset, RUN_ALIGN),
                   pl.multiple_of(sorted_start + offset, RUN_ALIGN), size)


def _wait_tile_runs(tab_ref, n_experts, max_rows, make_copy):
    total = tab_ref[0, 0, TABLE_FIELDS * n_experts]
    for _, size, present in _chunks(total, RUN_ALIGN_BITS, max_rows.bit_length()):
        @pl.when(present)
        def _(size=size):
            make_copy(0, 0, size).wait()


def _slot_rows(n_experts):
    pad = n_experts * (RUN_ALIGN - 1)
    return TOP_K * ROUTE_ROWS + (-(-pad // ROUTE_SUBLANES)) * ROUTE_SUBLANES


ZERO_ROWS = MOE_ROWS // 2


def _dispatch_kernel(tab_ref, prev_tab_ref, h_ref, rt_ref, xs_ref, buf_ref, zero_ref, sem, fill_sem,
                     *, n_experts, max_tail):
    i, n = pl.program_id(0), pl.num_programs(0)
    rows, d = h_ref.shape
    slots = buf_ref.shape[1]
    cur = i % 2

    rt = rt_ref[0]
    l1 = rt[ROUTE_L1:ROUTE_L1 + 1].astype(jnp.int32)
    l2 = rt[ROUTE_L2:ROUTE_L2 + 1].astype(jnp.int32)
    g1 = rt[ROUTE_G1:ROUTE_G1 + 1]
    g2 = rt[ROUTE_G2:ROUTE_G2 + 1]
    pos = lax.broadcasted_iota(jnp.int32, (slots, rows), 0)
    first = pos == l1
    second = pos == l2
    perm = (first | second).astype(BF16)
    x_sorted = jnp.dot(perm, h_ref[...], preferred_element_type=F32)
    gate = jnp.sum(jnp.where(first, g1, 0.0) + jnp.where(second, g2, 0.0), axis=1, keepdims=True)
    lane = lax.broadcasted_iota(jnp.int32, (slots, V7X_LANES), 1)
    gate_lanes = jnp.zeros((slots, V7X_LANES), F32)
    for idx, piece in enumerate(_split_bf16(gate, GATE_PIECES)):
        gate_lanes = jnp.where(lane == idx, piece.astype(F32), gate_lanes)
    buf_ref[cur] = jnp.concatenate([x_sorted, gate_lanes], axis=1).astype(BF16)

    def run_copy(which):
        def make(local, sorted_start, size):
            return pltpu.make_async_copy(buf_ref.at[which, pl.ds(local, size)],
                                         xs_ref.at[pl.ds(sorted_start, size)], sem.at[which])
        return make

    _for_each_run_piece(tab_ref, n_experts, rows,
                        lambda a, b, size: run_copy(cur)(a, b, size).start())

    @pl.when(i > 0)
    def _():
        _wait_tile_runs(prev_tab_ref, n_experts, slots, run_copy(1 - cur))

    @pl.when(i == n - 1)
    def _():
        _wait_tile_runs(tab_ref, n_experts, slots, run_copy(cur))
        zero_ref[...] = jnp.zeros_like(zero_ref)

        def fill_pieces(fn):
            for e in range(n_experts):
                start = tab_ref[0, 0, 3 * n_experts + e]
                length = tab_ref[0, 0, 4 * n_experts + e]
                for offset, size, present in _chunks(length, RUN_ALIGN_BITS, ZERO_ROWS.bit_length()):
                    @pl.when(present)
                    def _(offset=offset, size=size, start=start):
                        fn(pltpu.make_async_copy(
                            zero_ref.at[pl.ds(0, size)],
                            xs_ref.at[pl.ds(pl.multiple_of(start + offset, RUN_ALIGN), size)],
                            fill_sem))
            used_rows = tab_ref[0, 0, TABLE_FIELDS * n_experts + 1]
            for k in range(0, max_tail, ZERO_ROWS):
                @pl.when(used_rows + k < xs_ref.shape[0])
                def _(k=k):
                    fn(pltpu.make_async_copy(
                        zero_ref,
                        xs_ref.at[pl.ds(pl.multiple_of(used_rows + k, RUN_ALIGN), ZERO_ROWS)],
                        fill_sem))

        fill_pieces(lambda c: c.start())
        fill_pieces(lambda c: c.wait())


def _dispatch(h2d, route_t, table, n_rows, n_experts):
    t, d = h2d.shape
    n_tiles = t // ROUTE_ROWS
    slots = _slot_rows(n_experts)
    width = d + V7X_LANES
    tab_block = (1, 1, TABLE_WIDTH)
    kernel = functools.partial(_dispatch_kernel, n_experts=n_experts, max_tail=n_rows - TOP_K * t)
    return pl.pallas_call(
        kernel,
        grid=(n_tiles,),
        in_specs=[pl.BlockSpec(tab_block, lambda i: (i, 0, 0), memory_space=pltpu.SMEM),
                  pl.BlockSpec(tab_block, lambda i: (jnp.maximum(i - 1, 0), 0, 0),
                               memory_space=pltpu.SMEM),
                  pl.BlockSpec((ROUTE_ROWS, d), lambda i: (i, 0)),
                  pl.BlockSpec((1, ROUTE_SUBLANES, ROUTE_ROWS), lambda i: (i, 0, 0))],
        out_specs=pl.BlockSpec(memory_space=pl.ANY),
        out_shape=jax.ShapeDtypeStruct((n_rows, width), BF16),
        scratch_shapes=[pltpu.VMEM((2, slots, width), BF16),
                        pltpu.VMEM((ZERO_ROWS, width), BF16),
                        pltpu.SemaphoreType.DMA((2,)),
                        pltpu.SemaphoreType.DMA(())],
        compiler_params=_params("arbitrary"),
        name="moe_dispatch",
    )(table, table, h2d, route_t)


WEIGHT_CHUNKS = 16
CHUNKS_PER_TILE = 2


def _moe_kernel(block_ref, expert_ref, used_ref, first_ref, slot_ref, step_ref, next_ref, ready_ref,
                x_ref, wg_hbm, wu_hbm, wd_hbm, o_ref,
                wg_buf, wu_buf, wd_buf, stage_g, stage_u, stage_d, act_ref, sem):
    v = pl.program_id(0)
    d = o_ref.shape[1]
    n_chunks = WEIGHT_CHUNKS
    cw = V7X_MXU_DIM
    in_rows = stage_g.shape[1]
    out_rows = stage_d.shape[1]

    def blocks(j):
        return (pl.ds(pl.multiple_of(j * in_rows, in_rows), in_rows),
                pl.ds(pl.multiple_of(j * out_rows, out_rows), out_rows))

    def chunk_copies(e, j, k):
        rin, rout = blocks(j)
        return (pltpu.make_async_copy(wg_hbm.at[e, rin, :], stage_g.at[k], sem.at[k, 0]),
                pltpu.make_async_copy(wu_hbm.at[e, rin, :], stage_u.at[k], sem.at[k, 1]),
                pltpu.make_async_copy(wd_hbm.at[e, rout, :], stage_d.at[k], sem.at[k, 2]))

    def start(e, j, k):
        for copy in chunk_copies(e, j, k):
            copy.start()

    def finish(e, j, k, s):
        for copy in chunk_copies(e, j, k):
            copy.wait()
        rin, rout = blocks(j)
        wg_buf[s, rin, :] = stage_g[k].astype(BF16)
        wu_buf[s, rin, :] = stage_u[k].astype(BF16)
        wd_buf[s, rout, :] = stage_d[k].astype(BF16)

    @pl.when(used_ref[v] == 0)
    def _():
        o_ref[...] = jnp.zeros_like(o_ref)

    @pl.when(used_ref[v] != 0)
    def _():
        e, s, nxt = expert_ref[v], slot_ref[v], next_ref[v]

        @pl.when(first_ref[v] != 0)
        def _():
            def fetch(j, carry):
                start(e, j, 0)
                finish(e, j, 0, s)
                return carry
            lax.fori_loop(ready_ref[v], n_chunks, fetch, 0)

        ahead = [jnp.minimum(CHUNKS_PER_TILE * step_ref[v] + k, n_chunks - 1)
                 for k in range(CHUNKS_PER_TILE)]
        xb = x_ref[:, :d]

        def hidden_chunk(c):
            cols = slice(c * cw, (c + 1) * cw)
            gate = jnp.dot(xb, wg_buf[s, :, cols], preferred_element_type=F32)
            up = jnp.dot(xb, wu_buf[s, :, cols], preferred_element_type=F32)
            act_ref[:, cols] = (_silu(gate) * up).astype(BF16)

        n_cols = act_ref.shape[1] // cw
        bounds = [n_cols * k // CHUNKS_PER_TILE for k in range(CHUNKS_PER_TILE + 1)]
        for k in range(CHUNKS_PER_TILE):
            start(nxt, ahead[k], k)
        for k in range(CHUNKS_PER_TILE):
            for c in range(bounds[k], bounds[k + 1]):
                hidden_chunk(c)
            if k + 1 < CHUNKS_PER_TILE:
                finish(nxt, ahead[k], k, 1 - s)
        y = jnp.dot(act_ref[...], wd_buf[s], preferred_element_type=F32)
        pieces = x_ref[:, d:].astype(F32)
        weight = pieces[:, 0:1]
        for idx in range(1, GATE_PIECES):
            weight = weight + pieces[:, idx:idx + 1]
        o_ref[...] = (y * weight).astype(BF16)
        finish(nxt, ahead[-1], CHUNKS_PER_TILE - 1, 1 - s)


def _moe_experts(xs, plan, w_gate, w_up, w_down):
    n_rows, width = xs.shape
    n_experts, d, d_ff = w_gate.shape
    n_tiles = n_rows // MOE_ROWS
    in_rows, out_rows = d // WEIGHT_CHUNKS, d_ff // WEIGHT_CHUNKS
    assert in_rows % V7X_BF16_TILE_ROWS == 0 and out_rows % V7X_BF16_TILE_ROWS == 0
    grid_spec = pltpu.PrefetchScalarGridSpec(
        num_scalar_prefetch=len(plan),
        grid=(n_tiles,),
        in_specs=[
            pl.BlockSpec((MOE_ROWS, width), lambda v, blk, *_: (blk[v], 0)),
            pl.BlockSpec(memory_space=pl.ANY),
            pl.BlockSpec(memory_space=pl.ANY),
            pl.BlockSpec(memory_space=pl.ANY),
        ],
        out_specs=pl.BlockSpec((MOE_ROWS, d), lambda v, *_: (v, 0)),
        scratch_shapes=[pltpu.VMEM((2, d, d_ff), BF16),
                        pltpu.VMEM((2, d, d_ff), BF16),
                        pltpu.VMEM((2, d_ff, d), BF16),
                        pltpu.VMEM((CHUNKS_PER_TILE, in_rows, d_ff), F32),
                        pltpu.VMEM((CHUNKS_PER_TILE, in_rows, d_ff), F32),
                        pltpu.VMEM((CHUNKS_PER_TILE, out_rows, d), F32),
                        pltpu.VMEM((MOE_ROWS, d_ff), BF16),
                        pltpu.SemaphoreType.DMA((CHUNKS_PER_TILE, 3))],
    )
    return pl.pallas_call(
        _moe_kernel,
        grid_spec=grid_spec,
        out_shape=jax.ShapeDtypeStruct((n_rows, d), BF16),
        compiler_params=_params("arbitrary"),
        name="moe_experts",
    )(*plan, xs, w_gate, w_up, w_down)


def _plan_tiles(group_tiles, n_tiles, n_chunks):
    n_experts = group_tiles.shape[0]
    ends = jnp.cumsum(group_tiles)
    used_tiles = ends[-1]
    tile = jnp.arange(n_tiles, dtype=jnp.int32)
    used = tile < used_tiles
    block = jnp.minimum(tile, used_tiles - 1)
    expert = jnp.sum(ends[None, :] <= block[:, None], axis=1).astype(jnp.int32)
    expert = jnp.minimum(expert, n_experts - 1)

    idx = jnp.arange(n_experts, dtype=jnp.int32)
    present = group_tiles > 0
    later = present[None, :] & (idx[None, :] > idx[:, None])
    next_expert = jnp.min(jnp.where(later, idx[None, :], n_experts), axis=1)
    next_expert = jnp.where(next_expert == n_experts, idx, next_expert)
    earlier = present[None, :] & (idx[None, :] < idx[:, None])
    prev_expert = jnp.max(jnp.where(earlier, idx[None, :], -1), axis=1)
    ready = jnp.where(prev_expert >= 0,
                      jnp.minimum(CHUNKS_PER_TILE * group_tiles[jnp.maximum(prev_expert, 0)], n_chunks),
                      0)
    run_slot = jnp.sum(earlier, axis=1) % 2
    step = block - (ends - group_tiles)[expert]
    first = used & (step == 0)
    cast = lambda a: a.astype(jnp.int32)
    return (cast(block), expert, cast(used), cast(first), cast(run_slot[expert]), cast(step),
            cast(next_expert[expert]), cast(ready[expert]))


def _combine_kernel(tab_ref, next_tab_ref, x_ref, route_ref, g_ref, ys_ref, o_ref, buf_ref, sem,
                    *, n_experts):
    i, n = pl.program_id(0), pl.num_programs(0)
    rows = x_ref.shape[0]
    slots = buf_ref.shape[1]
    cur = i % 2

    def run_copy(which):
        def make(local, sorted_start, size):
            return pltpu.make_async_copy(ys_ref.at[pl.ds(sorted_start, size)],
                                         buf_ref.at[which, pl.ds(local, size)], sem.at[which])
        return make

    @pl.when(i == 0)
    def _():
        buf_ref[...] = jnp.zeros_like(buf_ref)
        _for_each_run_piece(tab_ref, n_experts, rows,
                            lambda a, b, size: run_copy(0)(a, b, size).start())

    @pl.when(i + 1 < n)
    def _():
        _for_each_run_piece(next_tab_ref, n_experts, rows,
                            lambda a, b, size: run_copy(1 - cur)(a, b, size).start())

    _wait_tile_runs(tab_ref, n_experts, slots, run_copy(cur))

    route = route_ref[...]
    l1 = route[:, ROUTE_L1:ROUTE_L1 + 1].astype(jnp.int32)
    l2 = route[:, ROUTE_L2:ROUTE_L2 + 1].astype(jnp.int32)
    pos = lax.broadcasted_iota(jnp.int32, (rows, slots), 1)
    pick = ((pos == l1) | (pos == l2)).astype(BF16)
    y = x_ref[...] + jnp.dot(pick, buf_ref[cur], preferred_element_type=F32)
    o_ref[...] = _rms_norm(y, g_ref[...])


def _combine(x2d, route2d, table, ys, final_norm, n_experts):
    t, d = x2d.shape
    n_tiles = t // ROUTE_ROWS
    slots = _slot_rows(n_experts)
    tab_block = (1, 1, TABLE_WIDTH)
    kernel = functools.partial(_combine_kernel, n_experts=n_experts)
    return pl.pallas_call(
        kernel,
        grid=(n_tiles,),
        in_specs=[pl.BlockSpec(tab_block, lambda i: (i, 0, 0), memory_space=pltpu.SMEM),
                  pl.BlockSpec(tab_block, lambda i: (jnp.minimum(i + 1, n_tiles - 1), 0, 0),
                               memory_space=pltpu.SMEM),
                  pl.BlockSpec((ROUTE_ROWS, d), lambda i: (i, 0)),
                  pl.BlockSpec((ROUTE_ROWS, V7X_LANES), lambda i: (i, 0)),
                  pl.BlockSpec((1, d), lambda i: (0, 0)),
                  pl.BlockSpec(memory_space=pl.ANY)],
        out_specs=pl.BlockSpec((ROUTE_ROWS, d), lambda i: (i, 0)),
        out_shape=jax.ShapeDtypeStruct((t, d), F32),
        scratch_shapes=[pltpu.VMEM((2, slots, d), BF16),
                        pltpu.SemaphoreType.DMA((2,))],
        compiler_params=_params("arbitrary"),
        name="moe_combine",
    )(table, table, x2d, route2d, final_norm, ys)


def kernel(x, attn_norm, attn_w_qkv, attn_sinks, attn_w_o, ffn_norm, ffn_w_gate, ffn_w_up,
           ffn_w_down, pool_norm, pool_w, pool_scale, moe_norm, moe_w_router, moe_w_gate, moe_w_up,
           moe_w_down, final_norm):
    b, s, d = x.shape
    t = b * s
    n_experts = moe_w_router.shape[-1]
    assert s % ATTN_ROWS == 0 and s % ROUTE_ROWS == 0
    assert n_experts <= V7X_SUBLANES and TABLE_FIELDS * n_experts + 2 <= TABLE_WIDTH
    assert attn_norm.shape[0] == 1 and pool_norm.shape[0] == 1

    qkv = _qkv_proj(x.reshape(t, d), attn_norm[0][None], attn_w_qkv[0].astype(BF16))
    x1 = _attention(x, qkv.reshape(b, s, -1), attn_sinks[0].astype(F32), attn_w_o[0].astype(BF16))
    w_router_t = jnp.pad(moe_w_router[0].T, ((0, ROUTE_SUBLANES - n_experts), (0, 0)))
    x3, h3, route, route_t, info = _ffn_pool_route(
        x1.reshape(t, d), s, ffn_norm[0][None], ffn_w_gate[0].astype(BF16), ffn_w_up[0].astype(BF16),
        ffn_w_down[0].astype(BF16), pool_norm[0][None], pool_scale[0][None], moe_norm[0][None],
        pool_w[0].astype(BF16), w_router_t, n_experts)

    n_tiles = t // ROUTE_ROWS
    info = info[:, :n_experts, :].astype(jnp.int32)
    base, padded, lstart = info[..., INFO_BASE], info[..., INFO_COUNT], info[..., INFO_LSTART]
    group_rows = jnp.sum(padded, axis=0)
    group_tiles = -(-group_rows // MOE_ROWS)
    group_start = (jnp.cumsum(group_tiles) - group_tiles) * MOE_ROWS
    max_rows = (t * TOP_K + n_tiles * n_experts * (RUN_ALIGN - 1)
                + n_experts * (MOE_ROWS - RUN_ALIGN))
    n_rows = -(-max_rows // MOE_ROWS) * MOE_ROWS
    per_tile = lambda v: jnp.broadcast_to(v[None, :], (n_tiles, n_experts))
    table = jnp.concatenate(
        [lstart, group_start[None, :] + base, padded,
         per_tile(group_start + group_rows), per_tile(group_tiles * MOE_ROWS - group_rows),
         jnp.sum(padded, axis=1, keepdims=True),
         jnp.broadcast_to(jnp.sum(group_tiles) * MOE_ROWS, (n_tiles, 1)),
         jnp.zeros((n_tiles, TABLE_WIDTH - TABLE_FIELDS * n_experts - 2), jnp.int32)], axis=1)
    table = table.reshape(n_tiles, 1, TABLE_WIDTH)

    xs = _dispatch(h3, route_t, table, n_rows, n_experts)
    plan = _plan_tiles(group_tiles, n_rows // MOE_ROWS, WEIGHT_CHUNKS)
    ys = _moe_experts(xs, plan, moe_w_gate[0], moe_w_up[0], moe_w_down[0])
    out = _combine(x3, route, table, ys, final_norm[None],
                   n_experts)
    return out.reshape(b, s, d)
```

```python
import functools
import itertools

import jax
import jax.numpy as jnp
import numpy as np
from jax import lax
from jax.experimental import pallas as pl
from jax.experimental.pallas import tpu as pltpu

F32 = jnp.float32
BF16 = jnp.bfloat16

HEAD_DIM = 64
N_KV_HEADS = 4
WINDOW = 128
POOL_WINDOWS = (2, 4, 8, 16)
TOP_K = 2
RMS_EPS = 1e-5
NEG_INF = -1e30

V7X_LANES = 128
V7X_SUBLANES = 8
V7X_BF16_TILE_ROWS = 16
V7X_MXU_DIM = 256
V7X_VMEM_LIMIT_BYTES = 60 * 1024 * 1024

QKV_ROWS = 1024
ATTN_ROWS = 1024
ROUTE_ROWS = 512
MOE_ROWS = 512
POOL_HALO = 16

RUN_ALIGN = V7X_BF16_TILE_ROWS
RUN_ALIGN_BITS = RUN_ALIGN.bit_length() - 1
GATE_PIECES = 3
ROUTE_SUBLANES = 16
ROUTE_L1, ROUTE_L2, ROUTE_G1, ROUTE_G2, ROUTE_E1, ROUTE_E2 = range(6)
INFO_BASE, INFO_COUNT, INFO_LSTART = range(3)
TABLE_FIELDS = 5
TABLE_WIDTH = 64


def _params(*semantics):
    return pltpu.CompilerParams(dimension_semantics=semantics,
                                vmem_limit_bytes=V7X_VMEM_LIMIT_BYTES)


def _rms_norm(x, gain):
    ms = jnp.mean(x * x, axis=-1, keepdims=True)
    return x * lax.rsqrt(ms + RMS_EPS) * gain


def _silu(x):
    return x / (1.0 + jnp.exp(-x))


def _split_bf16(x, pieces):
    out = []
    for _ in range(pieces - 1):
        top = x.astype(BF16)
        out.append(top)
        x = x - top.astype(F32)
    out.append(x.astype(BF16))
    return out


CONTRACT_LAST = (((1,), (1,)), ((), ()))
CONTRACT_FIRST = (((0,), (0,)), ((), ()))


def _qkv_kernel(x_ref, g_ref, w_ref, o_ref):
    h = _rms_norm(x_ref[...], g_ref[...]).astype(BF16)
    o_ref[...] = jnp.dot(h, w_ref[...], preferred_element_type=F32).astype(BF16)


def _qkv_proj(x2d, gain, w_qkv):
    t, d = x2d.shape
    n = w_qkv.shape[1]
    return pl.pallas_call(
        _qkv_kernel,
        grid=(t // QKV_ROWS,),
        in_specs=[pl.BlockSpec((QKV_ROWS, d), lambda i: (i, 0)),
                  pl.BlockSpec((1, d), lambda i: (0, 0)),
                  pl.BlockSpec((d, n), lambda i: (0, 0))],
        out_specs=pl.BlockSpec((QKV_ROWS, n), lambda i: (i, 0)),
        out_shape=jax.ShapeDtypeStruct((t, n), BF16),
        compiler_params=_params("parallel"),
        name="rms_qkv",
    )(x2d, gain, w_qkv)


def _attn_kernel(bias_ref, sink_ref, q_ref, kc_ref, kp_ref, vc_ref, vp_ref, x_ref, wo_ref, o_ref,
                 ot_ref, *, n_heads):
    group = n_heads // N_KV_HEADS
    gw = group * WINDOW
    j = pl.program_id(1)
    key = lax.broadcasted_iota(jnp.int32, (WINDOW, n_heads * WINDOW), 0)
    query = lax.broadcasted_iota(jnp.int32, (WINDOW, n_heads * WINDOW), 1) & (WINDOW - 1)
    from_prev = key > query
    from_prev_bf = from_prev.astype(BF16)
    no_prev = from_prev & (j == 0)
    sink = sink_ref[...]

    n_blocks = q_ref.shape[1] // WINDOW

    def score_stage(blk):
        rows = slice(blk * WINDOW, (blk + 1) * WINDOW)
        q_blk = q_ref[0, rows, :] * jnp.asarray(HEAD_DIM ** -0.5, BF16)
        if blk == 0:
            k_prev, v_prev = kp_ref[0], vp_ref[0]
        else:
            prev = slice((blk - 1) * WINDOW, blk * WINDOW)
            k_prev, v_prev = kc_ref[0, prev, :], vc_ref[0, prev, :]
        k_cat = jnp.concatenate([k_prev, kc_ref[0, rows, :]], axis=0)
        v_cat = jnp.concatenate([v_prev, vc_ref[0, rows, :]], axis=0)
        scores = []
        for kvh in range(N_KV_HEADS):
            q_g = jnp.concatenate([q_blk[:, h * HEAD_DIM:(h + 1) * HEAD_DIM]
                                   for h in range(kvh * group, (kvh + 1) * group)], axis=0)
            k_h = k_cat[:, kvh * HEAD_DIM:(kvh + 1) * HEAD_DIM]
            scores.append(lax.dot_general(k_h, q_g, CONTRACT_LAST, preferred_element_type=F32))
        return jnp.concatenate(scores, axis=1), v_cat

    def output_stage(blk, s_all, v_cat):
        rows = slice(blk * WINDOW, (blk + 1) * WINDOW)
        s = jnp.where(from_prev, s_all[:WINDOW], s_all[WINDOW:]) + bias_ref[...]
        if blk == 0:
            s = jnp.where(no_prev, NEG_INF, s)
        m = jnp.maximum(jnp.max(s, axis=0, keepdims=True), sink)
        e = jnp.exp(s - m)
        inv_denom = 1.0 / (jnp.sum(e, axis=0, keepdims=True) + jnp.exp(sink - m))
        e_bf = e.astype(BF16)
        p_prev = e_bf * from_prev_bf
        p_cat = jnp.concatenate([p_prev, e_bf - p_prev], axis=0)
        for kvh in range(N_KV_HEADS):
            lanes = slice(kvh * gw, (kvh + 1) * gw)
            v_h = v_cat[:, kvh * HEAD_DIM:(kvh + 1) * HEAD_DIM]
            o_t = lax.dot_general(v_h, p_cat[:, lanes], CONTRACT_FIRST,
                                  preferred_element_type=F32)
            o_t = (o_t * inv_denom[:, lanes]).astype(BF16)
            for g in range(group):
                h = kvh * group + g
                ot_ref[h * HEAD_DIM:(h + 1) * HEAD_DIM, rows] = o_t[:, g * WINDOW:(g + 1) * WINDOW]

    def project(first_blk, last_blk):
        rows = slice(first_blk * WINDOW, (last_blk + 1) * WINDOW)
        o_ref[0, rows, :] = x_ref[0, rows, :] + lax.dot_general(
            ot_ref[:, rows], wo_ref[...], CONTRACT_FIRST, preferred_element_type=F32)

    staged = score_stage(0)
    for blk in range(n_blocks):
        ahead = score_stage(blk + 1) if blk + 1 < n_blocks else None
        if blk % 2 == 0 and blk >= 2:
            project(blk - 2, blk - 1)
        output_stage(blk, *staged)
        staged = ahead
    project(n_blocks - 2 + n_blocks % 2, n_blocks - 1)


def _alibi_band_bias(n_heads):
    slopes = np.exp2(-8.0 * np.arange(1, n_heads + 1, dtype=np.float64) / n_heads)
    c = np.arange(WINDOW)[:, None, None]
    r = np.arange(WINDOW)[None, None, :]
    dist = ((r - c) % WINDOW).astype(np.float64)
    bias = (-slopes[None, :, None] * dist).reshape(WINDOW, n_heads * WINDOW)
    return jnp.asarray(bias.astype(np.float32))


def _attention(x, qkv, sinks, w_o):
    b, s, d = x.shape
    n_heads = w_o.shape[0] // HEAD_DIM
    q_width = n_heads * HEAD_DIM
    kv_width = N_KV_HEADS * HEAD_DIM
    k_col = q_width // kv_width
    v_col = k_col + 1
    blocks_per_step = ATTN_ROWS // WINDOW
    sink_row = jnp.repeat(sinks, WINDOW)[None, :]

    def prev_block(bi, j):
        return jnp.maximum(j * blocks_per_step - 1, 0)

    kernel = functools.partial(_attn_kernel, n_heads=n_heads)
    return pl.pallas_call(
        kernel,
        grid=(b, s // ATTN_ROWS),
        in_specs=[
            pl.BlockSpec((WINDOW, n_heads * WINDOW), lambda bi, j: (0, 0)),
            pl.BlockSpec((1, n_heads * WINDOW), lambda bi, j: (0, 0)),
            pl.BlockSpec((1, ATTN_ROWS, q_width), lambda bi, j: (bi, j, 0)),
            pl.BlockSpec((1, ATTN_ROWS, kv_width), lambda bi, j: (bi, j, k_col)),
            pl.BlockSpec((1, WINDOW, kv_width), lambda bi, j: (bi, prev_block(bi, j), k_col)),
            pl.BlockSpec((1, ATTN_ROWS, kv_width), lambda bi, j: (bi, j, v_col)),
            pl.BlockSpec((1, WINDOW, kv_width), lambda bi, j: (bi, prev_block(bi, j), v_col)),
            pl.BlockSpec((1, ATTN_ROWS, d), lambda bi, j: (bi, j, 0)),
            pl.BlockSpec((q_width, d), lambda bi, j: (0, 0)),
        ],
        out_specs=pl.BlockSpec((1, ATTN_ROWS, d), lambda bi, j: (bi, j, 0)),
        out_shape=jax.ShapeDtypeStruct((b, s, d), F32),
        scratch_shapes=[pltpu.VMEM((q_width, ATTN_ROWS), BF16)],
        compiler_params=_params("parallel", "parallel"),
        name="swa_attention",
    )(_alibi_band_bias(n_heads), sink_row, qkv, qkv, qkv, qkv, qkv, x, w_o)


def _ffn_pool_route_kernel(x_ref, fg_ref, wg_ref, wu_ref, wd_ref, pn_ref, ps_ref, mn_ref, wp_ref, wrt_ref,
                           x3_ref, h3_ref, route_ref, routet_ref, info_ref,
                           act_ref, x2_ref, halo_ref, running_ref, *, n_experts, tiles_per_seq):
    i = pl.program_id(0)
    rows = x_ref.shape[0]

    @pl.when(i == 0)
    def _():
        x2_ref[...] = jnp.zeros_like(x2_ref)
        halo_ref[...] = jnp.zeros_like(halo_ref)
        running_ref[...] = jnp.zeros_like(running_ref)

    x = x_ref[...]
    hf = _rms_norm(x, fg_ref[...]).astype(BF16)
    pending = iter(range(wg_ref.shape[1] // V7X_MXU_DIM))

    def swiglu_chunks(n):
        for c in itertools.islice(pending, n):
            cols = slice(c * V7X_MXU_DIM, (c + 1) * V7X_MXU_DIM)
            gate = jnp.dot(hf, wg_ref[:, cols], preferred_element_type=F32)
            up = jnp.dot(hf, wu_ref[:, cols], preferred_element_type=F32)
            act_ref[:, cols] = (_silu(gate) * up).astype(BF16)

    live = i > 0
    j = jnp.maximum(i - 1, 0) % tiles_per_seq
    xp = x2_ref[(i + 1) % 2]
    h = _rms_norm(xp, pn_ref[...])
    halo = jnp.where(j > 0, halo_ref[...], 0.0)
    halo_ref[...] = h[rows - POOL_HALO:, :]
    y = jnp.concatenate([halo, h], axis=0)
    pos = j * rows + lax.broadcasted_iota(jnp.int32, (rows, 1), 0)
    gdim = wp_ref.shape[1]
    mixed = []
    swiglu_chunks(2)
    for g, w in enumerate(POOL_WINDOWS):
        swiglu_chunks(2)
        cols = slice(g * gdim, (g + 1) * gdim)
        acc = y[:, cols]
        span = 1
        while span < w:
            acc = acc + pltpu.roll(acc, span, axis=0)
            span *= 2
        count = jnp.minimum(pos + 1, w).astype(F32)
        p = acc[POOL_HALO:, :] / count - h[:, cols]
        mixed.append(jnp.dot(p.astype(BF16), wp_ref[g], preferred_element_type=F32))
    swiglu_chunks(wg_ref.shape[1] // V7X_MXU_DIM)
    x3 = xp + jnp.concatenate(mixed, axis=-1) * ps_ref[...]
    x3_ref[...] = x3
    x2_ref[i % 2] = x + jnp.dot(act_ref[...], wd_ref[...], preferred_element_type=F32)

    h_hi, h_mid, h_lo = _split_bf16(_rms_norm(x3, mn_ref[...]), 3)
    h3_ref[...] = h_hi
    w_hi, w_mid, w_lo = _split_bf16(wrt_ref[...], 3)
    es = ROUTE_SUBLANES
    a = lax.dot_general(jnp.concatenate([w_hi, w_mid, w_lo], axis=0), h_hi, CONTRACT_LAST,
                        preferred_element_type=F32)
    b = lax.dot_general(jnp.concatenate([w_hi, w_mid], axis=0), h_mid, CONTRACT_LAST,
                        preferred_element_type=F32)
    c = lax.dot_general(w_hi, h_lo, CONTRACT_LAST, preferred_element_type=F32)
    logits = ((c + b[es:] + a[2 * es:]) + (b[:es] + a[es:2 * es])) + a[:es]

    eidx = lax.broadcasted_iota(jnp.int32, (es, rows), 0).astype(F32)
    logits = jnp.where(eidx < n_experts, logits, -jnp.inf)
    v1 = jnp.max(logits, axis=0, keepdims=True)
    e1 = jnp.min(jnp.where(logits == v1, eidx, float(es)), axis=0, keepdims=True)
    rest = jnp.where(eidx == e1, -jnp.inf, logits)
    v2 = jnp.max(rest, axis=0, keepdims=True)
    e2 = jnp.min(jnp.where(rest == v2, eidx, float(es)), axis=0, keepdims=True)
    ex = jnp.exp(v2 - v1)
    g1 = 1.0 / (1.0 + ex)
    g2 = ex / (1.0 + ex)

    chosen = ((eidx == e1) | (eidx == e2)).astype(F32)
    t_i = lax.broadcasted_iota(jnp.int32, (rows, rows), 0)
    t_j = lax.broadcasted_iota(jnp.int32, (rows, rows), 1)
    earlier = (t_i < t_j).astype(BF16)
    local_rank = jnp.dot(chosen.astype(BF16), earlier, preferred_element_type=F32)
    count = jnp.sum(chosen, axis=1, keepdims=True)
    padded = jnp.floor((count + (RUN_ALIGN - 1)) * (1.0 / RUN_ALIGN)) * RUN_ALIGN
    ecol = lax.broadcasted_iota(jnp.int32, (es, 1), 0)
    lstart = jnp.zeros((es, 1), F32)
    for e in range(n_experts):
        lstart = lstart + jnp.where(ecol > e, padded[e:e + 1, :], 0.0)
    lpos = lstart + local_rank
    l1 = jnp.sum(jnp.where(eidx == e1, lpos, 0.0), axis=0, keepdims=True)
    l2 = jnp.sum(jnp.where(eidx == e2, lpos, 0.0), axis=0, keepdims=True)

    base = running_ref[...]
    running_ref[...] = base + jnp.where(live, padded, 0.0)
    lane = lax.broadcasted_iota(jnp.int32, (es, V7X_LANES), 1)
    info_ref[0] = jnp.where(lane == INFO_BASE, base,
                            jnp.where(lane == INFO_COUNT, padded,
                                      jnp.where(lane == INFO_LSTART, lstart, 0.0)))

    rec = jnp.zeros((es, rows), F32)
    for idx, val in ((ROUTE_L1, l1), (ROUTE_L2, l2), (ROUTE_G1, g1), (ROUTE_G2, g2),
                     (ROUTE_E1, e1), (ROUTE_E2, e2)):
        rec = jnp.where(eidx == idx, val, rec)
    routet_ref[0] = rec
    rec_full = jnp.concatenate([rec, jnp.zeros((V7X_LANES - es, rows), F32)], axis=0)
    route_ref[...] = rec_full.T


def _ffn_pool_route(x2d, seq_len, ffn_gain, w_gate, w_up, w_down, pool_norm, pool_scale, moe_norm,
                    w_pool, w_router_t, n_experts):
    t, d = x2d.shape
    d_ff = w_gate.shape[1]
    rows = ROUTE_ROWS
    n_tiles = t // rows
    es = ROUTE_SUBLANES
    resident = dict(pipeline_mode=pl.Buffered(1))
    cur = lambda i: (jnp.minimum(i, n_tiles - 1), 0)
    prev = lambda i: (jnp.maximum(i - 1, 0), 0)
    prev3 = lambda i: (jnp.maximum(i - 1, 0), 0, 0)
    vec = pl.BlockSpec((1, d), lambda i: (0, 0))
    kernel = functools.partial(_ffn_pool_route_kernel, n_experts=n_experts,
                               tiles_per_seq=seq_len // rows)
    return pl.pallas_call(
        kernel,
        grid=(n_tiles + 1,),
        in_specs=[pl.BlockSpec((rows, d), cur),
                  vec,
                  pl.BlockSpec((d, d_ff), lambda i: (0, 0), **resident),
                  pl.BlockSpec((d, d_ff), lambda i: (0, 0), **resident),
                  pl.BlockSpec((d_ff, d), lambda i: (0, 0), **resident),
                  vec, vec, vec,
                  pl.BlockSpec(w_pool.shape, lambda i: (0, 0, 0)),
                  pl.BlockSpec(w_router_t.shape, lambda i: (0, 0))],
        out_specs=[pl.BlockSpec((rows, d), prev),
                   pl.BlockSpec((rows, d), prev),
                   pl.BlockSpec((rows, V7X_LANES), prev),
                   pl.BlockSpec((1, es, rows), prev3),
                   pl.BlockSpec((1, es, V7X_LANES), prev3)],
        out_shape=[jax.ShapeDtypeStruct((t, d), F32),
                   jax.ShapeDtypeStruct((t, d), BF16),
                   jax.ShapeDtypeStruct((t, V7X_LANES), F32),
                   jax.ShapeDtypeStruct((n_tiles, es, rows), F32),
                   jax.ShapeDtypeStruct((n_tiles, es, V7X_LANES), F32)],
        scratch_shapes=[pltpu.VMEM((rows, d_ff), BF16),
                        pltpu.VMEM((2, rows, d), F32),
                        pltpu.VMEM((POOL_HALO, d), F32),
                        pltpu.VMEM((es, V7X_LANES), F32)],
        compiler_params=_params("arbitrary"),
        name="swiglu_pool_route",
    )(x2d, ffn_gain, w_gate, w_up, w_down, pool_norm, pool_scale, moe_norm, w_pool, w_router_t)


def _chunks(count, lo_bit, hi_bit):
    for bit in range(lo_bit, hi_bit):
        size = 1 << bit
        offset = (count >> (bit + 1)) << (bit + 1)
        yield offset, size, (count & size) != 0


def _for_each_run_piece(tab_ref, n_experts, max_rows, fn):
    for e in range(n_experts):
        local = tab_ref[0, 0, e]
        sorted_start = tab_ref[0, 0, n_experts + e]
        count = tab_ref[0, 0, 2 * n_experts + e]
        for offset, size, present in _chunks(count, RUN_ALIGN_BITS, max_rows.bit_length()):
            @pl.when(present)
            def _(offset=offset, size=size, local=local, sorted_start=sorted_start):
                fn(pl.multiple_of(local + offset, RUN_ALIGN),
                   pl.multiple_of(sorted_start + offset, RUN_ALIGN), size)


def _wait_tile_runs(tab_ref, n_experts, max_rows, make_copy):
    total = tab_ref[0, 0, TABLE_FIELDS * n_experts]
    for _, size, present in _chunks(total, RUN_ALIGN_BITS, max_rows.bit_length()):
        @pl.when(present)
        def _(size=size):
            make_copy(0, 0, size).wait()


def _slot_rows(n_experts):
    pad = n_experts * (RUN_ALIGN - 1)
    return TOP_K * ROUTE_ROWS + (-(-pad // ROUTE_SUBLANES)) * ROUTE_SUBLANES


ZERO_ROWS = MOE_ROWS // 2


def _dispatch_kernel(tab_ref, prev_tab_ref, h_ref, rt_ref, xs_ref, buf_ref, zero_ref, sem, fill_sem,
                     *, n_experts, max_tail):
    i, n = pl.program_id(0), pl.num_programs(0)
    rows, d = h_ref.shape
    slots = buf_ref.shape[1]
    cur = i % 2

    rt = rt_ref[0]
    l1 = rt[ROUTE_L1:ROUTE_L1 + 1].astype(jnp.int32)
    l2 = rt[ROUTE_L2:ROUTE_L2 + 1].astype(jnp.int32)
    g1 = rt[ROUTE_G1:ROUTE_G1 + 1]
    g2 = rt[ROUTE_G2:ROUTE_G2 + 1]
    pos = lax.broadcasted_iota(jnp.int32, (slots, rows), 0)
    first = pos == l1
    second = pos == l2
    perm = (first | second).astype(BF16)
    x_sorted = jnp.dot(perm, h_ref[...], preferred_element_type=F32)
    gate = jnp.sum(jnp.where(first, g1, 0.0) + jnp.where(second, g2, 0.0), axis=1, keepdims=True)
    lane = lax.broadcasted_iota(jnp.int32, (slots, V7X_LANES), 1)
    gate_lanes = jnp.zeros((slots, V7X_LANES), F32)
    for idx, piece in enumerate(_split_bf16(gate, GATE_PIECES)):
        gate_lanes = jnp.where(lane == idx, piece.astype(F32), gate_lanes)
    buf_ref[cur] = jnp.concatenate([x_sorted, gate_lanes], axis=1).astype(BF16)

    def run_copy(which):
        def make(local, sorted_start, size):
            return pltpu.make_async_copy(buf_ref.at[which, pl.ds(local, size)],
                                         xs_ref.at[pl.ds(sorted_start, size)], sem.at[which])
        return make

    _for_each_run_piece(tab_ref, n_experts, rows,
                        lambda a, b, size: run_copy(cur)(a, b, size).start())

    @pl.when(i > 0)
    def _():
        _wait_tile_runs(prev_tab_ref, n_experts, slots, run_copy(1 - cur))

    @pl.when(i == n - 1)
    def _():
        _wait_tile_runs(tab_ref, n_experts, slots, run_copy(cur))
        zero_ref[...] = jnp.zeros_like(zero_ref)

        def fill_pieces(fn):
            for e in range(n_experts):
                start = tab_ref[0, 0, 3 * n_experts + e]
                length = tab_ref[0, 0, 4 * n_experts + e]
                for offset, size, present in _chunks(length, RUN_ALIGN_BITS, ZERO_ROWS.bit_length()):
                    @pl.when(present)
                    def _(offset=offset, size=size, start=start):
                        fn(pltpu.make_async_copy(
                            zero_ref.at[pl.ds(0, size)],
                            xs_ref.at[pl.ds(pl.multiple_of(start + offset, RUN_ALIGN), size)],
                            fill_sem))
            used_rows = tab_ref[0, 0, TABLE_FIELDS * n_experts + 1]
            for k in range(0, max_tail, ZERO_ROWS):
                @pl.when(used_rows + k < xs_ref.shape[0])
                def _(k=k):
                    fn(pltpu.make_async_copy(
                        zero_ref,
                        xs_ref.at[pl.ds(pl.multiple_of(used_rows + k, RUN_ALIGN), ZERO_ROWS)],
                        fill_sem))

        fill_pieces(lambda c: c.start())
        fill_pieces(lambda c: c.wait())


def _dispatch(h2d, route_t, table, n_rows, n_experts):
    t, d = h2d.shape
    n_tiles = t // ROUTE_ROWS
    slots = _slot_rows(n_experts)
    width = d + V7X_LANES
    tab_block = (1, 1, TABLE_WIDTH)
    kernel = functools.partial(_dispatch_kernel, n_experts=n_experts, max_tail=n_rows - TOP_K * t)
    return pl.pallas_call(
        kernel,
        grid=(n_tiles,),
        in_specs=[pl.BlockSpec(tab_block, lambda i: (i, 0, 0), memory_space=pltpu.SMEM),
                  pl.BlockSpec(tab_block, lambda i: (jnp.maximum(i - 1, 0), 0, 0),
                               memory_space=pltpu.SMEM),
                  pl.BlockSpec((ROUTE_ROWS, d), lambda i: (i, 0)),
                  pl.BlockSpec((1, ROUTE_SUBLANES, ROUTE_ROWS), lambda i: (i, 0, 0))],
        out_specs=pl.BlockSpec(memory_space=pl.ANY),
        out_shape=jax.ShapeDtypeStruct((n_rows, width), BF16),
        scratch_shapes=[pltpu.VMEM((2, slots, width), BF16),
                        pltpu.VMEM((ZERO_ROWS, width), BF16),
                        pltpu.SemaphoreType.DMA((2,)),
                        pltpu.SemaphoreType.DMA(())],
        compiler_params=_params("arbitrary"),
        name="moe_dispatch",
    )(table, table, h2d, route_t)


WEIGHT_CHUNKS = 16
CHUNKS_PER_TILE = 2


def _moe_kernel(block_ref, expert_ref, used_ref, first_ref, slot_ref, step_ref, next_ref, ready_ref,
                x_ref, wg_hbm, wu_hbm, wd_hbm, o_ref,
                wg_buf, wu_buf, wd_buf, stage_g, stage_u, stage_d, act_ref, sem):
    v = pl.program_id(0)
    d = o_ref.shape[1]
    n_chunks = WEIGHT_CHUNKS
    cw = V7X_MXU_DIM
    in_rows = stage_g.shape[1]
    out_rows = stage_d.shape[1]

    def blocks(j):
        return (pl.ds(pl.multiple_of(j * in_rows, in_rows), in_rows),
                pl.ds(pl.multiple_of(j * out_rows, out_rows), out_rows))

    def chunk_copies(e, j, k):
        rin, rout = blocks(j)
        return (pltpu.make_async_copy(wg_hbm.at[e, rin, :], stage_g.at[k], sem.at[k, 0]),
                pltpu.make_async_copy(wu_hbm.at[e, rin, :], stage_u.at[k], sem.at[k, 1]),
                pltpu.make_async_copy(wd_hbm.at[e, rout, :], stage_d.at[k], sem.at[k, 2]))

    def start(e, j, k):
        for copy in chunk_copies(e, j, k):
            copy.start()

    def finish(e, j, k, s):
        for copy in chunk_copies(e, j, k):
            copy.wait()
        rin, rout = blocks(j)
        wg_buf[s, rin, :] = stage_g[k].astype(BF16)
        wu_buf[s, rin, :] = stage_u[k].astype(BF16)
        wd_buf[s, rout, :] = stage_d[k].astype(BF16)

    @pl.when(used_ref[v] == 0)
    def _():
        o_ref[...] = jnp.zeros_like(o_ref)

    @pl.when(used_ref[v] != 0)
    def _():
        e, s, nxt = expert_ref[v], slot_ref[v], next_ref[v]

        @pl.when(first_ref[v] != 0)
        def _():
            missing = ready_ref[v]

            @pl.when(missing < n_chunks)
            def _():
                start(e, missing, missing % CHUNKS_PER_TILE)

            def fetch(j, carry):
                @pl.when(j + 1 < n_chunks)
                def _():
                    start(e, j + 1, (j + 1) % CHUNKS_PER_TILE)
                finish(e, j, j % CHUNKS_PER_TILE, s)
                return carry
            lax.fori_loop(missing, n_chunks, fetch, 0)

        ahead = [jnp.minimum(CHUNKS_PER_TILE * step_ref[v] + k, n_chunks - 1)
                 for k in range(CHUNKS_PER_TILE)]
        xb = x_ref[:, :d]

        def hidden_chunk(c):
            cols = slice(c * cw, (c + 1) * cw)
            gate = jnp.dot(xb, wg_buf[s, :, cols], preferred_element_type=F32)
            up = jnp.dot(xb, wu_buf[s, :, cols], preferred_element_type=F32)
            act_ref[:, cols] = (_silu(gate) * up).astype(BF16)

        n_cols = act_ref.shape[1] // cw
        bounds = [n_cols * k // CHUNKS_PER_TILE for k in range(CHUNKS_PER_TILE + 1)]
        for k in range(CHUNKS_PER_TILE):
            start(nxt, ahead[k], k)
        for k in range(CHUNKS_PER_TILE):
            for c in range(bounds[k], bounds[k + 1]):
                hidden_chunk(c)
            if k + 1 < CHUNKS_PER_TILE:
                finish(nxt, ahead[k], k, 1 - s)
        y = jnp.dot(act_ref[...], wd_buf[s], preferred_element_type=F32)
        pieces = x_ref[:, d:].astype(F32)
        weight = pieces[:, 0:1]
        for idx in range(1, GATE_PIECES):
            weight = weight + pieces[:, idx:idx + 1]
        o_ref[...] = (y * weight).astype(BF16)
        finish(nxt, ahead[-1], CHUNKS_PER_TILE - 1, 1 - s)


def _moe_experts(xs, plan, w_gate, w_up, w_down):
    n_rows, width = xs.shape
    n_experts, d, d_ff = w_gate.shape
    n_tiles = n_rows // MOE_ROWS
    in_rows, out_rows = d // WEIGHT_CHUNKS, d_ff // WEIGHT_CHUNKS
    assert in_rows % V7X_BF16_TILE_ROWS == 0 and out_rows % V7X_BF16_TILE_ROWS == 0
    grid_spec = pltpu.PrefetchScalarGridSpec(
        num_scalar_prefetch=len(plan),
        grid=(n_tiles,),
        in_specs=[
            pl.BlockSpec((MOE_ROWS, width), lambda v, blk, *_: (blk[v], 0)),
            pl.BlockSpec(memory_space=pl.ANY),
            pl.BlockSpec(memory_space=pl.ANY),
            pl.BlockSpec(memory_space=pl.ANY),
        ],
        out_specs=pl.BlockSpec((MOE_ROWS, d), lambda v, *_: (v, 0)),
        scratch_shapes=[pltpu.VMEM((2, d, d_ff), BF16),
                        pltpu.VMEM((2, d, d_ff), BF16),
                        pltpu.VMEM((2, d_ff, d), BF16),
                        pltpu.VMEM((CHUNKS_PER_TILE, in_rows, d_ff), F32),
                        pltpu.VMEM((CHUNKS_PER_TILE, in_rows, d_ff), F32),
                        pltpu.VMEM((CHUNKS_PER_TILE, out_rows, d), F32),
                        pltpu.VMEM((MOE_ROWS, d_ff), BF16),
                        pltpu.SemaphoreType.DMA((CHUNKS_PER_TILE, 3))],
    )
    return pl.pallas_call(
        _moe_kernel,
        grid_spec=grid_spec,
        out_shape=jax.ShapeDtypeStruct((n_rows, d), BF16),
        compiler_params=_params("arbitrary"),
        name="moe_experts",
    )(*plan, xs, w_gate, w_up, w_down)


def _plan_tiles(group_tiles, n_tiles, n_chunks):
    n_experts = group_tiles.shape[0]
    ends = jnp.cumsum(group_tiles)
    used_tiles = ends[-1]
    tile = jnp.arange(n_tiles, dtype=jnp.int32)
    used = tile < used_tiles
    block = jnp.minimum(tile, used_tiles - 1)
    expert = jnp.sum(ends[None, :] <= block[:, None], axis=1).astype(jnp.int32)
    expert = jnp.minimum(expert, n_experts - 1)

    idx = jnp.arange(n_experts, dtype=jnp.int32)
    present = group_tiles > 0
    later = present[None, :] & (idx[None, :] > idx[:, None])
    next_expert = jnp.min(jnp.where(later, idx[None, :], n_experts), axis=1)
    next_expert = jnp.where(next_expert == n_experts, idx, next_expert)
    earlier = present[None, :] & (idx[None, :] < idx[:, None])
    prev_expert = jnp.max(jnp.where(earlier, idx[None, :], -1), axis=1)
    ready = jnp.where(prev_expert >= 0,
                      jnp.minimum(CHUNKS_PER_TILE * group_tiles[jnp.maximum(prev_expert, 0)], n_chunks),
                      0)
    run_slot = jnp.sum(earlier, axis=1) % 2
    step = block - (ends - group_tiles)[expert]
    first = used & (step == 0)
    cast = lambda a: a.astype(jnp.int32)
    return (cast(block), expert, cast(used), cast(first), cast(run_slot[expert]), cast(step),
            cast(next_expert[expert]), cast(ready[expert]))


def _combine_kernel(tab_ref, next_tab_ref, x_ref, route_ref, g_ref, ys_ref, o_ref, buf_ref, sem,
                    *, n_experts):
    i, n = pl.program_id(0), pl.num_programs(0)
    rows = x_ref.shape[0]
    slots = buf_ref.shape[1]
    cur = i % 2

    def run_copy(which):
        def make(local, sorted_start, size):
            return pltpu.make_async_copy(ys_ref.at[pl.ds(sorted_start, size)],
                                         buf_ref.at[which, pl.ds(local, size)], sem.at[which])
        return make

    @pl.when(i == 0)
    def _():
        buf_ref[...] = jnp.zeros_like(buf_ref)
        _for_each_run_piece(tab_ref, n_experts, rows,
                            lambda a, b, size: run_copy(0)(a, b, size).start())

    @pl.when(i + 1 < n)
    def _():
        _for_each_run_piece(next_tab_ref, n_experts, rows,
                            lambda a, b, size: run_copy(1 - cur)(a, b, size).start())

    _wait_tile_runs(tab_ref, n_experts, slots, run_copy(cur))

    route = route_ref[...]
    l1 = route[:, ROUTE_L1:ROUTE_L1 + 1].astype(jnp.int32)
    l2 = route[:, ROUTE_L2:ROUTE_L2 + 1].astype(jnp.int32)
    pos = lax.broadcasted_iota(jnp.int32, (rows, slots), 1)
    pick = ((pos == l1) | (pos == l2)).astype(BF16)
    y = x_ref[...] + jnp.dot(pick, buf_ref[cur], preferred_element_type=F32)
    o_ref[...] = _rms_norm(y, g_ref[...])


def _combine(x2d, route2d, table, ys, final_norm, n_experts):
    t, d = x2d.shape
    n_tiles = t // ROUTE_ROWS
    slots = _slot_rows(n_experts)
    tab_block = (1, 1, TABLE_WIDTH)
    kernel = functools.partial(_combine_kernel, n_experts=n_experts)
    return pl.pallas_call(
        kernel,
        grid=(n_tiles,),
        in_specs=[pl.BlockSpec(tab_block, lambda i: (i, 0, 0), memory_space=pltpu.SMEM),
                  pl.BlockSpec(tab_block, lambda i: (jnp.minimum(i + 1, n_tiles - 1), 0, 0),
                               memory_space=pltpu.SMEM),
                  pl.BlockSpec((ROUTE_ROWS, d), lambda i: (i, 0)),
                  pl.BlockSpec((ROUTE_ROWS, V7X_LANES), lambda i: (i, 0)),
                  pl.BlockSpec((1, d), lambda i: (0, 0)),
                  pl.BlockSpec(memory_space=pl.ANY)],
        out_specs=pl.BlockSpec((ROUTE_ROWS, d), lambda i: (i, 0)),
        out_shape=jax.ShapeDtypeStruct((t, d), F32),
        scratch_shapes=[pltpu.VMEM((2, slots, d), BF16),
                        pltpu.SemaphoreType.DMA((2,))],
        compiler_params=_params("arbitrary"),
        name="moe_combine",
    )(table, table, x2d, route2d, final_norm, ys)


def kernel(x, attn_norm, attn_w_qkv, attn_sinks, attn_w_o, ffn_norm, ffn_w_gate, ffn_w_up,
           ffn_w_down, pool_norm, pool_w, pool_scale, moe_norm, moe_w_router, moe_w_gate, moe_w_up,
           moe_w_down, final_norm):
    b, s, d = x.shape
    t = b * s
    n_experts = moe_w_router.shape[-1]
    assert s % ATTN_ROWS == 0 and s % ROUTE_ROWS == 0
    assert n_experts <= V7X_SUBLANES and TABLE_FIELDS * n_experts + 2 <= TABLE_WIDTH
    assert attn_norm.shape[0] == 1 and pool_norm.shape[0] == 1

    qkv = _qkv_proj(x.reshape(t, d), attn_norm[0][None], attn_w_qkv[0].astype(BF16))
    x1 = _attention(x, qkv.reshape(b, s, -1), attn_sinks[0].astype(F32), attn_w_o[0].astype(BF16))
    w_router_t = jnp.pad(moe_w_router[0].T, ((0, ROUTE_SUBLANES - n_experts), (0, 0)))
    x3, h3, route, route_t, info = _ffn_pool_route(
        x1.reshape(t, d), s, ffn_norm[0][None], ffn_w_gate[0].astype(BF16), ffn_w_up[0].astype(BF16),
        ffn_w_down[0].astype(BF16), pool_norm[0][None], pool_scale[0][None], moe_norm[0][None],
        pool_w[0].astype(BF16), w_router_t, n_experts)

    n_tiles = t // ROUTE_ROWS
    info = info[:, :n_experts, :].astype(jnp.int32)
    base, padded, lstart = info[..., INFO_BASE], info[..., INFO_COUNT], info[..., INFO_LSTART]
    group_rows = jnp.sum(padded, axis=0)
    group_tiles = -(-group_rows // MOE_ROWS)
    group_start = (jnp.cumsum(group_tiles) - group_tiles) * MOE_ROWS
    max_rows = (t * TOP_K + n_tiles * n_experts * (RUN_ALIGN - 1)
                + n_experts * (MOE_ROWS - RUN_ALIGN))
    n_rows = -(-max_rows // MOE_ROWS) * MOE_ROWS
    per_tile = lambda v: jnp.broadcast_to(v[None, :], (n_tiles, n_experts))
    table = jnp.concatenate(
        [lstart, group_start[None, :] + base, padded,
         per_tile(group_start + group_rows), per_tile(group_tiles * MOE_ROWS - group_rows),
         jnp.sum(padded, axis=1, keepdims=True),
         jnp.broadcast_to(jnp.sum(group_tiles) * MOE_ROWS, (n_tiles, 1)),
         jnp.zeros((n_tiles, TABLE_WIDTH - TABLE_FIELDS * n_experts - 2), jnp.int32)], axis=1)
    table = table.reshape(n_tiles, 1, TABLE_WIDTH)

    xs = _dispatch(h3, route_t, table, n_rows, n_experts)
    plan = _plan_tiles(group_tiles, n_rows // MOE_ROWS, WEIGHT_CHUNKS)
    ys = _moe_experts(xs, plan, moe_w_gate[0], moe_w_up[0], moe_w_down[0])
    out = _combine(x3, route, table, ys, final_norm[None],
                   n_experts)
    return out.reshape(b, s, d)
```

```python
import functools
import itertools

import jax
import jax.numpy as jnp
import numpy as np
from jax import lax
from jax.experimental import pallas as pl
from jax.experimental.pallas import tpu as pltpu

F32 = jnp.float32
BF16 = jnp.bfloat16

HEAD_DIM = 64
N_KV_HEADS = 4
WINDOW = 128
POOL_WINDOWS = (2, 4, 8, 16)
TOP_K = 2
RMS_EPS = 1e-5
NEG_INF = -1e30

V7X_LANES = 128
V7X_SUBLANES = 8
V7X_BF16_TILE_ROWS = 16
V7X_MXU_DIM = 256
V7X_VMEM_LIMIT_BYTES = 60 * 1024 * 1024

QKV_ROWS = 1024
ATTN_ROWS = 1024
ROUTE_ROWS = 512
MOE_ROWS = 512
POOL_HALO = 16

RUN_ALIGN = V7X_BF16_TILE_ROWS
RUN_ALIGN_BITS = RUN_ALIGN.bit_length() - 1
GATE_PIECES = 3
ROUTE_SUBLANES = 16
ROUTE_L1, ROUTE_L2, ROUTE_G1, ROUTE_G2, ROUTE_E1, ROUTE_E2 = range(6)
INFO_BASE, INFO_COUNT, INFO_LSTART = range(3)
TABLE_FIELDS = 5
TABLE_WIDTH = 64


def _params(*semantics):
    return pltpu.CompilerParams(dimension_semantics=semantics,
                                vmem_limit_bytes=V7X_VMEM_LIMIT_BYTES)


def _rms_norm(x, gain):
    ms = jnp.mean(x * x, axis=-1, keepdims=True)
    return x * lax.rsqrt(ms + RMS_EPS) * gain


def _silu(x):
    return x / (1.0 + jnp.exp(-x))


def _split_bf16(x, pieces):
    out = []
    for _ in range(pieces - 1):
        top = x.astype(BF16)
        out.append(top)
        x = x - top.astype(F32)
    out.append(x.astype(BF16))
    return out


CONTRACT_LAST = (((1,), (1,)), ((), ()))
CONTRACT_FIRST = (((0,), (0,)), ((), ()))


def _qkv_kernel(x_ref, g_ref, w_ref, o_ref):
    h = _rms_norm(x_ref[...], g_ref[...]).astype(BF16)
    o_ref[...] = jnp.dot(h, w_ref[...], preferred_element_type=F32).astype(BF16)


def _qkv_proj(x2d, gain, w_qkv):
    t, d = x2d.shape
    n = w_qkv.shape[1]
    return pl.pallas_call(
        _qkv_kernel,
        grid=(t // QKV_ROWS,),
        in_specs=[pl.BlockSpec((QKV_ROWS, d), lambda i: (i, 0)),
                  pl.BlockSpec((1, d), lambda i: (0, 0)),
                  pl.BlockSpec((d, n), lambda i: (0, 0))],
        out_specs=pl.BlockSpec((QKV_ROWS, n), lambda i: (i, 0)),
        out_shape=jax.ShapeDtypeStruct((t, n), BF16),
        compiler_params=_params("parallel"),
        name="rms_qkv",
    )(x2d, gain, w_qkv)


def _attn_kernel(bias_ref, sink_ref, q_ref, kc_ref, kp_ref, vc_ref, vp_ref, x_ref, wo_ref, o_ref,
                 ot_ref, *, n_heads):
    group = n_heads // N_KV_HEADS
    gw = group * WINDOW
    j = pl.program_id(1)
    key = lax.broadcasted_iota(jnp.int32, (WINDOW, n_heads * WINDOW), 0)
    query = lax.broadcasted_iota(jnp.int32, (WINDOW, n_heads * WINDOW), 1) & (WINDOW - 1)
    from_prev = key > query
    from_prev_bf = from_prev.astype(BF16)
    no_prev = from_prev & (j == 0)
    sink = sink_ref[...]

    n_blocks = q_ref.shape[1] // WINDOW

    def score_stage(blk):
        rows = slice(blk * WINDOW, (blk + 1) * WINDOW)
        q_blk = q_ref[0, rows, :] * jnp.asarray(HEAD_DIM ** -0.5, BF16)
        if blk == 0:
            k_prev, v_prev = kp_ref[0], vp_ref[0]
        else:
            prev = slice((blk - 1) * WINDOW, blk * WINDOW)
            k_prev, v_prev = kc_ref[0, prev, :], vc_ref[0, prev, :]
        k_cat = jnp.concatenate([k_prev, kc_ref[0, rows, :]], axis=0)
        v_cat = jnp.concatenate([v_prev, vc_ref[0, rows, :]], axis=0)
        scores = []
        for kvh in range(N_KV_HEADS):
            q_g = jnp.concatenate([q_blk[:, h * HEAD_DIM:(h + 1) * HEAD_DIM]
                                   for h in range(kvh * group, (kvh + 1) * group)], axis=0)
            k_h = k_cat[:, kvh * HEAD_DIM:(kvh + 1) * HEAD_DIM]
            scores.append(lax.dot_general(k_h, q_g, CONTRACT_LAST, preferred_element_type=F32))
        return jnp.concatenate(scores, axis=1), v_cat

    def output_stage(blk, s_all, v_cat):
        rows = slice(blk * WINDOW, (blk + 1) * WINDOW)
        s = jnp.where(from_prev, s_all[:WINDOW], s_all[WINDOW:]) + bias_ref[...]
        if blk == 0:
            s = jnp.where(no_prev, NEG_INF, s)
        m = jnp.maximum(jnp.max(s, axis=0, keepdims=True), sink)
        e = jnp.exp(s - m)
        inv_denom = 1.0 / (jnp.sum(e, axis=0, keepdims=True) + jnp.exp(sink - m))
        e_bf = e.astype(BF16)
        p_prev = e_bf * from_prev_bf
        p_cat = jnp.concatenate([p_prev, e_bf - p_prev], axis=0)
        for kvh in range(N_KV_HEADS):
            lanes = slice(kvh * gw, (kvh + 1) * gw)
            v_h = v_cat[:, kvh * HEAD_DIM:(kvh + 1) * HEAD_DIM]
            o_t = lax.dot_general(v_h, p_cat[:, lanes], CONTRACT_FIRST,
                                  preferred_element_type=F32)
            o_t = (o_t * inv_denom[:, lanes]).astype(BF16)
            for g in range(group):
                h = kvh * group + g
                ot_ref[h * HEAD_DIM:(h + 1) * HEAD_DIM, rows] = o_t[:, g * WINDOW:(g + 1) * WINDOW]

    def project(first_blk, last_blk):
        rows = slice(first_blk * WINDOW, (last_blk + 1) * WINDOW)
        o_ref[0, rows, :] = x_ref[0, rows, :] + lax.dot_general(
            ot_ref[:, rows], wo_ref[...], CONTRACT_FIRST, preferred_element_type=F32)

    staged = score_stage(0)
    for blk in range(n_blocks):
        ahead = score_stage(blk + 1) if blk + 1 < n_blocks else None
        if blk % 2 == 0 and blk >= 2:
            project(blk - 2, blk - 1)
        output_stage(blk, *staged)
        staged = ahead
    project(n_blocks - 2 + n_blocks % 2, n_blocks - 1)


def _alibi_band_bias(n_heads):
    slopes = np.exp2(-8.0 * np.arange(1, n_heads + 1, dtype=np.float64) / n_heads)
    c = np.arange(WINDOW)[:, None, None]
    r = np.arange(WINDOW)[None, None, :]
    dist = ((r - c) % WINDOW).astype(np.float64)
    bias = (-slopes[None, :, None] * dist).reshape(WINDOW, n_heads * WINDOW)
    return jnp.asarray(bias.astype(np.float32))


def _attention(x, qkv, sinks, w_o):
    b, s, d = x.shape
    n_heads = w_o.shape[0] // HEAD_DIM
    q_width = n_heads * HEAD_DIM
    kv_width = N_KV_HEADS * HEAD_DIM
    k_col = q_width // kv_width
    v_col = k_col + 1
    blocks_per_step = ATTN_ROWS // WINDOW
    sink_row = jnp.repeat(sinks, WINDOW)[None, :]

    def prev_block(bi, j):
        return jnp.maximum(j * blocks_per_step - 1, 0)

    kernel = functools.partial(_attn_kernel, n_heads=n_heads)
    return pl.pallas_call(
        kernel,
        grid=(b, s // ATTN_ROWS),
        in_specs=[
            pl.BlockSpec((WINDOW, n_heads * WINDOW), lambda bi, j: (0, 0)),
            pl.BlockSpec((1, n_heads * WINDOW), lambda bi, j: (0, 0)),
            pl.BlockSpec((1, ATTN_ROWS, q_width), lambda bi, j: (bi, j, 0)),
            pl.BlockSpec((1, ATTN_ROWS, kv_width), lambda bi, j: (bi, j, k_col)),
            pl.BlockSpec((1, WINDOW, kv_width), lambda bi, j: (bi, prev_block(bi, j), k_col)),
            pl.BlockSpec((1, ATTN_ROWS, kv_width), lambda bi, j: (bi, j, v_col)),
            pl.BlockSpec((1, WINDOW, kv_width), lambda bi, j: (bi, prev_block(bi, j), v_col)),
            pl.BlockSpec((1, ATTN_ROWS, d), lambda bi, j: (bi, j, 0)),
            pl.BlockSpec((q_width, d), lambda bi, j: (0, 0)),
        ],
        out_specs=pl.BlockSpec((1, ATTN_ROWS, d), lambda bi, j: (bi, j, 0)),
        out_shape=jax.ShapeDtypeStruct((b, s, d), F32),
        scratch_shapes=[pltpu.VMEM((q_width, ATTN_ROWS), BF16)],
        compiler_params=_params("parallel", "parallel"),
        name="swa_attention",
    )(_alibi_band_bias(n_heads), sink_row, qkv, qkv, qkv, qkv, qkv, x, w_o)


def _ffn_pool_route_kernel(x_ref, fg_ref, wg_ref, wu_ref, wd_ref, pn_ref, ps_ref, mn_ref, wp_ref, wrt_ref,
                           x3_ref, h3_ref, route_ref, routet_ref, info_ref,
                           act_ref, x2_ref, halo_ref, running_ref, *, n_experts, tiles_per_seq):
    i = pl.program_id(0)
    rows = x_ref.shape[0]

    @pl.when(i == 0)
    def _():
        x2_ref[...] = jnp.zeros_like(x2_ref)
        halo_ref[...] = jnp.zeros_like(halo_ref)
        running_ref[...] = jnp.zeros_like(running_ref)

    x = x_ref[...]
    hf = _rms_norm(x, fg_ref[...]).astype(BF16)
    pending = iter(range(wg_ref.shape[1] // V7X_MXU_DIM))

    def swiglu_chunks(n):
        for c in itertools.islice(pending, n):
            cols = slice(c * V7X_MXU_DIM, (c + 1) * V7X_MXU_DIM)
            gate = jnp.dot(hf, wg_ref[:, cols], preferred_element_type=F32)
            up = jnp.dot(hf, wu_ref[:, cols], preferred_element_type=F32)
            act_ref[:, cols] = (_silu(gate) * up).astype(BF16)

    live = i > 0
    j = jnp.maximum(i - 1, 0) % tiles_per_seq
    xp = x2_ref[(i + 1) % 2]
    h = _rms_norm(xp, pn_ref[...])
    halo = jnp.where(j > 0, halo_ref[...], 0.0)
    halo_ref[...] = h[rows - POOL_HALO:, :]
    y = jnp.concatenate([halo, h], axis=0)
    pos = j * rows + lax.broadcasted_iota(jnp.int32, (rows, 1), 0)
    gdim = wp_ref.shape[1]
    mixed = []
    swiglu_chunks(2)
    for g, w in enumerate(POOL_WINDOWS):
        swiglu_chunks(2)
        cols = slice(g * gdim, (g + 1) * gdim)
        acc = y[:, cols]
        span = 1
        while span < w:
            acc = acc + pltpu.roll(acc, span, axis=0)
            span *= 2
        count = jnp.minimum(pos + 1, w).astype(F32)
        p = acc[POOL_HALO:, :] / count - h[:, cols]
        mixed.append(jnp.dot(p.astype(BF16), wp_ref[g], preferred_element_type=F32))
    swiglu_chunks(wg_ref.shape[1] // V7X_MXU_DIM)
    x3 = xp + jnp.concatenate(mixed, axis=-1) * ps_ref[...]
    x3_ref[...] = x3
    x2_ref[i % 2] = x + jnp.dot(act_ref[...], wd_ref[...], preferred_element_type=F32)

    h_hi, h_mid, h_lo = _split_bf16(_rms_norm(x3, mn_ref[...]), 3)
    h3_ref[...] = h_hi
    w_hi, w_mid, w_lo = _split_bf16(wrt_ref[...], 3)
    es = ROUTE_SUBLANES
    a = lax.dot_general(jnp.concatenate([w_hi, w_mid, w_lo], axis=0), h_hi, CONTRACT_LAST,
                        preferred_element_type=F32)
    b = lax.dot_general(jnp.concatenate([w_hi, w_mid], axis=0), h_mid, CONTRACT_LAST,
                        preferred_element_type=F32)
    c = lax.dot_general(w_hi, h_lo, CONTRACT_LAST, preferred_element_type=F32)
    logits = ((c + b[es:] + a[2 * es:]) + (b[:es] + a[es:2 * es])) + a[:es]

    eidx = lax.broadcasted_iota(jnp.int32, (es, rows), 0).astype(F32)
    logits = jnp.where(eidx < n_experts, logits, -jnp.inf)
    v1 = jnp.max(logits, axis=0, keepdims=True)
    e1 = jnp.min(jnp.where(logits == v1, eidx, float(es)), axis=0, keepdims=True)
    rest = jnp.where(eidx == e1, -jnp.inf, logits)
    v2 = jnp.max(rest, axis=0, keepdims=True)
    e2 = jnp.min(jnp.where(rest == v2, eidx, float(es)), axis=0, keepdims=True)
    ex = jnp.exp(v2 - v1)
    g1 = 1.0 / (1.0 + ex)
    g2 = ex / (1.0 + ex)

    chosen = ((eidx == e1) | (eidx == e2)).astype(F32)
    t_i = lax.broadcasted_iota(jnp.int32, (rows, rows), 0)
    t_j = lax.broadcasted_iota(jnp.int32, (rows, rows), 1)
    earlier = (t_i < t_j).astype(BF16)
    local_rank = jnp.dot(chosen.astype(BF16), earlier, preferred_element_type=F32)
    count = jnp.sum(chosen, axis=1, keepdims=True)
    padded = jnp.floor((count + (RUN_ALIGN - 1)) * (1.0 / RUN_ALIGN)) * RUN_ALIGN
    ecol = lax.broadcasted_iota(jnp.int32, (es, 1), 0)
    lstart = jnp.zeros((es, 1), F32)
    for e in range(n_experts):
        lstart = lstart + jnp.where(ecol > e, padded[e:e + 1, :], 0.0)
    lpos = lstart + local_rank
    l1 = jnp.sum(jnp.where(eidx == e1, lpos, 0.0), axis=0, keepdims=True)
    l2 = jnp.sum(jnp.where(eidx == e2, lpos, 0.0), axis=0, keepdims=True)

    base = running_ref[...]
    running_ref[...] = base + jnp.where(live, padded, 0.0)
    lane = lax.broadcasted_iota(jnp.int32, (es, V7X_LANES), 1)
    info_ref[0] = jnp.where(lane == INFO_BASE, base,
                            jnp.where(lane == INFO_COUNT, padded,
                                      jnp.where(lane == INFO_LSTART, lstart, 0.0)))

    rec = jnp.zeros((es, rows), F32)
    for idx, val in ((ROUTE_L1, l1), (ROUTE_L2, l2), (ROUTE_G1, g1), (ROUTE_G2, g2),
                     (ROUTE_E1, e1), (ROUTE_E2, e2)):
        rec = jnp.where(eidx == idx, val, rec)
    routet_ref[0] = rec
    rec_full = jnp.concatenate([rec, jnp.zeros((V7X_LANES - es, rows), F32)], axis=0)
    route_ref[...] = rec_full.T


def _ffn_pool_route(x2d, seq_len, ffn_gain, w_gate, w_up, w_down, pool_norm, pool_scale, moe_norm,
                    w_pool, w_router_t, n_experts):
    t, d = x2d.shape
    d_ff = w_gate.shape[1]
    rows = ROUTE_ROWS
    n_tiles = t // rows
    es = ROUTE_SUBLANES
    resident = dict(pipeline_mode=pl.Buffered(1))
    cur = lambda i: (jnp.minimum(i, n_tiles - 1), 0)
    prev = lambda i: (jnp.maximum(i - 1, 0), 0)
    prev3 = lambda i: (jnp.maximum(i - 1, 0), 0, 0)
    vec = pl.BlockSpec((1, d), lambda i: (0, 0))
    kernel = functools.partial(_ffn_pool_route_kernel, n_experts=n_experts,
                               tiles_per_seq=seq_len // rows)
    return pl.pallas_call(
        kernel,
        grid=(n_tiles + 1,),
        in_specs=[pl.BlockSpec((rows, d), cur),
                  vec,
                  pl.BlockSpec((d, d_ff), lambda i: (0, 0), **resident),
                  pl.BlockSpec((d, d_ff), lambda i: (0, 0), **resident),
                  pl.BlockSpec((d_ff, d), lambda i: (0, 0), **resident),
                  vec, vec, vec,
                  pl.BlockSpec(w_pool.shape, lambda i: (0, 0, 0)),
                  pl.BlockSpec(w_router_t.shape, lambda i: (0, 0))],
        out_specs=[pl.BlockSpec((rows, d), prev),
                   pl.BlockSpec((rows, d), prev),
                   pl.BlockSpec((rows, V7X_LANES), prev),
                   pl.BlockSpec((1, es, rows), prev3),
                   pl.BlockSpec((1, es, V7X_LANES), prev3)],
        out_shape=[jax.ShapeDtypeStruct((t, d), F32),
                   jax.ShapeDtypeStruct((t, d), BF16),
                   jax.ShapeDtypeStruct((t, V7X_LANES), F32),
                   jax.ShapeDtypeStruct((n_tiles, es, rows), F32),
                   jax.ShapeDtypeStruct((n_tiles, es, V7X_LANES), F32)],
        scratch_shapes=[pltpu.VMEM((rows, d_ff), BF16),
                        pltpu.VMEM((2, rows, d), F32),
                        pltpu.VMEM((POOL_HALO, d), F32),
                        pltpu.VMEM((es, V7X_LANES), F32)],
        compiler_params=_params("arbitrary"),
        name="swiglu_pool_route",
    )(x2d, ffn_gain, w_gate, w_up, w_down, pool_norm, pool_scale, moe_norm, w_pool, w_router_t)


def _chunks(count, lo_bit, hi_bit):
    for bit in range(lo_bit, hi_bit):
        size = 1 << bit
        offset = (count >> (bit + 1)) << (bit + 1)
        yield offset, size, (count & size) != 0


def _for_each_run_piece(tab_ref, n_experts, max_rows, fn):
    for e in range(n_experts):
        local = tab_ref[0, 0, e]
        sorted_start = tab_ref[0, 0, n_experts + e]
        count = tab_ref[0, 0, 2 * n_experts + e]
        for offset, size, present in _chunks(count, RUN_ALIGN_BITS, max_rows.bit_length()):
            @pl.when(present)
            def _(offset=offset, size=size, local=local, sorted_start=sorted_start):
                fn(pl.multiple_of(local + offset, RUN_ALIGN),
                   pl.multiple_of(sorted_start + offset, RUN_ALIGN), size)


def _wait_tile_runs(tab_ref, n_experts, max_rows, make_copy):
    total = tab_ref[0, 0, TABLE_FIELDS * n_experts]
    for _, size, present in _chunks(total, RUN_ALIGN_BITS, max_rows.bit_length()):
        @pl.when(present)
        def _(size=size):
            make_copy(0, 0, size).wait()


def _slot_rows(n_experts):
    pad = n_experts * (RUN_ALIGN - 1)
    return TOP_K * ROUTE_ROWS + (-(-pad // ROUTE_SUBLANES)) * ROUTE_SUBLANES


ZERO_ROWS = MOE_ROWS // 2


def _dispatch_kernel(tab_ref, prev_tab_ref, h_ref, rt_ref, xs_ref, buf_ref, zero_ref, sem, fill_sem,
                     *, n_experts, max_tail):
    i, n = pl.program_id(0), pl.num_programs(0)
    rows, d = h_ref.shape
    slots = buf_ref.shape[1]
    cur = i % 2

    rt = rt_ref[0]
    l1 = rt[ROUTE_L1:ROUTE_L1 + 1].astype(jnp.int32)
    l2 = rt[ROUTE_L2:ROUTE_L2 + 1].astype(jnp.int32)
    g1 = rt[ROUTE_G1:ROUTE_G1 + 1]
    g2 = rt[ROUTE_G2:ROUTE_G2 + 1]
    pos = lax.broadcasted_iota(jnp.int32, (slots, rows), 0)
    first = pos == l1
    second = pos == l2
    perm = (first | second).astype(BF16)
    x_sorted = jnp.dot(perm, h_ref[...], preferred_element_type=F32)
    gate = jnp.sum(jnp.where(first, g1, 0.0) + jnp.where(second, g2, 0.0), axis=1, keepdims=True)
    lane = lax.broadcasted_iota(jnp.int32, (slots, V7X_LANES), 1)
    gate_lanes = jnp.zeros((slots, V7X_LANES), F32)
    for idx, piece in enumerate(_split_bf16(gate, GATE_PIECES)):
        gate_lanes = jnp.where(lane == idx, piece.astype(F32), gate_lanes)
    buf_ref[cur] = jnp.concatenate([x_sorted, gate_lanes], axis=1).astype(BF16)

    def run_copy(which):
        def make(local, sorted_start, size):
            return pltpu.make_async_copy(buf_ref.at[which, pl.ds(local, size)],
                                         xs_ref.at[pl.ds(sorted_start, size)], sem.at[which])
        return make

    _for_each_run_piece(tab_ref, n_experts, rows,
                        lambda a, b, size: run_copy(cur)(a, b, size).start())

    @pl.when(i > 0)
    def _():
        _wait_tile_runs(prev_tab_ref, n_experts, slots, run_copy(1 - cur))

    @pl.when(i == n - 1)
    def _():
        _wait_tile_runs(tab_ref, n_experts, slots, run_copy(cur))
        zero_ref[...] = jnp.zeros_like(zero_ref)

        def fill_pieces(fn):
            for e in range(n_experts):
                start = tab_ref[0, 0, 3 * n_experts + e]
                length = tab_ref[0, 0, 4 * n_experts + e]
                for offset, size, present in _chunks(length, RUN_ALIGN_BITS, ZERO_ROWS.bit_length()):
                    @pl.when(present)
                    def _(offset=offset, size=size, start=start):
                        fn(pltpu.make_async_copy(
                            zero_ref.at[pl.ds(0, size)],
                            xs_ref.at[pl.ds(pl.multiple_of(start + offset, RUN_ALIGN), size)],
                            fill_sem))
            used_rows = tab_ref[0, 0, TABLE_FIELDS * n_experts + 1]
            for k in range(0, max_tail, ZERO_ROWS):
                @pl.when(used_rows + k < xs_ref.shape[0])
                def _(k=k):
                    fn(pltpu.make_async_copy(
                        zero_ref,
                        xs_ref.at[pl.ds(pl.multiple_of(used_rows + k, RUN_ALIGN), ZERO_ROWS)],
                        fill_sem))

        fill_pieces(lambda c: c.start())
        fill_pieces(lambda c: c.wait())


def _dispatch(h2d, route_t, table, n_rows, n_experts):
    t, d = h2d.shape
    n_tiles = t // ROUTE_ROWS
    slots = _slot_rows(n_experts)
    width = d + V7X_LANES
    tab_block = (1, 1, TABLE_WIDTH)
    kernel = functools.partial(_dispatch_kernel, n_experts=n_experts, max_tail=n_rows - TOP_K * t)
    return pl.pallas_call(
        kernel,
        grid=(n_tiles,),
        in_specs=[pl.BlockSpec(tab_block, lambda i: (i, 0, 0), memory_space=pltpu.SMEM),
                  pl.BlockSpec(tab_block, lambda i: (jnp.maximum(i - 1, 0), 0, 0),
                               memory_space=pltpu.SMEM),
                  pl.BlockSpec((ROUTE_ROWS, d), lambda i: (i, 0)),
                  pl.BlockSpec((1, ROUTE_SUBLANES, ROUTE_ROWS), lambda i: (i, 0, 0))],
        out_specs=pl.BlockSpec(memory_space=pl.ANY),
        out_shape=jax.ShapeDtypeStruct((n_rows, width), BF16),
        scratch_shapes=[pltpu.VMEM((2, slots, width), BF16),
                        pltpu.VMEM((ZERO_ROWS, width), BF16),
                        pltpu.SemaphoreType.DMA((2,)),
                        pltpu.SemaphoreType.DMA(())],
        compiler_params=_params("arbitrary"),
        name="moe_dispatch",
    )(table, table, h2d, route_t)


WEIGHT_CHUNKS = 16
CHUNKS_PER_TILE = 2


def _moe_kernel(block_ref, expert_ref, used_ref, first_ref, slot_ref, step_ref, next_ref, ready_ref,
                x_ref, wg_hbm, wu_hbm, wd_hbm, o_ref,
                wg_buf, wu_buf, wd_buf, stage_g, stage_u, stage_d, act_ref, sem):
    v = pl.program_id(0)
    d = o_ref.shape[1]
    n_chunks = WEIGHT_CHUNKS
    cw = V7X_MXU_DIM
    in_rows = stage_g.shape[1]
    out_rows = stage_d.shape[1]

    def blocks(j):
        return (pl.ds(pl.multiple_of(j * in_rows, in_rows), in_rows),
                pl.ds(pl.multiple_of(j * out_rows, out_rows), out_rows))

    def chunk_copies(e, j, k):
        rin, rout = blocks(j)
        return (pltpu.make_async_copy(wg_hbm.at[e, rin, :], stage_g.at[k], sem.at[k, 0]),
                pltpu.make_async_copy(wu_hbm.at[e, rin, :], stage_u.at[k], sem.at[k, 1]),
                pltpu.make_async_copy(wd_hbm.at[e, rout, :], stage_d.at[k], sem.at[k, 2]))

    def start(e, j, k):
        for copy in chunk_copies(e, j, k):
            copy.start()

    def finish(e, j, k, s):
        for copy in chunk_copies(e, j, k):
            copy.wait()
        rin, rout = blocks(j)
        wg_buf[s, rin, :] = stage_g[k].astype(BF16)
        wu_buf[s, rin, :] = stage_u[k].astype(BF16)
        wd_buf[s, rout, :] = stage_d[k].astype(BF16)

    @pl.when(used_ref[v] == 0)
    def _():
        o_ref[...] = jnp.zeros_like(o_ref)

    @pl.when(used_ref[v] != 0)
    def _():
        e, s, nxt = expert_ref[v], slot_ref[v], next_ref[v]

        @pl.when(first_ref[v] != 0)
        def _():
            missing = ready_ref[v]

            @pl.when(missing < n_chunks)
            def _():
                start(e, missing, missing % CHUNKS_PER_TILE)

            def fetch(j, carry):
                @pl.when(j + 1 < n_chunks)
                def _():
                    start(e, j + 1, (j + 1) % CHUNKS_PER_TILE)
                finish(e, j, j % CHUNKS_PER_TILE, s)
                return carry
            lax.fori_loop(missing, n_chunks, fetch, 0)

        ahead = [jnp.minimum(CHUNKS_PER_TILE * step_ref[v] + k, n_chunks - 1)
                 for k in range(CHUNKS_PER_TILE)]
        xb = x_ref[:, :d]

        def hidden_chunk(c):
            cols = slice(c * cw, (c + 1) * cw)
            gate = jnp.dot(xb, wg_buf[s, :, cols], preferred_element_type=F32)
            up = jnp.dot(xb, wu_buf[s, :, cols], preferred_element_type=F32)
            act_ref[:, cols] = (_silu(gate) * up).astype(BF16)

        n_cols = act_ref.shape[1] // cw
        bounds = [n_cols * k // CHUNKS_PER_TILE for k in range(CHUNKS_PER_TILE + 1)]
        for k in range(CHUNKS_PER_TILE):
            start(nxt, ahead[k], k)
        for k in range(CHUNKS_PER_TILE):
            for c in range(bounds[k], bounds[k + 1]):
                hidden_chunk(c)
            if k + 1 < CHUNKS_PER_TILE:
                finish(nxt, ahead[k], k, 1 - s)
        y = jnp.dot(act_ref[...], wd_buf[s], preferred_element_type=F32)
        pieces = x_ref[:, d:].astype(F32)
        weight = pieces[:, 0:1]
        for idx in range(1, GATE_PIECES):
            weight = weight + pieces[:, idx:idx + 1]
        o_ref[...] = (y * weight).astype(BF16)
        finish(nxt, ahead[-1], CHUNKS_PER_TILE - 1, 1 - s)


def _moe_experts(xs, plan, w_gate, w_up, w_down):
    n_rows, width = xs.shape
    n_experts, d, d_ff = w_gate.shape
    n_tiles = n_rows // MOE_ROWS
    in_rows, out_rows = d // WEIGHT_CHUNKS, d_ff // WEIGHT_CHUNKS
    assert in_rows % V7X_BF16_TILE_ROWS == 0 and out_rows % V7X_BF16_TILE_ROWS == 0
    grid_spec = pltpu.PrefetchScalarGridSpec(
        num_scalar_prefetch=len(plan),
        grid=(n_tiles,),
        in_specs=[
            pl.BlockSpec((MOE_ROWS, width), lambda v, blk, *_: (blk[v], 0)),
            pl.BlockSpec(memory_space=pl.ANY),
            pl.BlockSpec(memory_space=pl.ANY),
            pl.BlockSpec(memory_space=pl.ANY),
        ],
        out_specs=pl.BlockSpec((MOE_ROWS, d), lambda v, *_: (v, 0)),
        scratch_shapes=[pltpu.VMEM((2, d, d_ff), BF16),
                        pltpu.VMEM((2, d, d_ff), BF16),
                        pltpu.VMEM((2, d_ff, d), BF16),
                        pltpu.VMEM((CHUNKS_PER_TILE, in_rows, d_ff), F32),
                        pltpu.VMEM((CHUNKS_PER_TILE, in_rows, d_ff), F32),
                        pltpu.VMEM((CHUNKS_PER_TILE, out_rows, d), F32),
                        pltpu.VMEM((MOE_ROWS, d_ff), BF16),
                        pltpu.SemaphoreType.DMA((CHUNKS_PER_TILE, 3))],
    )
    return pl.pallas_call(
        _moe_kernel,
        grid_spec=grid_spec,
        out_shape=jax.ShapeDtypeStruct((n_rows, d), BF16),
        compiler_params=_params("arbitrary"),
        name="moe_experts",
    )(*plan, xs, w_gate, w_up, w_down)


def _plan_tiles(group_tiles, n_tiles, n_chunks):
    n_experts = group_tiles.shape[0]
    ends = jnp.cumsum(group_tiles)
    used_tiles = ends[-1]
    tile = jnp.arange(n_tiles, dtype=jnp.int32)
    used = tile < used_tiles
    block = jnp.minimum(tile, used_tiles - 1)
    expert = jnp.sum(ends[None, :] <= block[:, None], axis=1).astype(jnp.int32)
    expert = jnp.minimum(expert, n_experts - 1)

    idx = jnp.arange(n_experts, dtype=jnp.int32)
    present = group_tiles > 0
    later = present[None, :] & (idx[None, :] > idx[:, None])
    next_expert = jnp.min(jnp.where(later, idx[None, :], n_experts), axis=1)
    next_expert = jnp.where(next_expert == n_experts, idx, next_expert)
    earlier = present[None, :] & (idx[None, :] < idx[:, None])
    prev_expert = jnp.max(jnp.where(earlier, idx[None, :], -1), axis=1)

    def lookup(table, index):
        return jnp.sum(jnp.where(index[:, None] == idx[None, :], table[None, :], 0), axis=1)

    ready = jnp.minimum(CHUNKS_PER_TILE * lookup(group_tiles, prev_expert), n_chunks)
    run_slot = jnp.sum(earlier, axis=1) % 2
    step = block - lookup(ends - group_tiles, expert)
    first = used & (step == 0)
    cast = lambda a: a.astype(jnp.int32)
    return (cast(block), expert, cast(used), cast(first), cast(lookup(run_slot, expert)), cast(step),
            cast(lookup(next_expert, expert)), cast(lookup(ready, expert)))


def _combine_kernel(tab_ref, next_tab_ref, x_ref, route_ref, g_ref, ys_ref, o_ref, buf_ref, sem,
                    *, n_experts):
    i, n = pl.program_id(0), pl.num_programs(0)
    rows = x_ref.shape[0]
    slots = buf_ref.shape[1]
    cur = i % 2

    def run_copy(which):
        def make(local, sorted_start, size):
            return pltpu.make_async_copy(ys_ref.at[pl.ds(sorted_start, size)],
                                         buf_ref.at[which, pl.ds(local, size)], sem.at[which])
        return make

    @pl.when(i == 0)
    def _():
        buf_ref[...] = jnp.zeros_like(buf_ref)
        _for_each_run_piece(tab_ref, n_experts, rows,
                            lambda a, b, size: run_copy(0)(a, b, size).start())

    @pl.when(i + 1 < n)
    def _():
        _for_each_run_piece(next_tab_ref, n_experts, rows,
                            lambda a, b, size: run_copy(1 - cur)(a, b, size).start())

    _wait_tile_runs(tab_ref, n_experts, slots, run_copy(cur))

    route = route_ref[...]
    l1 = route[:, ROUTE_L1:ROUTE_L1 + 1].astype(jnp.int32)
    l2 = route[:, ROUTE_L2:ROUTE_L2 + 1].astype(jnp.int32)
    pos = lax.broadcasted_iota(jnp.int32, (rows, slots), 1)
    pick = ((pos == l1) | (pos == l2)).astype(BF16)
    y = x_ref[...] + jnp.dot(pick, buf_ref[cur], preferred_element_type=F32)
    o_ref[...] = _rms_norm(y, g_ref[...])


def _combine(x2d, route2d, table, ys, final_norm, n_experts):
    t, d = x2d.shape
    n_tiles = t // ROUTE_ROWS
    slots = _slot_rows(n_experts)
    tab_block = (1, 1, TABLE_WIDTH)
    kernel = functools.partial(_combine_kernel, n_experts=n_experts)
    return pl.pallas_call(
        kernel,
        grid=(n_tiles,),
        in_specs=[pl.BlockSpec(tab_block, lambda i: (i, 0, 0), memory_space=pltpu.SMEM),
                  pl.BlockSpec(tab_block, lambda i: (jnp.minimum(i + 1, n_tiles - 1), 0, 0),
                               memory_space=pltpu.SMEM),
                  pl.BlockSpec((ROUTE_ROWS, d), lambda i: (i, 0)),
                  pl.BlockSpec((ROUTE_ROWS, V7X_LANES), lambda i: (i, 0)),
                  pl.BlockSpec((1, d), lambda i: (0, 0)),
                  pl.BlockSpec(memory_space=pl.ANY)],
        out_specs=pl.BlockSpec((ROUTE_ROWS, d), lambda i: (i, 0)),
        out_shape=jax.ShapeDtypeStruct((t, d), F32),
        scratch_shapes=[pltpu.VMEM((2, slots, d), BF16),
                        pltpu.SemaphoreType.DMA((2,))],
        compiler_params=_params("arbitrary"),
        name="moe_combine",
    )(table, table, x2d, route2d, final_norm, ys)


def kernel(x, attn_norm, attn_w_qkv, attn_sinks, attn_w_o, ffn_norm, ffn_w_gate, ffn_w_up,
           ffn_w_down, pool_norm, pool_w, pool_scale, moe_norm, moe_w_router, moe_w_gate, moe_w_up,
           moe_w_down, final_norm):
    b, s, d = x.shape
    t = b * s
    n_experts = moe_w_router.shape[-1]
    assert s % ATTN_ROWS == 0 and s % ROUTE_ROWS == 0
    assert n_experts <= V7X_SUBLANES and TABLE_FIELDS * n_experts + 2 <= TABLE_WIDTH
    assert attn_norm.shape[0] == 1 and pool_norm.shape[0] == 1

    qkv = _qkv_proj(x.reshape(t, d), attn_norm[0][None], attn_w_qkv[0].astype(BF16))
    x1 = _attention(x, qkv.reshape(b, s, -1), attn_sinks[0].astype(F32), attn_w_o[0].astype(BF16))
    w_router_t = jnp.pad(moe_w_router[0].T, ((0, ROUTE_SUBLANES - n_experts), (0, 0)))
    x3, h3, route, route_t, info = _ffn_pool_route(
        x1.reshape(t, d), s, ffn_norm[0][None], ffn_w_gate[0].astype(BF16), ffn_w_up[0].astype(BF16),
        ffn_w_down[0].astype(BF16), pool_norm[0][None], pool_scale[0][None], moe_norm[0][None],
        pool_w[0].astype(BF16), w_router_t, n_experts)

    n_tiles = t // ROUTE_ROWS
    info = info[:, :n_experts, :].astype(jnp.int32)
    base, padded, lstart = info[..., INFO_BASE], info[..., INFO_COUNT], info[..., INFO_LSTART]
    group_rows = jnp.sum(padded, axis=0)
    group_tiles = -(-group_rows // MOE_ROWS)
    group_start = (jnp.cumsum(group_tiles) - group_tiles) * MOE_ROWS
    max_rows = (t * TOP_K + n_tiles * n_experts * (RUN_ALIGN - 1)
                + n_experts * (MOE_ROWS - RUN_ALIGN))
    n_rows = -(-max_rows // MOE_ROWS) * MOE_ROWS
    per_tile = lambda v: jnp.broadcast_to(v[None, :], (n_tiles, n_experts))
    table = jnp.concatenate(
        [lstart, group_start[None, :] + base, padded,
         per_tile(group_start + group_rows), per_tile(group_tiles * MOE_ROWS - group_rows),
         jnp.sum(padded, axis=1, keepdims=True),
         jnp.broadcast_to(jnp.sum(group_tiles) * MOE_ROWS, (n_tiles, 1)),
         jnp.zeros((n_tiles, TABLE_WIDTH - TABLE_FIELDS * n_experts - 2), jnp.int32)], axis=1)
    table = table.reshape(n_tiles, 1, TABLE_WIDTH)

    xs = _dispatch(h3, route_t, table, n_rows, n_experts)
    plan = _plan_tiles(group_tiles, n_rows // MOE_ROWS, WEIGHT_CHUNKS)
    ys = _moe_experts(xs, plan, moe_w_gate[0], moe_w_up[0], moe_w_down[0])
    out = _combine(x3, route, table, ys, final_norm[None],
                   n_experts)
    return out.reshape(b, s, d)
```

```python
import functools
import itertools

import jax
import jax.numpy as jnp
import numpy as np
from jax import lax
from jax.experimental import pallas as pl
from jax.experimental.pallas import tpu as pltpu

F32 = jnp.float32
BF16 = jnp.bfloat16

HEAD_DIM = 64
N_KV_HEADS = 4
WINDOW = 128
POOL_WINDOWS = (2, 4, 8, 16)
TOP_K = 2
RMS_EPS = 1e-5
NEG_INF = -1e30

V7X_LANES = 128
V7X_SUBLANES = 8
V7X_BF16_TILE_ROWS = 16
V7X_MXU_DIM = 256
V7X_VMEM_LIMIT_BYTES = 60 * 1024 * 1024

QKV_ROWS = 2048
ATTN_ROWS = 2048
ROUTE_ROWS = 512
MOE_ROWS = 512
POOL_HALO = 16

RUN_ALIGN = V7X_BF16_TILE_ROWS
RUN_ALIGN_BITS = RUN_ALIGN.bit_length() - 1
GATE_PIECES = 3
ROUTE_SUBLANES = 16
ROUTE_L1, ROUTE_L2, ROUTE_G1, ROUTE_G2, ROUTE_E1, ROUTE_E2 = range(6)
INFO_BASE, INFO_COUNT, INFO_LSTART = range(3)
TABLE_FIELDS = 5
TABLE_WIDTH = 64


def _params(*semantics):
    return pltpu.CompilerParams(dimension_semantics=semantics,
                                vmem_limit_bytes=V7X_VMEM_LIMIT_BYTES)


def _rms_norm(x, gain):
    ms = jnp.mean(x * x, axis=-1, keepdims=True)
    return x * lax.rsqrt(ms + RMS_EPS) * gain


def _silu(x):
    return x / (1.0 + jnp.exp(-x))


def _split_bf16(x, pieces):
    out = []
    for _ in range(pieces - 1):
        top = x.astype(BF16)
        out.append(top)
        x = x - top.astype(F32)
    out.append(x.astype(BF16))
    return out


CONTRACT_LAST = (((1,), (1,)), ((), ()))
CONTRACT_FIRST = (((0,), (0,)), ((), ()))


def _qkv_kernel(x_ref, g_ref, w_ref, o_ref):
    h = _rms_norm(x_ref[...], g_ref[...]).astype(BF16)
    o_ref[...] = jnp.dot(h, w_ref[...], preferred_element_type=F32).astype(BF16)


def _qkv_proj(x2d, gain, w_qkv):
    t, d = x2d.shape
    n = w_qkv.shape[1]
    return pl.pallas_call(
        _qkv_kernel,
        grid=(t // QKV_ROWS,),
        in_specs=[pl.BlockSpec((QKV_ROWS, d), lambda i: (i, 0)),
                  pl.BlockSpec((1, d), lambda i: (0, 0)),
                  pl.BlockSpec((d, n), lambda i: (0, 0))],
        out_specs=pl.BlockSpec((QKV_ROWS, n), lambda i: (i, 0)),
        out_shape=jax.ShapeDtypeStruct((t, n), BF16),
        compiler_params=_params("parallel"),
        name="rms_qkv",
    )(x2d, gain, w_qkv)


def _attn_kernel(bias_ref, sink_ref, q_ref, kc_ref, kp_ref, vc_ref, vp_ref, x_ref, wo_ref, o_ref,
                 ot_ref, *, n_heads):
    group = n_heads // N_KV_HEADS
    gw = group * WINDOW
    j = pl.program_id(1)
    key = lax.broadcasted_iota(jnp.int32, (WINDOW, n_heads * WINDOW), 0)
    query = lax.broadcasted_iota(jnp.int32, (WINDOW, n_heads * WINDOW), 1) & (WINDOW - 1)
    from_prev = key > query
    from_prev_bf = from_prev.astype(BF16)
    no_prev = from_prev & (j == 0)
    sink = sink_ref[...]

    n_blocks = q_ref.shape[1] // WINDOW

    def score_stage(blk):
        rows = slice(blk * WINDOW, (blk + 1) * WINDOW)
        q_blk = q_ref[0, rows, :] * jnp.asarray(HEAD_DIM ** -0.5, BF16)
        if blk == 0:
            k_prev, v_prev = kp_ref[0], vp_ref[0]
        else:
            prev = slice((blk - 1) * WINDOW, blk * WINDOW)
            k_prev, v_prev = kc_ref[0, prev, :], vc_ref[0, prev, :]
        k_cat = jnp.concatenate([k_prev, kc_ref[0, rows, :]], axis=0)
        v_cat = jnp.concatenate([v_prev, vc_ref[0, rows, :]], axis=0)
        scores = []
        for kvh in range(N_KV_HEADS):
            q_g = jnp.concatenate([q_blk[:, h * HEAD_DIM:(h + 1) * HEAD_DIM]
                                   for h in range(kvh * group, (kvh + 1) * group)], axis=0)
            k_h = k_cat[:, kvh * HEAD_DIM:(kvh + 1) * HEAD_DIM]
            scores.append(lax.dot_general(k_h, q_g, CONTRACT_LAST, preferred_element_type=F32))
        return jnp.concatenate(scores, axis=1), v_cat

    def output_stage(blk, s_all, v_cat):
        rows = slice(blk * WINDOW, (blk + 1) * WINDOW)
        s = jnp.where(from_prev, s_all[:WINDOW], s_all[WINDOW:]) + bias_ref[...]
        if blk == 0:
            s = jnp.where(no_prev, NEG_INF, s)
        m = jnp.maximum(jnp.max(s, axis=0, keepdims=True), sink)
        e = jnp.exp(s - m)
        inv_denom = 1.0 / (jnp.sum(e, axis=0, keepdims=True) + jnp.exp(sink - m))
        e_bf = e.astype(BF16)
        p_prev = e_bf * from_prev_bf
        p_cat = jnp.concatenate([p_prev, e_bf - p_prev], axis=0)
        for kvh in range(N_KV_HEADS):
            lanes = slice(kvh * gw, (kvh + 1) * gw)
            v_h = v_cat[:, kvh * HEAD_DIM:(kvh + 1) * HEAD_DIM]
            o_t = lax.dot_general(v_h, p_cat[:, lanes], CONTRACT_FIRST,
                                  preferred_element_type=F32)
            o_t = (o_t * inv_denom[:, lanes]).astype(BF16)
            for g in range(group):
                h = kvh * group + g
                ot_ref[h * HEAD_DIM:(h + 1) * HEAD_DIM, rows] = o_t[:, g * WINDOW:(g + 1) * WINDOW]

    def project(first_blk, last_blk):
        rows = slice(first_blk * WINDOW, (last_blk + 1) * WINDOW)
        o_ref[0, rows, :] = x_ref[0, rows, :] + lax.dot_general(
            ot_ref[:, rows], wo_ref[...], CONTRACT_FIRST, preferred_element_type=F32)

    staged = score_stage(0)
    for blk in range(n_blocks):
        ahead = score_stage(blk + 1) if blk + 1 < n_blocks else None
        if blk % 2 == 0 and blk >= 2:
            project(blk - 2, blk - 1)
        output_stage(blk, *staged)
        staged = ahead
    project(n_blocks - 2 + n_blocks % 2, n_blocks - 1)


def _alibi_band_bias(n_heads):
    slopes = np.exp2(-8.0 * np.arange(1, n_heads + 1, dtype=np.float64) / n_heads)
    c = np.arange(WINDOW)[:, None, None]
    r = np.arange(WINDOW)[None, None, :]
    dist = ((r - c) % WINDOW).astype(np.float64)
    bias = (-slopes[None, :, None] * dist).reshape(WINDOW, n_heads * WINDOW)
    return jnp.asarray(bias.astype(np.float32))


def _attention(x, qkv, sinks, w_o):
    b, s, d = x.shape
    n_heads = w_o.shape[0] // HEAD_DIM
    q_width = n_heads * HEAD_DIM
    kv_width = N_KV_HEADS * HEAD_DIM
    k_col = q_width // kv_width
    v_col = k_col + 1
    blocks_per_step = ATTN_ROWS // WINDOW
    sink_row = jnp.repeat(sinks, WINDOW)[None, :]

    def prev_block(bi, j):
        return jnp.maximum(j * blocks_per_step - 1, 0)

    kernel = functools.partial(_attn_kernel, n_heads=n_heads)
    return pl.pallas_call(
        kernel,
        grid=(b, s // ATTN_ROWS),
        in_specs=[
            pl.BlockSpec((WINDOW, n_heads * WINDOW), lambda bi, j: (0, 0)),
            pl.BlockSpec((1, n_heads * WINDOW), lambda bi, j: (0, 0)),
            pl.BlockSpec((1, ATTN_ROWS, q_width), lambda bi, j: (bi, j, 0)),
            pl.BlockSpec((1, ATTN_ROWS, kv_width), lambda bi, j: (bi, j, k_col)),
            pl.BlockSpec((1, WINDOW, kv_width), lambda bi, j: (bi, prev_block(bi, j), k_col)),
            pl.BlockSpec((1, ATTN_ROWS, kv_width), lambda bi, j: (bi, j, v_col)),
            pl.BlockSpec((1, WINDOW, kv_width), lambda bi, j: (bi, prev_block(bi, j), v_col)),
            pl.BlockSpec((1, ATTN_ROWS, d), lambda bi, j: (bi, j, 0)),
            pl.BlockSpec((q_width, d), lambda bi, j: (0, 0)),
        ],
        out_specs=pl.BlockSpec((1, ATTN_ROWS, d), lambda bi, j: (bi, j, 0)),
        out_shape=jax.ShapeDtypeStruct((b, s, d), F32),
        scratch_shapes=[pltpu.VMEM((q_width, ATTN_ROWS), BF16)],
        compiler_params=_params("parallel", "parallel"),
        name="swa_attention",
    )(_alibi_band_bias(n_heads), sink_row, qkv, qkv, qkv, qkv, qkv, x, w_o)


def _ffn_pool_route_kernel(x_ref, fg_ref, wg_ref, wu_ref, wd_ref, pn_ref, ps_ref, mn_ref, wp_ref, wrt_ref,
                           x3_ref, h3_ref, route_ref, routet_ref, info_ref,
                           act_ref, x2_ref, halo_ref, running_ref, *, n_experts, tiles_per_seq):
    i = pl.program_id(0)
    rows = x_ref.shape[0]

    @pl.when(i == 0)
    def _():
        x2_ref[...] = jnp.zeros_like(x2_ref)
        halo_ref[...] = jnp.zeros_like(halo_ref)
        running_ref[...] = jnp.zeros_like(running_ref)

    x = x_ref[...]
    hf = _rms_norm(x, fg_ref[...]).astype(BF16)
    pending = iter(range(wg_ref.shape[1] // V7X_MXU_DIM))

    def swiglu_chunks(n):
        for c in itertools.islice(pending, n):
            cols = slice(c * V7X_MXU_DIM, (c + 1) * V7X_MXU_DIM)
            gate = jnp.dot(hf, wg_ref[:, cols], preferred_element_type=F32)
            up = jnp.dot(hf, wu_ref[:, cols], preferred_element_type=F32)
            act_ref[:, cols] = (_silu(gate) * up).astype(BF16)

    live = i > 0
    j = jnp.maximum(i - 1, 0) % tiles_per_seq
    xp = x2_ref[(i + 1) % 2]
    h = _rms_norm(xp, pn_ref[...])
    halo = jnp.where(j > 0, halo_ref[...], 0.0)
    halo_ref[...] = h[rows - POOL_HALO:, :]
    y = jnp.concatenate([halo, h], axis=0)
    pos = j * rows + lax.broadcasted_iota(jnp.int32, (rows, 1), 0)
    gdim = wp_ref.shape[1]
    mixed = []
    swiglu_chunks(2)
    for g, w in enumerate(POOL_WINDOWS):
        swiglu_chunks(2)
        cols = slice(g * gdim, (g + 1) * gdim)
        acc = y[:, cols]
        span = 1
        while span < w:
            acc = acc + pltpu.roll(acc, span, axis=0)
            span *= 2
        count = jnp.minimum(pos + 1, w).astype(F32)
        p = acc[POOL_HALO:, :] / count - h[:, cols]
        mixed.append(jnp.dot(p.astype(BF16), wp_ref[g], preferred_element_type=F32))
    swiglu_chunks(wg_ref.shape[1] // V7X_MXU_DIM)
    x3 = xp + jnp.concatenate(mixed, axis=-1) * ps_ref[...]
    x3_ref[...] = x3
    x2_ref[i % 2] = x + jnp.dot(act_ref[...], wd_ref[...], preferred_element_type=F32)

    h_hi, h_mid, h_lo = _split_bf16(_rms_norm(x3, mn_ref[...]), 3)
    h3_ref[...] = h_hi
    w_hi, w_mid, w_lo = _split_bf16(wrt_ref[...], 3)
    es = ROUTE_SUBLANES
    a = lax.dot_general(jnp.concatenate([w_hi, w_mid, w_lo], axis=0), h_hi, CONTRACT_LAST,
                        preferred_element_type=F32)
    b = lax.dot_general(jnp.concatenate([w_hi, w_mid], axis=0), h_mid, CONTRACT_LAST,
                        preferred_element_type=F32)
    c = lax.dot_general(w_hi, h_lo, CONTRACT_LAST, preferred_element_type=F32)
    logits = ((c + b[es:] + a[2 * es:]) + (b[:es] + a[es:2 * es])) + a[:es]

    eidx = lax.broadcasted_iota(jnp.int32, (es, rows), 0).astype(F32)
    logits = jnp.where(eidx < n_experts, logits, -jnp.inf)
    v1 = jnp.max(logits, axis=0, keepdims=True)
    e1 = jnp.min(jnp.where(logits == v1, eidx, float(es)), axis=0, keepdims=True)
    rest = jnp.where(eidx == e1, -jnp.inf, logits)
    v2 = jnp.max(rest, axis=0, keepdims=True)
    e2 = jnp.min(jnp.where(rest == v2, eidx, float(es)), axis=0, keepdims=True)
    ex = jnp.exp(v2 - v1)
    g1 = 1.0 / (1.0 + ex)
    g2 = ex / (1.0 + ex)

    chosen = ((eidx == e1) | (eidx == e2)).astype(F32)
    t_i = lax.broadcasted_iota(jnp.int32, (rows, rows), 0)
    t_j = lax.broadcasted_iota(jnp.int32, (rows, rows), 1)
    earlier = (t_i < t_j).astype(BF16)
    local_rank = jnp.dot(chosen.astype(BF16), earlier, preferred_element_type=F32)
    count = jnp.sum(chosen, axis=1, keepdims=True)
    padded = jnp.floor((count + (RUN_ALIGN - 1)) * (1.0 / RUN_ALIGN)) * RUN_ALIGN
    ecol = lax.broadcasted_iota(jnp.int32, (es, 1), 0)
    lstart = jnp.zeros((es, 1), F32)
    for e in range(n_experts):
        lstart = lstart + jnp.where(ecol > e, padded[e:e + 1, :], 0.0)
    lpos = lstart + local_rank
    l1 = jnp.sum(jnp.where(eidx == e1, lpos, 0.0), axis=0, keepdims=True)
    l2 = jnp.sum(jnp.where(eidx == e2, lpos, 0.0), axis=0, keepdims=True)

    base = running_ref[...]
    running_ref[...] = base + jnp.where(live, padded, 0.0)
    lane = lax.broadcasted_iota(jnp.int32, (es, V7X_LANES), 1)
    info_ref[0] = jnp.where(lane == INFO_BASE, base,
                            jnp.where(lane == INFO_COUNT, padded,
                                      jnp.where(lane == INFO_LSTART, lstart, 0.0)))

    rec = jnp.zeros((es, rows), F32)
    for idx, val in ((ROUTE_L1, l1), (ROUTE_L2, l2), (ROUTE_G1, g1), (ROUTE_G2, g2),
                     (ROUTE_E1, e1), (ROUTE_E2, e2)):
        rec = jnp.where(eidx == idx, val, rec)
    routet_ref[0] = rec
    rec_full = jnp.concatenate([rec, jnp.zeros((V7X_LANES - es, rows), F32)], axis=0)
    route_ref[...] = rec_full.T


def _ffn_pool_route(x2d, seq_len, ffn_gain, w_gate, w_up, w_down, pool_norm, pool_scale, moe_norm,
                    w_pool, w_router_t, n_experts):
    t, d = x2d.shape
    d_ff = w_gate.shape[1]
    rows = ROUTE_ROWS
    n_tiles = t // rows
    es = ROUTE_SUBLANES
    resident = dict(pipeline_mode=pl.Buffered(1))
    cur = lambda i: (jnp.minimum(i, n_tiles - 1), 0)
    prev = lambda i: (jnp.maximum(i - 1, 0), 0)
    prev3 = lambda i: (jnp.maximum(i - 1, 0), 0, 0)
    vec = pl.BlockSpec((1, d), lambda i: (0, 0))
    kernel = functools.partial(_ffn_pool_route_kernel, n_experts=n_experts,
                               tiles_per_seq=seq_len // rows)
    return pl.pallas_call(
        kernel,
        grid=(n_tiles + 1,),
        in_specs=[pl.BlockSpec((rows, d), cur),
                  vec,
                  pl.BlockSpec((d, d_ff), lambda i: (0, 0), **resident),
                  pl.BlockSpec((d, d_ff), lambda i: (0, 0), **resident),
                  pl.BlockSpec((d_ff, d), lambda i: (0, 0), **resident),
                  vec, vec, vec,
                  pl.BlockSpec(w_pool.shape, lambda i: (0, 0, 0)),
                  pl.BlockSpec(w_router_t.shape, lambda i: (0, 0))],
        out_specs=[pl.BlockSpec((rows, d), prev),
                   pl.BlockSpec((rows, d), prev),
                   pl.BlockSpec((rows, V7X_LANES), prev),
                   pl.BlockSpec((1, es, rows), prev3),
                   pl.BlockSpec((1, es, V7X_LANES), prev3)],
        out_shape=[jax.ShapeDtypeStruct((t, d), F32),
                   jax.ShapeDtypeStruct((t, d), BF16),
                   jax.ShapeDtypeStruct((t, V7X_LANES), F32),
                   jax.ShapeDtypeStruct((n_tiles, es, rows), F32),
                   jax.ShapeDtypeStruct((n_tiles, es, V7X_LANES), F32)],
        scratch_shapes=[pltpu.VMEM((rows, d_ff), BF16),
                        pltpu.VMEM((2, rows, d), F32),
                        pltpu.VMEM((POOL_HALO, d), F32),
                        pltpu.VMEM((es, V7X_LANES), F32)],
        compiler_params=_params("arbitrary"),
        name="swiglu_pool_route",
    )(x2d, ffn_gain, w_gate, w_up, w_down, pool_norm, pool_scale, moe_norm, w_pool, w_router_t)


def _chunks(count, lo_bit, hi_bit):
    for bit in range(lo_bit, hi_bit):
        size = 1 << bit
        offset = (count >> (bit + 1)) << (bit + 1)
        yield offset, size, (count & size) != 0


def _for_each_run_piece(tab_ref, n_experts, max_rows, fn):
    for e in range(n_experts):
        local = tab_ref[0, 0, e]
        sorted_start = tab_ref[0, 0, n_experts + e]
        count = tab_ref[0, 0, 2 * n_experts + e]
        for offset, size, present in _chunks(count, RUN_ALIGN_BITS, max_rows.bit_length()):
            @pl.when(present)
            def _(offset=offset, size=size, local=local, sorted_start=sorted_start):
                fn(pl.multiple_of(local + offset, RUN_ALIGN),
                   pl.multiple_of(sorted_start + offset, RUN_ALIGN), size)


def _wait_tile_runs(tab_ref, n_experts, max_rows, make_copy):
    total = tab_ref[0, 0, TABLE_FIELDS * n_experts]
    for _, size, present in _chunks(total, RUN_ALIGN_BITS, max_rows.bit_length()):
        @pl.when(present)
        def _(size=size):
            make_copy(0, 0, size).wait()


def _slot_rows(n_experts):
    pad = n_experts * (RUN_ALIGN - 1)
    return TOP_K * ROUTE_ROWS + (-(-pad // ROUTE_SUBLANES)) * ROUTE_SUBLANES


ZERO_ROWS = MOE_ROWS // 2


def _dispatch_kernel(tab_ref, prev_tab_ref, h_ref, rt_ref, xs_ref, buf_ref, zero_ref, sem, fill_sem,
                     *, n_experts, max_tail):
    i, n = pl.program_id(0), pl.num_programs(0)
    rows, d = h_ref.shape
    slots = buf_ref.shape[1]
    cur = i % 2

    rt = rt_ref[0]
    l1 = rt[ROUTE_L1:ROUTE_L1 + 1].astype(jnp.int32)
    l2 = rt[ROUTE_L2:ROUTE_L2 + 1].astype(jnp.int32)
    g1 = rt[ROUTE_G1:ROUTE_G1 + 1]
    g2 = rt[ROUTE_G2:ROUTE_G2 + 1]
    pos = lax.broadcasted_iota(jnp.int32, (slots, rows), 0)
    first = pos == l1
    second = pos == l2
    perm = (first | second).astype(BF16)
    x_sorted = jnp.dot(perm, h_ref[...], preferred_element_type=F32)
    gate = jnp.sum(jnp.where(first, g1, 0.0) + jnp.where(second, g2, 0.0), axis=1, keepdims=True)
    lane = lax.broadcasted_iota(jnp.int32, (slots, V7X_LANES), 1)
    gate_lanes = jnp.zeros((slots, V7X_LANES), F32)
    for idx, piece in enumerate(_split_bf16(gate, GATE_PIECES)):
        gate_lanes = jnp.where(lane == idx, piece.astype(F32), gate_lanes)
    buf_ref[cur] = jnp.concatenate([x_sorted, gate_lanes], axis=1).astype(BF16)

    def run_copy(which):
        def make(local, sorted_start, size):
            return pltpu.make_async_copy(buf_ref.at[which, pl.ds(local, size)],
                                         xs_ref.at[pl.ds(sorted_start, size)], sem.at[which])
        return make

    _for_each_run_piece(tab_ref, n_experts, rows,
                        lambda a, b, size: run_copy(cur)(a, b, size).start())

    @pl.when(i > 0)
    def _():
        _wait_tile_runs(prev_tab_ref, n_experts, slots, run_copy(1 - cur))

    @pl.when(i == n - 1)
    def _():
        _wait_tile_runs(tab_ref, n_experts, slots, run_copy(cur))
        zero_ref[...] = jnp.zeros_like(zero_ref)

        def fill_pieces(fn):
            for e in range(n_experts):
                start = tab_ref[0, 0, 3 * n_experts + e]
                length = tab_ref[0, 0, 4 * n_experts + e]
                for offset, size, present in _chunks(length, RUN_ALIGN_BITS, ZERO_ROWS.bit_length()):
                    @pl.when(present)
                    def _(offset=offset, size=size, start=start):
                        fn(pltpu.make_async_copy(
                            zero_ref.at[pl.ds(0, size)],
                            xs_ref.at[pl.ds(pl.multiple_of(start + offset, RUN_ALIGN), size)],
                            fill_sem))
            used_rows = tab_ref[0, 0, TABLE_FIELDS * n_experts + 1]
            for k in range(0, max_tail, ZERO_ROWS):
                @pl.when(used_rows + k < xs_ref.shape[0])
                def _(k=k):
                    fn(pltpu.make_async_copy(
                        zero_ref,
                        xs_ref.at[pl.ds(pl.multiple_of(used_rows + k, RUN_ALIGN), ZERO_ROWS)],
                        fill_sem))

        fill_pieces(lambda c: c.start())
        fill_pieces(lambda c: c.wait())


def _dispatch(h2d, route_t, table, n_rows, n_experts):
    t, d = h2d.shape
    n_tiles = t // ROUTE_ROWS
    slots = _slot_rows(n_experts)
    width = d + V7X_LANES
    tab_block = (1, 1, TABLE_WIDTH)
    kernel = functools.partial(_dispatch_kernel, n_experts=n_experts, max_tail=n_rows - TOP_K * t)
    return pl.pallas_call(
        kernel,
        grid=(n_tiles,),
        in_specs=[pl.BlockSpec(tab_block, lambda i: (i, 0, 0), memory_space=pltpu.SMEM),
                  pl.BlockSpec(tab_block, lambda i: (jnp.maximum(i - 1, 0), 0, 0),
                               memory_space=pltpu.SMEM),
                  pl.BlockSpec((ROUTE_ROWS, d), lambda i: (i, 0)),
                  pl.BlockSpec((1, ROUTE_SUBLANES, ROUTE_ROWS), lambda i: (i, 0, 0))],
        out_specs=pl.BlockSpec(memory_space=pl.ANY),
        out_shape=jax.ShapeDtypeStruct((n_rows, width), BF16),
        scratch_shapes=[pltpu.VMEM((2, slots, width), BF16),
                        pltpu.VMEM((ZERO_ROWS, width), BF16),
                        pltpu.SemaphoreType.DMA((2,)),
                        pltpu.SemaphoreType.DMA(())],
        compiler_params=_params("arbitrary"),
        name="moe_dispatch",
    )(table, table, h2d, route_t)


WEIGHT_CHUNKS = 16
CHUNKS_PER_TILE = 2


def _moe_kernel(block_ref, expert_ref, used_ref, first_ref, slot_ref, step_ref, next_ref, ready_ref,
                x_ref, wg_hbm, wu_hbm, wd_hbm, o_ref,
                wg_buf, wu_buf, wd_buf, stage_g, stage_u, stage_d, act_ref, sem):
    v = pl.program_id(0)
    d = o_ref.shape[1]
    n_chunks = WEIGHT_CHUNKS
    cw = V7X_MXU_DIM
    in_rows = stage_g.shape[1]
    out_rows = stage_d.shape[1]

    def blocks(j):
        return (pl.ds(pl.multiple_of(j * in_rows, in_rows), in_rows),
                pl.ds(pl.multiple_of(j * out_rows, out_rows), out_rows))

    def chunk_copies(e, j, k):
        rin, rout = blocks(j)
        return (pltpu.make_async_copy(wg_hbm.at[e, rin, :], stage_g.at[k], sem.at[k, 0]),
                pltpu.make_async_copy(wu_hbm.at[e, rin, :], stage_u.at[k], sem.at[k, 1]),
                pltpu.make_async_copy(wd_hbm.at[e, rout, :], stage_d.at[k], sem.at[k, 2]))

    def start(e, j, k):
        for copy in chunk_copies(e, j, k):
            copy.start()

    def finish(e, j, k, s):
        for copy in chunk_copies(e, j, k):
            copy.wait()
        rin, rout = blocks(j)
        wg_buf[s, rin, :] = stage_g[k].astype(BF16)
        wu_buf[s, rin, :] = stage_u[k].astype(BF16)
        wd_buf[s, rout, :] = stage_d[k].astype(BF16)

    @pl.when(used_ref[v] == 0)
    def _():
        o_ref[...] = jnp.zeros_like(o_ref)

    @pl.when(used_ref[v] != 0)
    def _():
        e, s, nxt = expert_ref[v], slot_ref[v], next_ref[v]

        @pl.when(first_ref[v] != 0)
        def _():
            missing = ready_ref[v]

            @pl.when(missing < n_chunks)
            def _():
                start(e, missing, missing % CHUNKS_PER_TILE)

            def fetch(j, carry):
                @pl.when(j + 1 < n_chunks)
                def _():
                    start(e, j + 1, (j + 1) % CHUNKS_PER_TILE)
                finish(e, j, j % CHUNKS_PER_TILE, s)
                return carry
            lax.fori_loop(missing, n_chunks, fetch, 0)

        ahead = [jnp.minimum(CHUNKS_PER_TILE * step_ref[v] + k, n_chunks - 1)
                 for k in range(CHUNKS_PER_TILE)]
        xb = x_ref[:, :d]

        def hidden_chunk(c):
            cols = slice(c * cw, (c + 1) * cw)
            gate = jnp.dot(xb, wg_buf[s, :, cols], preferred_element_type=F32)
            up = jnp.dot(xb, wu_buf[s, :, cols], preferred_element_type=F32)
            act_ref[:, cols] = (_silu(gate) * up).astype(BF16)

        n_cols = act_ref.shape[1] // cw
        bounds = [n_cols * k // CHUNKS_PER_TILE for k in range(CHUNKS_PER_TILE + 1)]
        for k in range(CHUNKS_PER_TILE):
            start(nxt, ahead[k], k)
        for k in range(CHUNKS_PER_TILE):
            for c in range(bounds[k], bounds[k + 1]):
                hidden_chunk(c)
            if k + 1 < CHUNKS_PER_TILE:
                finish(nxt, ahead[k], k, 1 - s)
        y = jnp.dot(act_ref[...], wd_buf[s], preferred_element_type=F32)
        pieces = x_ref[:, d:].astype(F32)
        weight = pieces[:, 0:1]
        for idx in range(1, GATE_PIECES):
            weight = weight + pieces[:, idx:idx + 1]
        o_ref[...] = (y * weight).astype(BF16)
        finish(nxt, ahead[-1], CHUNKS_PER_TILE - 1, 1 - s)


def _moe_experts(xs, plan, w_gate, w_up, w_down):
    n_rows, width = xs.shape
    n_experts, d, d_ff = w_gate.shape
    n_tiles = n_rows // MOE_ROWS
    in_rows, out_rows = d // WEIGHT_CHUNKS, d_ff // WEIGHT_CHUNKS
    assert in_rows % V7X_BF16_TILE_ROWS == 0 and out_rows % V7X_BF16_TILE_ROWS == 0
    grid_spec = pltpu.PrefetchScalarGridSpec(
        num_scalar_prefetch=len(plan),
        grid=(n_tiles,),
        in_specs=[
            pl.BlockSpec((MOE_ROWS, width), lambda v, blk, *_: (blk[v], 0)),
            pl.BlockSpec(memory_space=pl.ANY),
            pl.BlockSpec(memory_space=pl.ANY),
            pl.BlockSpec(memory_space=pl.ANY),
        ],
        out_specs=pl.BlockSpec((MOE_ROWS, d), lambda v, *_: (v, 0)),
        scratch_shapes=[pltpu.VMEM((2, d, d_ff), BF16),
                        pltpu.VMEM((2, d, d_ff), BF16),
                        pltpu.VMEM((2, d_ff, d), BF16),
                        pltpu.VMEM((CHUNKS_PER_TILE, in_rows, d_ff), F32),
                        pltpu.VMEM((CHUNKS_PER_TILE, in_rows, d_ff), F32),
                        pltpu.VMEM((CHUNKS_PER_TILE, out_rows, d), F32),
                        pltpu.VMEM((MOE_ROWS, d_ff), BF16),
                        pltpu.SemaphoreType.DMA((CHUNKS_PER_TILE, 3))],
    )
    return pl.pallas_call(
        _moe_kernel,
        grid_spec=grid_spec,
        out_shape=jax.ShapeDtypeStruct((n_rows, d), BF16),
        compiler_params=_params("arbitrary"),
        name="moe_experts",
    )(*plan, xs, w_gate, w_up, w_down)


def _plan_tiles(group_tiles, n_tiles, n_chunks):
    n_experts = group_tiles.shape[0]
    ends = jnp.cumsum(group_tiles)
    used_tiles = ends[-1]
    tile = jnp.arange(n_tiles, dtype=jnp.int32)
    used = tile < used_tiles
    block = jnp.minimum(tile, used_tiles - 1)
    expert = jnp.sum(ends[None, :] <= block[:, None], axis=1).astype(jnp.int32)
    expert = jnp.minimum(expert, n_experts - 1)

    idx = jnp.arange(n_experts, dtype=jnp.int32)
    present = group_tiles > 0
    later = present[None, :] & (idx[None, :] > idx[:, None])
    next_expert = jnp.min(jnp.where(later, idx[None, :], n_experts), axis=1)
    next_expert = jnp.where(next_expert == n_experts, idx, next_expert)
    earlier = present[None, :] & (idx[None, :] < idx[:, None])
    prev_expert = jnp.max(jnp.where(earlier, idx[None, :], -1), axis=1)

    def lookup(table, index):
        return jnp.sum(jnp.where(index[:, None] == idx[None, :], table[None, :], 0), axis=1)

    ready = jnp.minimum(CHUNKS_PER_TILE * lookup(group_tiles, prev_expert), n_chunks)
    run_slot = jnp.sum(earlier, axis=1) % 2
    step = block - lookup(ends - group_tiles, expert)
    first = used & (step == 0)
    cast = lambda a: a.astype(jnp.int32)
    return (cast(block), expert, cast(used), cast(first), cast(lookup(run_slot, expert)), cast(step),
            cast(lookup(next_expert, expert)), cast(lookup(ready, expert)))


def _combine_kernel(tab_ref, next_tab_ref, x_ref, route_ref, g_ref, ys_ref, o_ref, buf_ref, sem,
                    *, n_experts):
    i, n = pl.program_id(0), pl.num_programs(0)
    rows = x_ref.shape[0]
    slots = buf_ref.shape[1]
    cur = i % 2

    def run_copy(which):
        def make(local, sorted_start, size):
            return pltpu.make_async_copy(ys_ref.at[pl.ds(sorted_start, size)],
                                         buf_ref.at[which, pl.ds(local, size)], sem.at[which])
        return make

    @pl.when(i == 0)
    def _():
        buf_ref[...] = jnp.zeros_like(buf_ref)
        _for_each_run_piece(tab_ref, n_experts, rows,
                            lambda a, b, size: run_copy(0)(a, b, size).start())

    @pl.when(i + 1 < n)
    def _():
        _for_each_run_piece(next_tab_ref, n_experts, rows,
                            lambda a, b, size: run_copy(1 - cur)(a, b, size).start())

    _wait_tile_runs(tab_ref, n_experts, slots, run_copy(cur))

    route = route_ref[...]
    l1 = route[:, ROUTE_L1:ROUTE_L1 + 1].astype(jnp.int32)
    l2 = route[:, ROUTE_L2:ROUTE_L2 + 1].astype(jnp.int32)
    pos = lax.broadcasted_iota(jnp.int32, (rows, slots), 1)
    pick = ((pos == l1) | (pos == l2)).astype(BF16)
    y = x_ref[...] + jnp.dot(pick, buf_ref[cur], preferred_element_type=F32)
    o_ref[...] = _rms_norm(y, g_ref[...])


def _combine(x2d, route2d, table, ys, final_norm, n_experts):
    t, d = x2d.shape
    n_tiles = t // ROUTE_ROWS
    slots = _slot_rows(n_experts)
    tab_block = (1, 1, TABLE_WIDTH)
    kernel = functools.partial(_combine_kernel, n_experts=n_experts)
    return pl.pallas_call(
        kernel,
        grid=(n_tiles,),
        in_specs=[pl.BlockSpec(tab_block, lambda i: (i, 0, 0), memory_space=pltpu.SMEM),
                  pl.BlockSpec(tab_block, lambda i: (jnp.minimum(i + 1, n_tiles - 1), 0, 0),
                               memory_space=pltpu.SMEM),
                  pl.BlockSpec((ROUTE_ROWS, d), lambda i: (i, 0)),
                  pl.BlockSpec((ROUTE_ROWS, V7X_LANES), lambda i: (i, 0)),
                  pl.BlockSpec((1, d), lambda i: (0, 0)),
                  pl.BlockSpec(memory_space=pl.ANY)],
        out_specs=pl.BlockSpec((ROUTE_ROWS, d), lambda i: (i, 0)),
        out_shape=jax.ShapeDtypeStruct((t, d), F32),
        scratch_shapes=[pltpu.VMEM((2, slots, d), BF16),
                        pltpu.SemaphoreType.DMA((2,))],
        compiler_params=_params("arbitrary"),
        name="moe_combine",
    )(table, table, x2d, route2d, final_norm, ys)


def kernel(x, attn_norm, attn_w_qkv, attn_sinks, attn_w_o, ffn_norm, ffn_w_gate, ffn_w_up,
           ffn_w_down, pool_norm, pool_w, pool_scale, moe_norm, moe_w_router, moe_w_gate, moe_w_up,
           moe_w_down, final_norm):
    b, s, d = x.shape
    t = b * s
    n_experts = moe_w_router.shape[-1]
    assert s % ATTN_ROWS == 0 and s % ROUTE_ROWS == 0
    assert n_experts <= V7X_SUBLANES and TABLE_FIELDS * n_experts + 2 <= TABLE_WIDTH
    assert attn_norm.shape[0] == 1 and pool_norm.shape[0] == 1

    qkv = _qkv_proj(x.reshape(t, d), attn_norm[0][None], attn_w_qkv[0].astype(BF16))
    x1 = _attention(x, qkv.reshape(b, s, -1), attn_sinks[0].astype(F32), attn_w_o[0].astype(BF16))
    w_router_t = jnp.pad(moe_w_router[0].T, ((0, ROUTE_SUBLANES - n_experts), (0, 0)))
    x3, h3, route, route_t, info = _ffn_pool_route(
        x1.reshape(t, d), s, ffn_norm[0][None], ffn_w_gate[0].astype(BF16), ffn_w_up[0].astype(BF16),
        ffn_w_down[0].astype(BF16), pool_norm[0][None], pool_scale[0][None], moe_norm[0][None],
        pool_w[0].astype(BF16), w_router_t, n_experts)

    n_tiles = t // ROUTE_ROWS
    info = info[:, :n_experts, :].astype(jnp.int32)
    base, padded, lstart = info[..., INFO_BASE], info[..., INFO_COUNT], info[..., INFO_LSTART]
    group_rows = jnp.sum(padded, axis=0)
    group_tiles = -(-group_rows // MOE_ROWS)
    group_start = (jnp.cumsum(group_tiles) - group_tiles) * MOE_ROWS
    max_rows = (t * TOP_K + n_tiles * n_experts * (RUN_ALIGN - 1)
                + n_experts * (MOE_ROWS - RUN_ALIGN))
    n_rows = -(-max_rows // MOE_ROWS) * MOE_ROWS
    per_tile = lambda v: jnp.broadcast_to(v[None, :], (n_tiles, n_experts))
    table = jnp.concatenate(
        [lstart, group_start[None, :] + base, padded,
         per_tile(group_start + group_rows), per_tile(group_tiles * MOE_ROWS - group_rows),
         jnp.sum(padded, axis=1, keepdims=True),
         jnp.broadcast_to(jnp.sum(group_tiles) * MOE_ROWS, (n_tiles, 1)),
         jnp.zeros((n_tiles, TABLE_WIDTH - TABLE_FIELDS * n_experts - 2), jnp.int32)], axis=1)
    table = table.reshape(n_tiles, 1, TABLE_WIDTH)

    xs = _dispatch(h3, route_t, table, n_rows, n_experts)
    plan = _plan_tiles(group_tiles, n_rows // MOE_ROWS, WEIGHT_CHUNKS)
    ys = _moe_experts(xs, plan, moe_w_gate[0], moe_w_up[0], moe_w_down[0])
    out = _combine(x3, route, table, ys, final_norm[None],
                   n_experts)
    return out.reshape(b, s, d)
```

```python
import functools
import itertools

import jax
import jax.numpy as jnp
import numpy as np
from jax import lax
from jax.experimental import pallas as pl
from jax.experimental.pallas import tpu as pltpu

F32 = jnp.float32
BF16 = jnp.bfloat16

HEAD_DIM = 64
N_KV_HEADS = 4
WINDOW = 128
POOL_WINDOWS = (2, 4, 8, 16)
TOP_K = 2
RMS_EPS = 1e-5
NEG_INF = -1e30

V7X_LANES = 128
V7X_SUBLANES = 8
V7X_BF16_TILE_ROWS = 16
V7X_MXU_DIM = 256
V7X_VMEM_LIMIT_BYTES = 60 * 1024 * 1024

QKV_ROWS = 2048
ATTN_ROWS = 2048
ROUTE_ROWS = 512
MOE_ROWS = 512
POOL_HALO = 16

RUN_ALIGN = V7X_BF16_TILE_ROWS
RUN_ALIGN_BITS = RUN_ALIGN.bit_length() - 1
GATE_PIECES = 3
ROUTE_SUBLANES = 16
ROUTE_L1, ROUTE_L2, ROUTE_G1, ROUTE_G2, ROUTE_E1, ROUTE_E2 = range(6)
INFO_BASE, INFO_COUNT, INFO_LSTART = range(3)
TABLE_FIELDS = 5
TABLE_WIDTH = 64


def _params(*semantics):
    return pltpu.CompilerParams(dimension_semantics=semantics,
                                vmem_limit_bytes=V7X_VMEM_LIMIT_BYTES)


def _rms_norm(x, gain):
    ms = jnp.mean(x * x, axis=-1, keepdims=True)
    return x * lax.rsqrt(ms + RMS_EPS) * gain


def _silu(x):
    return x / (1.0 + jnp.exp(-x))


def _split_bf16(x, pieces):
    out = []
    for _ in range(pieces - 1):
        top = x.astype(BF16)
        out.append(top)
        x = x - top.astype(F32)
    out.append(x.astype(BF16))
    return out


CONTRACT_LAST = (((1,), (1,)), ((), ()))
CONTRACT_FIRST = (((0,), (0,)), ((), ()))


def _qkv_kernel(x_ref, g_ref, w_ref, o_ref):
    h = _rms_norm(x_ref[...], g_ref[...]).astype(BF16)
    o_ref[...] = jnp.dot(h, w_ref[...], preferred_element_type=F32).astype(BF16)


def _qkv_proj(x2d, gain, w_qkv):
    t, d = x2d.shape
    n = w_qkv.shape[1]
    return pl.pallas_call(
        _qkv_kernel,
        grid=(t // QKV_ROWS,),
        in_specs=[pl.BlockSpec((QKV_ROWS, d), lambda i: (i, 0)),
                  pl.BlockSpec((1, d), lambda i: (0, 0)),
                  pl.BlockSpec((d, n), lambda i: (0, 0))],
        out_specs=pl.BlockSpec((QKV_ROWS, n), lambda i: (i, 0)),
        out_shape=jax.ShapeDtypeStruct((t, n), BF16),
        compiler_params=_params("parallel"),
        name="rms_qkv",
    )(x2d, gain, w_qkv)


def _attn_kernel(bias_ref, sink_ref, q_ref, kc_ref, kp_ref, vc_ref, vp_ref, x_ref, wo_ref, o_ref,
                 ot_ref, *, n_heads):
    group = n_heads // N_KV_HEADS
    gw = group * WINDOW
    j = pl.program_id(1)
    key = lax.broadcasted_iota(jnp.int32, (WINDOW, n_heads * WINDOW), 0)
    query = lax.broadcasted_iota(jnp.int32, (WINDOW, n_heads * WINDOW), 1) & (WINDOW - 1)
    from_prev = key > query
    from_prev_bf = from_prev.astype(BF16)
    no_prev = from_prev & (j == 0)
    sink = sink_ref[...]

    n_blocks = q_ref.shape[1] // WINDOW

    def score_stage(blk):
        rows = slice(blk * WINDOW, (blk + 1) * WINDOW)
        q_blk = q_ref[0, rows, :] * jnp.asarray(HEAD_DIM ** -0.5, BF16)
        if blk == 0:
            k_prev, v_prev = kp_ref[0], vp_ref[0]
        else:
            prev = slice((blk - 1) * WINDOW, blk * WINDOW)
            k_prev, v_prev = kc_ref[0, prev, :], vc_ref[0, prev, :]
        k_cat = jnp.concatenate([k_prev, kc_ref[0, rows, :]], axis=0)
        v_cat = jnp.concatenate([v_prev, vc_ref[0, rows, :]], axis=0)
        scores = []
        for kvh in range(N_KV_HEADS):
            q_g = jnp.concatenate([q_blk[:, h * HEAD_DIM:(h + 1) * HEAD_DIM]
                                   for h in range(kvh * group, (kvh + 1) * group)], axis=0)
            k_h = k_cat[:, kvh * HEAD_DIM:(kvh + 1) * HEAD_DIM]
            scores.append(lax.dot_general(k_h, q_g, CONTRACT_LAST, preferred_element_type=F32))
        return jnp.concatenate(scores, axis=1), v_cat

    def output_stage(blk, s_all, v_cat):
        rows = slice(blk * WINDOW, (blk + 1) * WINDOW)
        s = jnp.where(from_prev, s_all[:WINDOW], s_all[WINDOW:]) + bias_ref[...]
        if blk == 0:
            s = jnp.where(no_prev, NEG_INF, s)
        m = jnp.maximum(jnp.max(s, axis=0, keepdims=True), sink)
        e = jnp.exp(s - m)
        inv_denom = 1.0 / (jnp.sum(e, axis=0, keepdims=True) + jnp.exp(sink - m))
        e_bf = e.astype(BF16)
        p_prev = e_bf * from_prev_bf
        p_cat = jnp.concatenate([p_prev, e_bf - p_prev], axis=0)
        for kvh in range(N_KV_HEADS):
            lanes = slice(kvh * gw, (kvh + 1) * gw)
            v_h = v_cat[:, kvh * HEAD_DIM:(kvh + 1) * HEAD_DIM]
            o_t = lax.dot_general(v_h, p_cat[:, lanes], CONTRACT_FIRST,
                                  preferred_element_type=F32)
            o_t = (o_t * inv_denom[:, lanes]).astype(BF16)
            for g in range(group):
                h = kvh * group + g
                ot_ref[h * HEAD_DIM:(h + 1) * HEAD_DIM, rows] = o_t[:, g * WINDOW:(g + 1) * WINDOW]

    def project(first_blk, last_blk):
        rows = slice(first_blk * WINDOW, (last_blk + 1) * WINDOW)
        o_ref[0, rows, :] = x_ref[0, rows, :] + lax.dot_general(
            ot_ref[:, rows], wo_ref[...], CONTRACT_FIRST, preferred_element_type=F32)

    staged = score_stage(0)
    for blk in range(n_blocks):
        ahead = score_stage(blk + 1) if blk + 1 < n_blocks else None
        if blk % 2 == 0 and blk >= 2:
            project(blk - 2, blk - 1)
        output_stage(blk, *staged)
        staged = ahead
    project(n_blocks - 2 + n_blocks % 2, n_blocks - 1)


def _alibi_band_bias(n_heads):
    slopes = np.exp2(-8.0 * np.arange(1, n_heads + 1, dtype=np.float64) / n_heads)
    c = np.arange(WINDOW)[:, None, None]
    r = np.arange(WINDOW)[None, None, :]
    dist = ((r - c) % WINDOW).astype(np.float64)
    bias = (-slopes[None, :, None] * dist).reshape(WINDOW, n_heads * WINDOW)
    return jnp.asarray(bias.astype(np.float32))


def _attention(x, qkv, sinks, w_o):
    b, s, d = x.shape
    n_heads = w_o.shape[0] // HEAD_DIM
    q_width = n_heads * HEAD_DIM
    kv_width = N_KV_HEADS * HEAD_DIM
    k_col = q_width // kv_width
    v_col = k_col + 1
    blocks_per_step = ATTN_ROWS // WINDOW
    sink_row = jnp.repeat(sinks, WINDOW)[None, :]

    def prev_block(bi, j):
        return jnp.maximum(j * blocks_per_step - 1, 0)

    kernel = functools.partial(_attn_kernel, n_heads=n_heads)
    return pl.pallas_call(
        kernel,
        grid=(b, s // ATTN_ROWS),
        in_specs=[
            pl.BlockSpec((WINDOW, n_heads * WINDOW), lambda bi, j: (0, 0)),
            pl.BlockSpec((1, n_heads * WINDOW), lambda bi, j: (0, 0)),
            pl.BlockSpec((1, ATTN_ROWS, q_width), lambda bi, j: (bi, j, 0)),
            pl.BlockSpec((1, ATTN_ROWS, kv_width), lambda bi, j: (bi, j, k_col)),
            pl.BlockSpec((1, WINDOW, kv_width), lambda bi, j: (bi, prev_block(bi, j), k_col)),
            pl.BlockSpec((1, ATTN_ROWS, kv_width), lambda bi, j: (bi, j, v_col)),
            pl.BlockSpec((1, WINDOW, kv_width), lambda bi, j: (bi, prev_block(bi, j), v_col)),
            pl.BlockSpec((1, ATTN_ROWS, d), lambda bi, j: (bi, j, 0)),
            pl.BlockSpec((q_width, d), lambda bi, j: (0, 0)),
        ],
        out_specs=pl.BlockSpec((1, ATTN_ROWS, d), lambda bi, j: (bi, j, 0)),
        out_shape=jax.ShapeDtypeStruct((b, s, d), F32),
        scratch_shapes=[pltpu.VMEM((q_width, ATTN_ROWS), BF16)],
        compiler_params=_params("parallel", "parallel"),
        name="swa_attention",
    )(_alibi_band_bias(n_heads), sink_row, qkv, qkv, qkv, qkv, qkv, x, w_o)


def _ffn_pool_route_kernel(x_ref, fg_ref, wg_ref, wu_ref, wd_ref, pn_ref, ps_ref, mn_ref, wp_ref, wrt_ref,
                           x3_ref, h3_ref, route_ref, routet_ref, info_ref,
                           act_ref, x2_ref, halo_ref, running_ref, *, n_experts, tiles_per_seq):
    i = pl.program_id(0)
    rows = x_ref.shape[0]

    @pl.when(i == 0)
    def _():
        x2_ref[...] = jnp.zeros_like(x2_ref)
        halo_ref[...] = jnp.zeros_like(halo_ref)
        running_ref[...] = jnp.zeros_like(running_ref)

    x = x_ref[...]
    hf = _rms_norm(x, fg_ref[...]).astype(BF16)
    pending = iter(range(wg_ref.shape[1] // V7X_MXU_DIM))

    def swiglu_chunks(n):
        for c in itertools.islice(pending, n):
            cols = slice(c * V7X_MXU_DIM, (c + 1) * V7X_MXU_DIM)
            gate = jnp.dot(hf, wg_ref[:, cols], preferred_element_type=F32)
            up = jnp.dot(hf, wu_ref[:, cols], preferred_element_type=F32)
            act_ref[:, cols] = (_silu(gate) * up).astype(BF16)

    live = i > 0
    j = jnp.maximum(i - 1, 0) % tiles_per_seq
    xp = x2_ref[(i + 1) % 2]
    h = _rms_norm(xp, pn_ref[...])
    halo = jnp.where(j > 0, halo_ref[...], 0.0)
    halo_ref[...] = h[rows - POOL_HALO:, :]
    y = jnp.concatenate([halo, h], axis=0)
    pos = j * rows + lax.broadcasted_iota(jnp.int32, (rows, 1), 0)
    gdim = wp_ref.shape[1]
    mixed = []
    swiglu_chunks(2)
    for g, w in enumerate(POOL_WINDOWS):
        swiglu_chunks(2)
        cols = slice(g * gdim, (g + 1) * gdim)
        acc = y[:, cols]
        span = 1
        while span < w:
            acc = acc + pltpu.roll(acc, span, axis=0)
            span *= 2
        count = jnp.minimum(pos + 1, w).astype(F32)
        p = acc[POOL_HALO:, :] / count - h[:, cols]
        mixed.append(jnp.dot(p.astype(BF16), wp_ref[g], preferred_element_type=F32))
    swiglu_chunks(wg_ref.shape[1] // V7X_MXU_DIM)
    x3 = xp + jnp.concatenate(mixed, axis=-1) * ps_ref[...]
    x3_ref[...] = x3
    x2_ref[i % 2] = x + jnp.dot(act_ref[...], wd_ref[...], preferred_element_type=F32)

    h_hi, h_mid, h_lo = _split_bf16(_rms_norm(x3, mn_ref[...]), 3)
    h3_ref[...] = h_hi
    w_hi, w_mid, w_lo = _split_bf16(wrt_ref[...], 3)
    es = ROUTE_SUBLANES
    a = lax.dot_general(jnp.concatenate([w_hi, w_mid, w_lo], axis=0), h_hi, CONTRACT_LAST,
                        preferred_element_type=F32)
    b = lax.dot_general(jnp.concatenate([w_hi, w_mid], axis=0), h_mid, CONTRACT_LAST,
                        preferred_element_type=F32)
    c = lax.dot_general(w_hi, h_lo, CONTRACT_LAST, preferred_element_type=F32)
    logits = ((c + b[es:] + a[2 * es:]) + (b[:es] + a[es:2 * es])) + a[:es]

    eidx = lax.broadcasted_iota(jnp.int32, (es, rows), 0).astype(F32)
    logits = jnp.where(eidx < n_experts, logits, -jnp.inf)
    v1 = jnp.max(logits, axis=0, keepdims=True)
    e1 = jnp.min(jnp.where(logits == v1, eidx, float(es)), axis=0, keepdims=True)
    rest = jnp.where(eidx == e1, -jnp.inf, logits)
    v2 = jnp.max(rest, axis=0, keepdims=True)
    e2 = jnp.min(jnp.where(rest == v2, eidx, float(es)), axis=0, keepdims=True)
    ex = jnp.exp(v2 - v1)
    g1 = 1.0 / (1.0 + ex)
    g2 = ex / (1.0 + ex)

    chosen = ((eidx == e1) | (eidx == e2)).astype(F32)
    t_i = lax.broadcasted_iota(jnp.int32, (rows, rows), 0)
    t_j = lax.broadcasted_iota(jnp.int32, (rows, rows), 1)
    earlier = (t_i < t_j).astype(BF16)
    local_rank = jnp.dot(chosen.astype(BF16), earlier, preferred_element_type=F32)
    count = jnp.sum(chosen, axis=1, keepdims=True)
    padded = jnp.floor((count + (RUN_ALIGN - 1)) * (1.0 / RUN_ALIGN)) * RUN_ALIGN
    ecol = lax.broadcasted_iota(jnp.int32, (es, 1), 0)
    lstart = jnp.zeros((es, 1), F32)
    for e in range(n_experts):
        lstart = lstart + jnp.where(ecol > e, padded[e:e + 1, :], 0.0)
    lpos = lstart + local_rank
    l1 = jnp.sum(jnp.where(eidx == e1, lpos, 0.0), axis=0, keepdims=True)
    l2 = jnp.sum(jnp.where(eidx == e2, lpos, 0.0), axis=0, keepdims=True)

    base = running_ref[...]
    running_ref[...] = base + jnp.where(live, padded, 0.0)
    lane = lax.broadcasted_iota(jnp.int32, (es, V7X_LANES), 1)
    info_ref[0] = jnp.where(lane == INFO_BASE, base,
                            jnp.where(lane == INFO_COUNT, padded,
                                      jnp.where(lane == INFO_LSTART, lstart, 0.0)))

    rec = jnp.zeros((es, rows), F32)
    for idx, val in ((ROUTE_L1, l1), (ROUTE_L2, l2), (ROUTE_G1, g1), (ROUTE_G2, g2),
                     (ROUTE_E1, e1), (ROUTE_E2, e2)):
        rec = jnp.where(eidx == idx, val, rec)
    routet_ref[0] = rec
    rec_full = jnp.concatenate([rec, jnp.zeros((V7X_LANES - es, rows), F32)], axis=0)
    route_ref[...] = rec_full.T


def _ffn_pool_route(x2d, seq_len, ffn_gain, w_gate, w_up, w_down, pool_norm, pool_scale, moe_norm,
                    w_pool, w_router_t, n_experts):
    t, d = x2d.shape
    d_ff = w_gate.shape[1]
    rows = ROUTE_ROWS
    n_tiles = t // rows
    es = ROUTE_SUBLANES
    resident = dict(pipeline_mode=pl.Buffered(1))
    cur = lambda i: (jnp.minimum(i, n_tiles - 1), 0)
    prev = lambda i: (jnp.maximum(i - 1, 0), 0)
    prev3 = lambda i: (jnp.maximum(i - 1, 0), 0, 0)
    vec = pl.BlockSpec((1, d), lambda i: (0, 0))
    kernel = functools.partial(_ffn_pool_route_kernel, n_experts=n_experts,
                               tiles_per_seq=seq_len // rows)
    return pl.pallas_call(
        kernel,
        grid=(n_tiles + 1,),
        in_specs=[pl.BlockSpec((rows, d), cur),
                  vec,
                  pl.BlockSpec((d, d_ff), lambda i: (0, 0), **resident),
                  pl.BlockSpec((d, d_ff), lambda i: (0, 0), **resident),
                  pl.BlockSpec((d_ff, d), lambda i: (0, 0), **resident),
                  vec, vec, vec,
                  pl.BlockSpec(w_pool.shape, lambda i: (0, 0, 0)),
                  pl.BlockSpec(w_router_t.shape, lambda i: (0, 0))],
        out_specs=[pl.BlockSpec((rows, d), prev),
                   pl.BlockSpec((rows, d), prev),
                   pl.BlockSpec((rows, V7X_LANES), prev),
                   pl.BlockSpec((1, es, rows), prev3),
                   pl.BlockSpec((1, es, V7X_LANES), prev3)],
        out_shape=[jax.ShapeDtypeStruct((t, d), F32),
                   jax.ShapeDtypeStruct((t, d), BF16),
                   jax.ShapeDtypeStruct((t, V7X_LANES), F32),
                   jax.ShapeDtypeStruct((n_tiles, es, rows), F32),
                   jax.ShapeDtypeStruct((n_tiles, es, V7X_LANES), F32)],
        scratch_shapes=[pltpu.VMEM((rows, d_ff), BF16),
                        pltpu.VMEM((2, rows, d), F32),
                        pltpu.VMEM((POOL_HALO, d), F32),
                        pltpu.VMEM((es, V7X_LANES), F32)],
        compiler_params=_params("arbitrary"),
        name="swiglu_pool_route",
    )(x2d, ffn_gain, w_gate, w_up, w_down, pool_norm, pool_scale, moe_norm, w_pool, w_router_t)


def _chunks(count, lo_bit, hi_bit):
    for bit in range(lo_bit, hi_bit):
        size = 1 << bit
        offset = (count >> (bit + 1)) << (bit + 1)
        yield offset, size, (count & size) != 0


def _for_each_run_piece(tab_ref, n_experts, max_rows, fn):
    for e in range(n_experts):
        local = tab_ref[0, 0, e]
        sorted_start = tab_ref[0, 0, n_experts + e]
        count = tab_ref[0, 0, 2 * n_experts + e]
        for offset, size, present in _chunks(count, RUN_ALIGN_BITS, max_rows.bit_length()):
            @pl.when(present)
            def _(offset=offset, size=size, local=local, sorted_start=sorted_start):
                fn(pl.multiple_of(local + offset, RUN_ALIGN),
                   pl.multiple_of(sorted_start + offset, RUN_ALIGN), size)


def _wait_tile_runs(tab_ref, n_experts, max_rows, make_copy):
    total = tab_ref[0, 0, TABLE_FIELDS * n_experts]
    for _, size, present in _chunks(total, RUN_ALIGN_BITS, max_rows.bit_length()):
        @pl.when(present)
        def _(size=size):
            make_copy(0, 0, size).wait()


def _slot_rows(n_experts):
    pad = n_experts * (RUN_ALIGN - 1)
    return TOP_K * ROUTE_ROWS + (-(-pad // ROUTE_SUBLANES)) * ROUTE_SUBLANES


ZERO_ROWS = MOE_ROWS // 2


def _dispatch_kernel(tab_ref, prev_tab_ref, h_ref, rt_ref, xs_ref, buf_ref, zero_ref, sem, fill_sem,
                     *, n_experts, max_tail):
    i, n = pl.program_id(0), pl.num_programs(0)
    rows, d = h_ref.shape
    slots = buf_ref.shape[1]
    cur = i % 2

    rt = rt_ref[0]
    l1 = rt[ROUTE_L1:ROUTE_L1 + 1].astype(jnp.int32)
    l2 = rt[ROUTE_L2:ROUTE_L2 + 1].astype(jnp.int32)
    g1 = rt[ROUTE_G1:ROUTE_G1 + 1]
    g2 = rt[ROUTE_G2:ROUTE_G2 + 1]
    pos = lax.broadcasted_iota(jnp.int32, (slots, rows), 0)
    first = pos == l1
    second = pos == l2
    perm = (first | second).astype(BF16)
    x_sorted = jnp.dot(perm, h_ref[...], preferred_element_type=F32)
    gate = jnp.sum(jnp.where(first, g1, 0.0) + jnp.where(second, g2, 0.0), axis=1, keepdims=True)
    lane = lax.broadcasted_iota(jnp.int32, (slots, V7X_LANES), 1)
    gate_lanes = jnp.zeros((slots, V7X_LANES), F32)
    for idx, piece in enumerate(_split_bf16(gate, GATE_PIECES)):
        gate_lanes = jnp.where(lane == idx, piece.astype(F32), gate_lanes)
    buf_ref[cur] = jnp.concatenate([x_sorted, gate_lanes], axis=1).astype(BF16)

    def run_copy(which):
        def make(local, sorted_start, size):
            return pltpu.make_async_copy(buf_ref.at[which, pl.ds(local, size)],
                                         xs_ref.at[pl.ds(sorted_start, size)], sem.at[which])
        return make

    _for_each_run_piece(tab_ref, n_experts, rows,
                        lambda a, b, size: run_copy(cur)(a, b, size).start())

    @pl.when(i > 0)
    def _():
        _wait_tile_runs(prev_tab_ref, n_experts, slots, run_copy(1 - cur))

    @pl.when(i == n - 1)
    def _():
        _wait_tile_runs(tab_ref, n_experts, slots, run_copy(cur))
        zero_ref[...] = jnp.zeros_like(zero_ref)

        def fill_pieces(fn):
            for e in range(n_experts):
                start = tab_ref[0, 0, 3 * n_experts + e]
                length = tab_ref[0, 0, 4 * n_experts + e]
                for offset, size, present in _chunks(length, RUN_ALIGN_BITS, ZERO_ROWS.bit_length()):
                    @pl.when(present)
                    def _(offset=offset, size=size, start=start):
                        fn(pltpu.make_async_copy(
                            zero_ref.at[pl.ds(0, size)],
                            xs_ref.at[pl.ds(pl.multiple_of(start + offset, RUN_ALIGN), size)],
                            fill_sem))
            used_rows = tab_ref[0, 0, TABLE_FIELDS * n_experts + 1]
            for k in range(0, max_tail, ZERO_ROWS):
                @pl.when(used_rows + k < xs_ref.shape[0])
                def _(k=k):
                    fn(pltpu.make_async_copy(
                        zero_ref,
                        xs_ref.at[pl.ds(pl.multiple_of(used_rows + k, RUN_ALIGN), ZERO_ROWS)],
                        fill_sem))

        fill_pieces(lambda c: c.start())
        fill_pieces(lambda c: c.wait())


def _dispatch(h2d, route_t, table, n_rows, n_experts):
    t, d = h2d.shape
    n_tiles = t // ROUTE_ROWS
    slots = _slot_rows(n_experts)
    width = d + V7X_LANES
    tab_block = (1, 1, TABLE_WIDTH)
    kernel = functools.partial(_dispatch_kernel, n_experts=n_experts, max_tail=n_rows - TOP_K * t)
    return pl.pallas_call(
        kernel,
        grid=(n_tiles,),
        in_specs=[pl.BlockSpec(tab_block, lambda i: (i, 0, 0), memory_space=pltpu.SMEM),
                  pl.BlockSpec(tab_block, lambda i: (jnp.maximum(i - 1, 0), 0, 0),
                               memory_space=pltpu.SMEM),
                  pl.BlockSpec((ROUTE_ROWS, d), lambda i: (i, 0)),
                  pl.BlockSpec((1, ROUTE_SUBLANES, ROUTE_ROWS), lambda i: (i, 0, 0))],
        out_specs=pl.BlockSpec(memory_space=pl.ANY),
        out_shape=jax.ShapeDtypeStruct((n_rows, width), BF16),
        scratch_shapes=[pltpu.VMEM((2, slots, width), BF16),
                        pltpu.VMEM((ZERO_ROWS, width), BF16),
                        pltpu.SemaphoreType.DMA((2,)),
                        pltpu.SemaphoreType.DMA(())],
        compiler_params=_params("arbitrary"),
        name="moe_dispatch",
    )(table, table, h2d, route_t)


WEIGHT_CHUNKS = 16
CHUNKS_PER_TILE = 2


def _moe_kernel(block_ref, expert_ref, used_ref, first_ref, slot_ref, step_ref, next_ref, ready_ref,
                half_ref, x_ref, wg_hbm, wu_hbm, wd_hbm, o_ref,
                wg_buf, wu_buf, wd_buf, stage_g, stage_u, stage_d, act_ref, sem):
    v = pl.program_id(0)
    d = o_ref.shape[1]
    n_chunks = WEIGHT_CHUNKS
    cw = V7X_MXU_DIM
    in_rows = stage_g.shape[1]
    out_rows = stage_d.shape[1]

    def blocks(j):
        return (pl.ds(pl.multiple_of(j * in_rows, in_rows), in_rows),
                pl.ds(pl.multiple_of(j * out_rows, out_rows), out_rows))

    def chunk_copies(e, j, k):
        rin, rout = blocks(j)
        return (pltpu.make_async_copy(wg_hbm.at[e, rin, :], stage_g.at[k], sem.at[k, 0]),
                pltpu.make_async_copy(wu_hbm.at[e, rin, :], stage_u.at[k], sem.at[k, 1]),
                pltpu.make_async_copy(wd_hbm.at[e, rout, :], stage_d.at[k], sem.at[k, 2]))

    def start(e, j, k):
        for copy in chunk_copies(e, j, k):
            copy.start()

    def finish(e, j, k, s):
        for copy in chunk_copies(e, j, k):
            copy.wait()
        rin, rout = blocks(j)
        wg_buf[s, rin, :] = stage_g[k].astype(BF16)
        wu_buf[s, rin, :] = stage_u[k].astype(BF16)
        wd_buf[s, rout, :] = stage_d[k].astype(BF16)

    @pl.when(used_ref[v] == 0)
    def _():
        o_ref[...] = jnp.zeros_like(o_ref)

    @pl.when(used_ref[v] != 0)
    def _():
        e, s, nxt = expert_ref[v], slot_ref[v], next_ref[v]

        @pl.when(first_ref[v] != 0)
        def _():
            missing = ready_ref[v]

            @pl.when(missing < n_chunks)
            def _():
                start(e, missing, missing % CHUNKS_PER_TILE)

            def fetch(j, carry):
                @pl.when(j + 1 < n_chunks)
                def _():
                    start(e, j + 1, (j + 1) % CHUNKS_PER_TILE)
                finish(e, j, j % CHUNKS_PER_TILE, s)
                return carry
            lax.fori_loop(missing, n_chunks, fetch, 0)

        ahead = [jnp.minimum(CHUNKS_PER_TILE * step_ref[v] + k, n_chunks - 1)
                 for k in range(CHUNKS_PER_TILE)]

        def tile_body(nrows):
            xb = x_ref[:nrows, :d]

            def hidden_chunk(c):
                cols = slice(c * cw, (c + 1) * cw)
                gate = jnp.dot(xb, wg_buf[s, :, cols], preferred_element_type=F32)
                up = jnp.dot(xb, wu_buf[s, :, cols], preferred_element_type=F32)
                act_ref[:nrows, cols] = (_silu(gate) * up).astype(BF16)

            n_cols = act_ref.shape[1] // cw
            bounds = [n_cols * k // CHUNKS_PER_TILE for k in range(CHUNKS_PER_TILE + 1)]
            for k in range(CHUNKS_PER_TILE):
                start(nxt, ahead[k], k)
            for k in range(CHUNKS_PER_TILE):
                for c in range(bounds[k], bounds[k + 1]):
                    hidden_chunk(c)
                if k + 1 < CHUNKS_PER_TILE:
                    finish(nxt, ahead[k], k, 1 - s)
            y = jnp.dot(act_ref[:nrows, :], wd_buf[s], preferred_element_type=F32)
            pieces = x_ref[:nrows, d:].astype(F32)
            weight = pieces[:, 0:1]
            for idx in range(1, GATE_PIECES):
                weight = weight + pieces[:, idx:idx + 1]
            o_ref[:nrows, :] = (y * weight).astype(BF16)
            if nrows < o_ref.shape[0]:
                o_ref[nrows:, :] = jnp.zeros((o_ref.shape[0] - nrows, d), BF16)
            finish(nxt, ahead[-1], CHUNKS_PER_TILE - 1, 1 - s)

        @pl.when(half_ref[v] == 0)
        def _():
            tile_body(o_ref.shape[0])

        @pl.when(half_ref[v] != 0)
        def _():
            tile_body(o_ref.shape[0] // 2)


def _moe_experts(xs, plan, w_gate, w_up, w_down):
    n_rows, width = xs.shape
    n_experts, d, d_ff = w_gate.shape
    n_tiles = n_rows // MOE_ROWS
    in_rows, out_rows = d // WEIGHT_CHUNKS, d_ff // WEIGHT_CHUNKS
    assert in_rows % V7X_BF16_TILE_ROWS == 0 and out_rows % V7X_BF16_TILE_ROWS == 0
    grid_spec = pltpu.PrefetchScalarGridSpec(
        num_scalar_prefetch=len(plan),
        grid=(n_tiles,),
        in_specs=[
            pl.BlockSpec((MOE_ROWS, width), lambda v, blk, *_: (blk[v], 0)),
            pl.BlockSpec(memory_space=pl.ANY),
            pl.BlockSpec(memory_space=pl.ANY),
            pl.BlockSpec(memory_space=pl.ANY),
        ],
        out_specs=pl.BlockSpec((MOE_ROWS, d), lambda v, *_: (v, 0)),
        scratch_shapes=[pltpu.VMEM((2, d, d_ff), BF16),
                        pltpu.VMEM((2, d, d_ff), BF16),
                        pltpu.VMEM((2, d_ff, d), BF16),
                        pltpu.VMEM((CHUNKS_PER_TILE, in_rows, d_ff), F32),
                        pltpu.VMEM((CHUNKS_PER_TILE, in_rows, d_ff), F32),
                        pltpu.VMEM((CHUNKS_PER_TILE, out_rows, d), F32),
                        pltpu.VMEM((MOE_ROWS, d_ff), BF16),
                        pltpu.SemaphoreType.DMA((CHUNKS_PER_TILE, 3))],
    )
    return pl.pallas_call(
        _moe_kernel,
        grid_spec=grid_spec,
        out_shape=jax.ShapeDtypeStruct((n_rows, d), BF16),
        compiler_params=_params("arbitrary"),
        name="moe_experts",
    )(*plan, xs, w_gate, w_up, w_down)


def _plan_tiles(group_rows, group_tiles, n_tiles, n_chunks):
    n_experts = group_tiles.shape[0]
    ends = jnp.cumsum(group_tiles)
    used_tiles = ends[-1]
    tile = jnp.arange(n_tiles, dtype=jnp.int32)
    used = tile < used_tiles
    block = jnp.minimum(tile, used_tiles - 1)
    expert = jnp.sum(ends[None, :] <= block[:, None], axis=1).astype(jnp.int32)
    expert = jnp.minimum(expert, n_experts - 1)

    idx = jnp.arange(n_experts, dtype=jnp.int32)
    present = group_tiles > 0
    later = present[None, :] & (idx[None, :] > idx[:, None])
    next_expert = jnp.min(jnp.where(later, idx[None, :], n_experts), axis=1)
    next_expert = jnp.where(next_expert == n_experts, idx, next_expert)
    earlier = present[None, :] & (idx[None, :] < idx[:, None])
    prev_expert = jnp.max(jnp.where(earlier, idx[None, :], -1), axis=1)

    def lookup(table, index):
        return jnp.sum(jnp.where(index[:, None] == idx[None, :], table[None, :], 0), axis=1)

    ready = jnp.minimum(CHUNKS_PER_TILE * lookup(group_tiles, prev_expert), n_chunks)
    run_slot = jnp.sum(earlier, axis=1) % 2
    step = block - lookup(ends - group_tiles, expert)
    first = used & (step == 0)
    tail_rows = group_rows - (group_tiles - 1) * MOE_ROWS
    half = used & (block == lookup(ends, expert) - 1) & (lookup(tail_rows, expert) <= MOE_ROWS // 2)
    cast = lambda a: a.astype(jnp.int32)
    return (cast(block), expert, cast(used), cast(first), cast(lookup(run_slot, expert)), cast(step),
            cast(lookup(next_expert, expert)), cast(lookup(ready, expert)), cast(half))


def _combine_kernel(tab_ref, next_tab_ref, x_ref, route_ref, g_ref, ys_ref, o_ref, buf_ref, sem,
                    *, n_experts):
    i, n = pl.program_id(0), pl.num_programs(0)
    rows = x_ref.shape[0]
    slots = buf_ref.shape[1]
    cur = i % 2

    def run_copy(which):
        def make(local, sorted_start, size):
            return pltpu.make_async_copy(ys_ref.at[pl.ds(sorted_start, size)],
                                         buf_ref.at[which, pl.ds(local, size)], sem.at[which])
        return make

    @pl.when(i == 0)
    def _():
        buf_ref[...] = jnp.zeros_like(buf_ref)
        _for_each_run_piece(tab_ref, n_experts, rows,
                            lambda a, b, size: run_copy(0)(a, b, size).start())

    @pl.when(i + 1 < n)
    def _():
        _for_each_run_piece(next_tab_ref, n_experts, rows,
                            lambda a, b, size: run_copy(1 - cur)(a, b, size).start())

    _wait_tile_runs(tab_ref, n_experts, slots, run_copy(cur))

    route = route_ref[...]
    l1 = route[:, ROUTE_L1:ROUTE_L1 + 1].astype(jnp.int32)
    l2 = route[:, ROUTE_L2:ROUTE_L2 + 1].astype(jnp.int32)
    pos = lax.broadcasted_iota(jnp.int32, (rows, slots), 1)
    pick = ((pos == l1) | (pos == l2)).astype(BF16)
    y = x_ref[...] + jnp.dot(pick, buf_ref[cur], preferred_element_type=F32)
    o_ref[...] = _rms_norm(y, g_ref[...])


def _combine(x2d, route2d, table, ys, final_norm, n_experts):
    t, d = x2d.shape
    n_tiles = t // ROUTE_ROWS
    slots = _slot_rows(n_experts)
    tab_block = (1, 1, TABLE_WIDTH)
    kernel = functools.partial(_combine_kernel, n_experts=n_experts)
    return pl.pallas_call(
        kernel,
        grid=(n_tiles,),
        in_specs=[pl.BlockSpec(tab_block, lambda i: (i, 0, 0), memory_space=pltpu.SMEM),
                  pl.BlockSpec(tab_block, lambda i: (jnp.minimum(i + 1, n_tiles - 1), 0, 0),
                               memory_space=pltpu.SMEM),
                  pl.BlockSpec((ROUTE_ROWS, d), lambda i: (i, 0)),
                  pl.BlockSpec((ROUTE_ROWS, V7X_LANES), lambda i: (i, 0)),
                  pl.BlockSpec((1, d), lambda i: (0, 0)),
                  pl.BlockSpec(memory_space=pl.ANY)],
        out_specs=pl.BlockSpec((ROUTE_ROWS, d), lambda i: (i, 0)),
        out_shape=jax.ShapeDtypeStruct((t, d), F32),
        scratch_shapes=[pltpu.VMEM((2, slots, d), BF16),
                        pltpu.SemaphoreType.DMA((2,))],
        compiler_params=_params("arbitrary"),
        name="moe_combine",
    )(table, table, x2d, route2d, final_norm, ys)


def kernel(x, attn_norm, attn_w_qkv, attn_sinks, attn_w_o, ffn_norm, ffn_w_gate, ffn_w_up,
           ffn_w_down, pool_norm, pool_w, pool_scale, moe_norm, moe_w_router, moe_w_gate, moe_w_up,
           moe_w_down, final_norm):
    b, s, d = x.shape
    t = b * s
    n_experts = moe_w_router.shape[-1]
    assert s % ATTN_ROWS == 0 and s % ROUTE_ROWS == 0
    assert n_experts <= V7X_SUBLANES and TABLE_FIELDS * n_experts + 2 <= TABLE_WIDTH
    assert attn_norm.shape[0] == 1 and pool_norm.shape[0] == 1

    qkv = _qkv_proj(x.reshape(t, d), attn_norm[0][None], attn_w_qkv[0].astype(BF16))
    x1 = _attention(x, qkv.reshape(b, s, -1), attn_sinks[0].astype(F32), attn_w_o[0].astype(BF16))
    w_router_t = jnp.pad(moe_w_router[0].T, ((0, ROUTE_SUBLANES - n_experts), (0, 0)))
    x3, h3, route, route_t, info = _ffn_pool_route(
        x1.reshape(t, d), s, ffn_norm[0][None], ffn_w_gate[0].astype(BF16), ffn_w_up[0].astype(BF16),
        ffn_w_down[0].astype(BF16), pool_norm[0][None], pool_scale[0][None], moe_norm[0][None],
        pool_w[0].astype(BF16), w_router_t, n_experts)

    n_tiles = t // ROUTE_ROWS
    info = info[:, :n_experts, :].astype(jnp.int32)
    base, padded, lstart = info[..., INFO_BASE], info[..., INFO_COUNT], info[..., INFO_LSTART]
    group_rows = jnp.sum(padded, axis=0)
    group_tiles = -(-group_rows // MOE_ROWS)
    group_start = (jnp.cumsum(group_tiles) - group_tiles) * MOE_ROWS
    max_rows = (t * TOP_K + n_tiles * n_experts * (RUN_ALIGN - 1)
                + n_experts * (MOE_ROWS - RUN_ALIGN))
    n_rows = -(-max_rows // MOE_ROWS) * MOE_ROWS
    per_tile = lambda v: jnp.broadcast_to(v[None, :], (n_tiles, n_experts))
    table = jnp.concatenate(
        [lstart, group_start[None, :] + base, padded,
         per_tile(group_start + group_rows), per_tile(group_tiles * MOE_ROWS - group_rows),
         jnp.sum(padded, axis=1, keepdims=True),
         jnp.broadcast_to(jnp.sum(group_tiles) * MOE_ROWS, (n_tiles, 1)),
         jnp.zeros((n_tiles, TABLE_WIDTH - TABLE_FIELDS * n_experts - 2), jnp.int32)], axis=1)
    table = table.reshape(n_tiles, 1, TABLE_WIDTH)

    xs = _dispatch(h3, route_t, table, n_rows, n_experts)
    plan = _plan_tiles(group_rows, group_tiles, n_rows // MOE_ROWS, WEIGHT_CHUNKS)
    ys = _moe_experts(xs, plan, moe_w_gate[0], moe_w_up[0], moe_w_down[0])
    out = _combine(x3, route, table, ys, final_norm[None],
                   n_experts)
    return out.reshape(b, s, d)
```
